```python
import math
import jax, jax.numpy as jnp
from jax import lax
import numpy as np

D_MODEL = 2048
BATCH = 16
SEQ = 2048
DEPTH = 4

GRID_W = 64
CTX_LEN = 256
N_EVEN = (DEPTH + 1) // 2
N_ODD = DEPTH // 2

DA_HEADS = 8
DA_D = 64
DA_WIDTH = DA_HEADS * 2 * DA_D
HG_HEADS = 8
HG_K = 128
HG_V = 128
HG_WIDTH = HG_HEADS * HG_V
HG_CHUNK = 32
POOL_WIDTH = D_MODEL
POOL_WINDOWS = (2, 4, 8, 16)
POOL_GROUP = POOL_WIDTH // len(POOL_WINDOWS)

ROPE_BASE = 10000.0
Q_BLOCK = 128
EPS = 1e-6

EVEN_SPLITS = (DA_WIDTH, DA_WIDTH, DA_WIDTH, DA_WIDTH,
               HG_HEADS * HG_K, HG_HEADS * HG_K, HG_HEADS * HG_K, HG_WIDTH, HG_WIDTH)
EVEN_IN = sum(EVEN_SPLITS)
EVEN_OUT = DA_WIDTH + HG_WIDTH

kernel_name = "hybrid_diffattn_hgrn2_pool_dit"


def rms_norm(x, g):
    xf = x.astype(jnp.float32)
    y = xf * lax.rsqrt(jnp.mean(xf * xf, axis=-1, keepdims=True) + EPS)
    return (y * g.astype(jnp.float32)).astype(x.dtype)


def axial_rope(x, row, col):
    n_freq = DA_D // 4
    inv = ROPE_BASE ** (-jnp.arange(n_freq, dtype=jnp.float32) / n_freq)

    def rot(xa, pos):
        ang = pos.astype(jnp.float32)[:, None] * inv[None, :]
        cos = jnp.cos(ang)[None, :, None, None, :]
        sin = jnp.sin(ang)[None, :, None, None, :]
        x1, x2 = jnp.split(xa.astype(jnp.float32), 2, axis=-1)
        return jnp.concatenate([x1 * cos - x2 * sin, x2 * cos + x1 * sin], axis=-1)

    half = DA_D // 2
    out = jnp.concatenate([rot(x[..., :half], row), rot(x[..., half:], col)], axis=-1)
    return out.astype(x.dtype)


def diff_attention(q, k, v, lam):
    B, Tq = q.shape[0], q.shape[1]
    nb = Tq // Q_BLOCK
    qb = jnp.moveaxis(q.reshape(B, nb, Q_BLOCK, DA_HEADS, 2, DA_D), 1, 0)
    scale = DA_D ** -0.5

    def block(qi):
        s = jnp.einsum('bqhmd,bkhmd->bhmqk', qi, k, preferred_element_type=jnp.float32) * scale
        p = jax.nn.softmax(s, axis=-1)
        a = p[:, :, 0] - lam * p[:, :, 1]
        return jnp.einsum('bhqk,bkhe->bqhe', a.astype(v.dtype), v)

    o = lax.map(block, qb)
    return jnp.moveaxis(o, 0, 1).reshape(B, Tq, DA_HEADS, 2 * DA_D)


def hgrn_scan(q, k, v, logf, s0):
    B, T = q.shape[0], q.shape[1]
    n = T // HG_CHUNK

    def chunks(a):
        return jnp.moveaxis(a.astype(jnp.float32).reshape(B, n, HG_CHUNK, HG_HEADS, a.shape[-1]), 1, 0)

    tri = jnp.tril(jnp.ones((HG_CHUNK, HG_CHUNK), dtype=bool))[None, :, :, None, None]

    def step(s, inp):
        qc, kc, vc, gc = inp
        b = jnp.cumsum(gc, axis=1)
        inter = jnp.einsum('bchk,bhkv->bchv', qc * jnp.exp(b), s)
        diff = b[:, :, None] - b[:, None, :]
        decay = jnp.exp(jnp.where(tri, diff, -jnp.inf))
        att = jnp.einsum('bthk,btshk,bshk->bhts', qc, decay, kc)
        intra = jnp.einsum('bhts,bshv->bthv', att, vc)
        b_last = b[:, -1]
        s_new = jnp.exp(b_last)[..., None] * s + jnp.einsum(
            'bshk,bshv->bhkv', kc * jnp.exp(b_last[:, None] - b), vc)
        return s_new, inter + intra

    s_fin, o = lax.scan(step, s0, (chunks(q), chunks(k), chunks(v), chunks(logf)))
    return jnp.moveaxis(o, 0, 1).reshape(B, T, HG_HEADS, HG_V), s_fin


def directional_scan(q_c, k_c, v_c, g_c, q_l, k_l, v_l, g_l, reverse):
    fl = (lambda a: jnp.flip(a, axis=1)) if reverse else (lambda a: a)
    s0 = jnp.zeros((q_c.shape[0], HG_HEADS, HG_K, HG_V), jnp.float32)
    o_c, s_c = hgrn_scan(fl(q_c), fl(k_c), fl(v_c), fl(g_c), s0)
    o_l, _ = hgrn_scan(fl(q_l), fl(k_l), fl(v_l), fl(g_l), s_c)
    return fl(o_c), fl(o_l)


def hg_inputs(p, lb):
    B_, T_ = p[4].shape[0], p[4].shape[1]
    hk = lambda a: a.reshape(B_, T_, HG_HEADS, HG_K)
    q = jax.nn.silu(hk(p[4]))
    v = p[7].reshape(B_, T_, HG_HEADS, HG_V)
    dirs = []
    for z, lbd in ((p[5], lb[0]), (p[6], lb[1])):
        z = hk(z).astype(jnp.float32)
        lbd = lbd.reshape(HG_HEADS, HG_K)
        logf = jnp.log(lbd + (1.0 - lbd) * jax.nn.sigmoid(z))
        k = (1.0 - lbd) * jax.nn.sigmoid(-z)
        dirs.append((k, logf))
    return q, v, dirs


def even_mixer(h_lat, h_ctx, w_in, w_out, lam_vecs, subln_g, lb, hg_norm_g, layer_idx, row, col, need_ctx):
    idx = tuple(int(i) for i in np.cumsum(EVEN_SPLITS)[:-1])
    pl = jnp.split(h_lat @ w_in, idx, axis=-1)
    pc = jnp.split(h_ctx @ w_in, idx, axis=-1)

    lam_init = 0.8 - 0.6 * math.exp(-0.3 * layer_idx)
    lv = lam_vecs.astype(jnp.float32)
    lam = jnp.exp(jnp.sum(lv[0] * lv[1])) - jnp.exp(jnp.sum(lv[2] * lv[3])) + lam_init
    qk_shape = lambda a: a.reshape(a.shape[0], a.shape[1], DA_HEADS, 2, DA_D)
    v_shape = lambda a: a.reshape(a.shape[0], a.shape[1], DA_HEADS, 2 * DA_D)
    q_l = axial_rope(qk_shape(pl[0]), row, col)
    k_l = axial_rope(qk_shape(pl[1]), row, col)
    k_c = qk_shape(pc[1])
    v_c = v_shape(pc[2])
    keys = jnp.concatenate([k_c, k_l], axis=1)
    vals = jnp.concatenate([v_c, v_shape(pl[2])], axis=1)

    def da_finish(o, gate):
        o = rms_norm(o.astype(gate.dtype), subln_g) * (1.0 - lam_init)
        return o.reshape(o.shape[0], o.shape[1], DA_WIDTH) * jax.nn.silu(gate)

    a_lat = da_finish(diff_attention(q_l, keys, vals, lam), pl[3])

    q_hl, v_hl, dirs_l = hg_inputs(pl, lb)
    q_hc, v_hc, dirs_c = hg_inputs(pc, lb)
    oc_f, ol_f = directional_scan(q_hc, dirs_c[0][0], v_hc, dirs_c[0][1],
                                  q_hl, dirs_l[0][0], v_hl, dirs_l[0][1], reverse=False)
    oc_b, ol_b = directional_scan(q_hc, dirs_c[1][0], v_hc, dirs_c[1][1],
                                  q_hl, dirs_l[1][0], v_hl, dirs_l[1][1], reverse=True)

    def hg_finish(o, gate):
        o = rms_norm(o.astype(gate.dtype), hg_norm_g)
        return o.reshape(o.shape[0], o.shape[1], HG_WIDTH) * jax.nn.silu(gate)

    b_lat = hg_finish(ol_f + ol_b, pl[8])
    y_lat = jnp.concatenate([a_lat, b_lat], axis=-1) @ w_out

    y_ctx = None
    if need_ctx:
        a_ctx = da_finish(diff_attention(qk_shape(pc[0]), k_c, v_c, lam), pc[3])
        b_ctx = hg_finish(oc_f + oc_b, pc[8])
        y_ctx = jnp.concatenate([a_ctx, b_ctx], axis=-1) @ w_out
    return y_lat, y_ctx


def pool_minus_identity(u):
    B, T, E = u.shape
    ug = u.reshape(B, T, len(POOL_WINDOWS), POOL_GROUP)
    t = np.arange(T)
    outs = []
    for j, w in enumerate(POOL_WINDOWS):
        xg = ug[:, :, j].astype(jnp.float32)
        cs = jnp.concatenate([jnp.zeros((B, 1, POOL_GROUP), jnp.float32), jnp.cumsum(xg, axis=1)], axis=1)
        lo = np.clip(t - w // 2, 0, T)
        hi = np.clip(t + (w - w // 2), 0, T)
        cnt = (hi - lo).astype(np.float32)[None, :, None]
        mean = (cs[:, hi] - cs[:, lo]) / cnt
        outs.append(mean - xg)
    return jnp.stack(outs, axis=2).reshape(B, T, E).astype(u.dtype)


def odd_mixer(h, w_in, w_pool, ls, w_out):
    u, z = jnp.split(h @ w_in, 2, axis=-1)
    r = pool_minus_identity(u)
    B, T = r.shape[0], r.shape[1]
    y = jnp.einsum('btgc,gcd->btgd', r.reshape(B, T, len(POOL_WINDOWS), POOL_GROUP), w_pool)
    y = y.reshape(B, T, POOL_WIDTH) * ls * jax.nn.silu(z)
    return y @ w_out


def setup_inputs(seed: int = 0) -> dict:
    key = jax.random.key(seed)
    ks = jax.random.split(key, 20)
    D = D_MODEL
    nrm = lambda k, shape, s: jax.random.normal(k, shape, jnp.float32) * s
    return {
        "x": nrm(ks[0], (BATCH, SEQ, D), 1.0),
        "c": nrm(ks[1], (BATCH, D), 1.0),
        "ctx": nrm(ks[2], (BATCH, CTX_LEN, D), 1.0),
        "c_ctx": nrm(ks[3], (D,), 1.0),
        "w_mod": nrm(ks[4], (DEPTH, D, 3 * D), 0.5 * D ** -0.5),
        "b_mod": nrm(ks[5], (DEPTH, 3 * D), 0.02),
        "g_pre": 1.0 + nrm(ks[6], (DEPTH, D), 0.02),
        "g_post": 1.0 + nrm(ks[7], (DEPTH, D), 0.02),
        "ev_w_in": nrm(ks[8], (N_EVEN, D, EVEN_IN), D ** -0.5),
        "ev_w_out": nrm(ks[9], (N_EVEN, EVEN_OUT, D), EVEN_OUT ** -0.5),
        "ev_lambda": nrm(ks[10], (N_EVEN, 4, DA_D), 0.1),
        "ev_subln_g": 1.0 + nrm(ks[11], (N_EVEN, 2 * DA_D), 0.02),
        "ev_hg_lb_logits": nrm(ks[12], (N_EVEN, 2, HG_HEADS * HG_K), 1.0),
        "ev_hg_norm_g": 1.0 + nrm(ks[13], (N_EVEN, HG_V), 0.02),
        "od_w_in": nrm(ks[14], (N_ODD, D, 2 * POOL_WIDTH), D ** -0.5),
        "od_w_pool": nrm(ks[15], (N_ODD, len(POOL_WINDOWS), POOL_GROUP, POOL_GROUP), POOL_GROUP ** -0.5),
        "od_scale": 1.0 + nrm(ks[16], (N_ODD, POOL_WIDTH), 0.02),
        "od_w_out": nrm(ks[17], (N_ODD, POOL_WIDTH, D), POOL_WIDTH ** -0.5),
    }


def reference(x, c, ctx, c_ctx, w_mod, b_mod, g_pre, g_post, ev_w_in, ev_w_out, ev_lambda, ev_subln_g,
              ev_hg_lb_logits, ev_hg_norm_g, od_w_in, od_w_pool, od_scale, od_w_out):
    S = x.shape[1]
    ROWS = S // GRID_W
    row = jnp.repeat(jnp.arange(ROWS), GRID_W)
    col = jnp.tile(jnp.arange(GRID_W), ROWS)
    lb_cum = jnp.cumsum(jax.nn.softmax(ev_hg_lb_logits.astype(jnp.float32), axis=0), axis=0)
    lb_all = lb_cum - lb_cum[0]
    silu_c = jax.nn.silu(c)
    silu_cc = jax.nn.silu(c_ctx)
    for l in range(DEPTH):
        even = (l % 2 == 0)
        need_ctx = l < DEPTH - 1
        ctx_active = even or need_ctx
        mod_l = silu_c @ w_mod[l] + b_mod[l]
        sh_l, sc_l, gt_l = jnp.split(mod_l[:, None, :], 3, axis=-1)
        h_lat = rms_norm(x, g_pre[l]) * (1.0 + sc_l) + sh_l
        if ctx_active:
            mod_c = silu_cc @ w_mod[l] + b_mod[l]
            sh_c, sc_c, gt_c = jnp.split(mod_c, 3, axis=-1)
            h_ctx = rms_norm(ctx, g_pre[l]) * (1.0 + sc_c) + sh_c
        if even:
            e = l // 2
            y_lat, y_ctx = even_mixer(h_lat, h_ctx, ev_w_in[e], ev_w_out[e], ev_lambda[e], ev_subln_g[e],
                                      lb_all[e], ev_hg_norm_g[e], l, row, col, need_ctx)
        else:
            o = l // 2
            y_lat = odd_mixer(h_lat, od_w_in[o], od_w_pool[o], od_scale[o], od_w_out[o])
            y_ctx = odd_mixer(h_ctx, od_w_in[o], od_w_pool[o], od_scale[o], od_w_out[o]) if need_ctx else None
        x = x + gt_l * rms_norm(y_lat, g_post[l])
        if need_ctx:
            ctx = ctx + gt_c * rms_norm(y_ctx, g_post[l])
    return x
```

```python
import functools
import math

import jax
import jax.numpy as jnp
from jax import lax
from jax.experimental import pallas as pl
from jax.experimental.pallas import tpu as pltpu

DA_HEADS = 8
DA_D = 64
HEAD_W = 2 * DA_D
HG_HEADS = 8
HG_K = 128
GRID_W = 64
ROPE_BASE = 10000.0
EPS = 1e-6
POOL_WINDOWS = (2, 4, 8, 16)
HG_CHUNK = 16
SUBLANES = 8
NORM_SLAB = 128

F32 = jnp.float32
BF16 = jnp.bfloat16
VMEM_LIMIT = 56 * 1024 * 1024


def _cparams(sem):
    return pltpu.CompilerParams(dimension_semantics=sem, vmem_limit_bytes=VMEM_LIMIT)


def _sigmoid(x):
    return 1.0 / (1.0 + jnp.exp(-x))


def _mod_kernel(c_ref, w_ref, b_ref, o_ref):
    c = c_ref[...]
    s = c * _sigmoid(c)
    o_ref[0] = jnp.dot(s, w_ref[0], preferred_element_type=F32,
                       precision=lax.Precision.HIGHEST) + b_ref[0]


def _modulation(cvec, w_mod, b_mod):
    L, D, N = w_mod.shape
    R = cvec.shape[0]
    tn = 1024 if N % 1024 == 0 else N
    return pl.pallas_call(
        _mod_kernel,
        grid=(L, N // tn),
        in_specs=[pl.BlockSpec((R, D), lambda l, j: (0, 0)),
                  pl.BlockSpec((1, D, tn), lambda l, j: (l, 0, j)),
                  pl.BlockSpec((1, 1, tn), lambda l, j: (l, 0, j))],
        out_specs=pl.BlockSpec((1, R, tn), lambda l, j: (l, 0, j)),
        out_shape=jax.ShapeDtypeStruct((L, R, N), F32),
        compiler_params=_cparams(("parallel", "parallel")),
        name="modulation",
    )(cvec, w_mod, b_mod.reshape(L, 1, N))


def _row_mod(mod_ref, lat_row, ctx_row, row0, tm, n_ctx):
    rows = row0 + lax.broadcasted_iota(jnp.int32, (tm, 1), 0)
    return jnp.where(rows < n_ctx, mod_ref[0, ctx_row:ctx_row + 1, :], mod_ref[0, lat_row:lat_row + 1, :])


def _norm_proj_kernel(x_ref, mod_ref, g_ref, w_ref, o_ref, h_scr, *, tm, n_ctx):
    r = pl.program_id(1)

    @pl.when(pl.program_id(2) == 0)
    def _():
        def slab(k, carry):
            r0 = pl.multiple_of(k * NORM_SLAB, NORM_SLAB)
            x = x_ref[0, pl.ds(r0, NORM_SLAB), :]
            y = x * lax.rsqrt(jnp.mean(x * x, axis=-1, keepdims=True) + EPS) * g_ref[...]
            sh = _row_mod(mod_ref, 0, 3, r * tm + r0, NORM_SLAB, n_ctx)
            sc = _row_mod(mod_ref, 1, 4, r * tm + r0, NORM_SLAB, n_ctx)
            h_scr[pl.ds(r0, NORM_SLAB), :] = (y * (1.0 + sc) + sh).astype(BF16)
            return carry

        lax.fori_loop(0, tm // NORM_SLAB, slab, 0)

    o_ref[0] = jnp.dot(h_scr[...], w_ref[...], preferred_element_type=F32).astype(o_ref.dtype)


def _norm_proj(x, modrows, g, w, out_dtype, n_ctx, tm, tn):
    B, T, D = x.shape
    N = w.shape[1]
    return pl.pallas_call(
        functools.partial(_norm_proj_kernel, tm=tm, n_ctx=n_ctx),
        grid=(B, T // tm, N // tn),
        in_specs=[pl.BlockSpec((1, tm, D), lambda b, r, j: (b, r, 0)),
                  pl.BlockSpec((1, SUBLANES, D), lambda b, r, j: (b, 0, 0)),
                  pl.BlockSpec((1, D), lambda b, r, j: (0, 0)),
                  pl.BlockSpec((D, tn), lambda b, r, j: (0, j))],
        out_specs=pl.BlockSpec((1, tm, tn), lambda b, r, j: (b, r, j)),
        out_shape=jax.ShapeDtypeStruct((B, T, N), out_dtype),
        scratch_shapes=[pltpu.VMEM((tm, D), BF16)],
        compiler_params=_cparams(("parallel", "parallel", "arbitrary")),
        name="norm_proj",
    )(x, modrows, g.reshape(1, D), w)


def _rope(x, cos, sin_signed):
    lane = lax.broadcasted_iota(jnp.int32, x.shape, 1)
    n = x.shape[1]
    partner = jnp.where(lane % 32 < 16, pltpu.roll(x, n - 16, axis=1), pltpu.roll(x, 16, axis=1))
    return x * cos + partner * sin_signed


def _attn_kernel(lam_ref, q_ref, k_ref, v_ref, gate_ref, cq_ref, sq_ref, ck_ref, sk_ref, g_ref,
                 o_ref, kt_scr, *, tq, n_ctx, out_scale):
    i = pl.program_id(2)

    @pl.when(i == 0)
    def _():
        kr = _rope(k_ref[0].astype(F32), ck_ref[...], sk_ref[...])
        kt_scr[...] = kr.T.astype(BF16)

    qr = _rope(q_ref[0].astype(F32), cq_ref[...], sq_ref[...]) * (DA_D ** -0.5)
    first_map = lax.broadcasted_iota(jnp.int32, qr.shape, 1) < DA_D
    q2 = jnp.concatenate([jnp.where(first_map, qr, 0.0), jnp.where(first_map, 0.0, qr)], axis=0).astype(BF16)
    lam = lam_ref[0]

    def attend(nk):
        s = jnp.dot(q2, kt_scr[:, :nk], preferred_element_type=F32)
        e = jnp.exp(s - jnp.max(s, axis=-1, keepdims=True))
        rinv = 1.0 / jnp.sum(e, axis=-1, keepdims=True)
        a = e[:tq] * rinv[:tq] - e[tq:] * (lam * rinv[tq:])
        o = jnp.dot(a.astype(BF16), v_ref[0, :nk, :], preferred_element_type=F32)
        y = o * lax.rsqrt(jnp.mean(o * o, axis=-1, keepdims=True) + EPS) * g_ref[...] * out_scale
        gate = gate_ref[0].astype(F32)
        o_ref[0] = (y * (gate * _sigmoid(gate))).astype(o_ref.dtype)

    @pl.when(i == 0)
    def _():
        attend(n_ctx)

    @pl.when(i != 0)
    def _():
        attend(kt_scr.shape[1])


def _diff_attention(p, lam, cos, sin, subln_g, out_scale, n_ctx):
    B, T, _ = p.shape
    tq = n_ctx
    H = DA_HEADS
    head = lambda part: (lambda b, h, i: (b, i, part * H + h))
    head_all = lambda part: (lambda b, h, i: (b, 0, part * H + h))
    return pl.pallas_call(
        functools.partial(_attn_kernel, tq=tq, n_ctx=n_ctx, out_scale=out_scale),
        grid=(B, H, T // tq),
        in_specs=[pl.BlockSpec(memory_space=pltpu.SMEM),
                  pl.BlockSpec((1, tq, HEAD_W), head(0)),
                  pl.BlockSpec((1, T, HEAD_W), head_all(1)),
                  pl.BlockSpec((1, T, HEAD_W), head_all(2)),
                  pl.BlockSpec((1, tq, HEAD_W), head(3)),
                  pl.BlockSpec((tq, HEAD_W), lambda b, h, i: (i, 0)),
                  pl.BlockSpec((tq, HEAD_W), lambda b, h, i: (i, 0)),
                  pl.BlockSpec((T, HEAD_W), lambda b, h, i: (0, 0)),
                  pl.BlockSpec((T, HEAD_W), lambda b, h, i: (0, 0)),
                  pl.BlockSpec((1, HEAD_W), lambda b, h, i: (0, 0))],
        out_specs=pl.BlockSpec((1, tq, HEAD_W), lambda b, h, i: (b, i, h)),
        out_shape=jax.ShapeDtypeStruct((B, T, H * HEAD_W), BF16),
        scratch_shapes=[pltpu.VMEM((HEAD_W, T), BF16)],
        compiler_params=_cparams(("parallel", "parallel", "arbitrary")),
        name="diff_attention",
    )(lam, p, p, p, p, cos, sin, cos, sin, subln_g.reshape(1, HEAD_W))


def _rope_tables(n_ctx, seq):
    n_freq = DA_D // 4
    inv = ROPE_BASE ** (-jnp.arange(n_freq, dtype=F32) / n_freq)
    t = jnp.arange(seq)
    lane = jnp.arange(HEAD_W)
    pos = jnp.where((lane % DA_D < DA_D // 2)[None, :], (t // GRID_W)[:, None], (t % GRID_W)[:, None]).astype(F32)
    ang = pos * inv[lane % n_freq][None, :]
    sign = jnp.where(lane % 32 < 16, -1.0, 1.0).astype(F32)[None, :]
    cos = jnp.concatenate([jnp.ones((n_ctx, HEAD_W), F32), jnp.cos(ang)], axis=0)
    sin = jnp.concatenate([jnp.zeros((n_ctx, HEAD_W), F32), jnp.sin(ang) * sign], axis=0)
    return cos, sin


def _chunk_cumsum(g, reverse):
    n = g.shape[0]
    row = lax.broadcasted_iota(jnp.int32, g.shape, 0)
    b = g
    sh = 1
    while sh < n:
        if reverse:
            b = b + jnp.where(row < n - sh, pltpu.roll(b, n - sh, axis=0), 0.0)
        else:
            b = b + jnp.where(row >= sh, pltpu.roll(b, sh, axis=0), 0.0)
        sh *= 2
    return b


def _intra_chunk(qq, kk, b, v, reverse):
    C = qq.shape[0]
    half = SUBLANES
    sub = lax.broadcasted_iota(jnp.int32, (half, qq.shape[1]), 0)
    parts = [jnp.zeros((half, v.shape[1]), F32) for _ in range(C // half)]
    for s in range(C):
        sb = s // half
        blocks = range(sb + 1) if reverse else range(sb, C // half)
        for tb in blocks:
            rows = slice(tb * half, (tb + 1) * half)
            w = jnp.exp(b[rows] - b[s:s + 1]) * (qq[rows] * kk[s:s + 1])
            if tb == sb:
                keep = (sub <= s - sb * half) if reverse else (sub >= s - sb * half)
                w = jnp.where(keep, w, 0.0)
            parts[tb] = parts[tb] + jnp.sum(w, axis=-1, keepdims=True) * v[s:s + 1]
    return jnp.concatenate(parts, axis=0)


def _hgrn_kernel(q_ref, v_ref, gate_ref, zf_ref, zb_ref, lb_ref, g_ref, o_ref,
                 qt_scr, kt_scr, oi_scr, dl_scr, st_scr, *, n_ctx):
    T = q_ref.shape[1]
    C = HG_CHUNK
    nc = T // C
    nc_ctx = n_ctx // C

    def prep(c, carry):
        r0 = pl.multiple_of(c * C, C)
        hq = q_ref[0, pl.ds(r0, C), :].astype(F32)
        qq = hq * _sigmoid(hq)
        v = v_ref[0, pl.ds(r0, C), :].astype(F32)
        for d, z_ref in enumerate((zf_ref, zb_ref)):
            reverse = d == 1
            z = z_ref[0, pl.ds(r0, C), :]
            lb = lb_ref[0, d:d + 1, :]
            e = jnp.exp(-jnp.abs(z))
            inv = 1.0 / (1.0 + e)
            sig = jnp.where(z >= 0, inv, e * inv)
            nsig = jnp.where(z >= 0, e * inv, inv)
            kk = (1.0 - lb) * nsig
            b = _chunk_cumsum(jnp.log(lb + (1.0 - lb) * sig), reverse)
            b_last = b[0:1] if reverse else b[C - 1:C]
            qt_scr[d, pl.ds(r0, C), :] = (qq * jnp.exp(b)).astype(BF16)
            kt_scr[d, pl.ds(r0, C), :] = (kk * jnp.exp(b_last - b)).astype(BF16)
            dl_scr[d, pl.ds(c, 1), :] = jnp.exp(b_last)
            oi_scr[d, pl.ds(r0, C), :] = _intra_chunk(qq, kk, b, v, reverse)
        return carry

    lax.fori_loop(0, nc, prep, 0)

    st_scr[...] = jnp.zeros_like(st_scr)

    def scan(i, carry):
        cb = jnp.where(i < nc_ctx, nc_ctx - 1 - i, nc + nc_ctx - 1 - i)
        for d, c in enumerate((i, cb)):
            r0 = pl.multiple_of(c * C, C)
            st = st_scr[d]
            inter = lax.dot_general(qt_scr[d, pl.ds(r0, C), :], st.astype(BF16),
                                    (((1,), (1,)), ((), ())), preferred_element_type=F32)
            oi_scr[d, pl.ds(r0, C), :] = oi_scr[d, pl.ds(r0, C), :] + inter
            upd = lax.dot_general(v_ref[0, pl.ds(r0, C), :], kt_scr[d, pl.ds(r0, C), :],
                                  (((0,), (0,)), ((), ())), preferred_element_type=F32)
            st_scr[d] = dl_scr[d, pl.ds(c, 1), :] * st + upd
        return carry

    lax.fori_loop(0, nc, scan, 0)

    o = oi_scr[0] + oi_scr[1]
    y = o * lax.rsqrt(jnp.mean(o * o, axis=-1, keepdims=True) + EPS) * g_ref[...]
    gate = gate_ref[0].astype(F32)
    o_ref[0] = (y * (gate * _sigmoid(gate))).astype(o_ref.dtype)


def _hgrn(p, pf, lb, norm_g, n_ctx, part0):
    B, T, _ = p.shape
    H = HG_HEADS
    nc = T // HG_CHUNK
    part = lambda k: (lambda b, h: (b, 0, k * H + h))
    blk = (1, T, HEAD_W)
    return pl.pallas_call(
        functools.partial(_hgrn_kernel, n_ctx=n_ctx),
        grid=(B, H),
        in_specs=[pl.BlockSpec(blk, part(part0)), pl.BlockSpec(blk, part(part0 + 1)),
                  pl.BlockSpec(blk, part(part0 + 2)),
                  pl.BlockSpec(blk, part(0)), pl.BlockSpec(blk, part(1)),
                  pl.BlockSpec((1, 2, HG_K), lambda b, h: (h, 0, 0)),
                  pl.BlockSpec((1, HEAD_W), lambda b, h: (0, 0))],
        out_specs=pl.BlockSpec(blk, lambda b, h: (b, 0, h)),
        out_shape=jax.ShapeDtypeStruct((B, T, H * HEAD_W), BF16),
        scratch_shapes=[pltpu.VMEM((2, T, HG_K), BF16), pltpu.VMEM((2, T, HG_K), BF16),
                        pltpu.VMEM((2, T, HEAD_W), F32), pltpu.VMEM((2, nc, HG_K), F32),
                        pltpu.VMEM((2, HEAD_W, HG_K), F32)],
        compiler_params=_cparams(("parallel", "parallel")),
        name="hgrn2",
    )(p, p, p, pf, pf, lb, norm_g.reshape(1, HEAD_W))


def _shift_rows(x, d):
    n = x.shape[0]
    row = lax.broadcasted_iota(jnp.int32, (n, 1), 0)
    rolled = pltpu.roll(x, (-d) % n, axis=0)
    keep = (row + d >= 0) & (row + d < n)
    return jnp.where(keep, rolled, 0.0)


def _pool_minus_identity(x, w):
    n = x.shape[0]
    ahead = w - w // 2
    behind = w // 2
    fwd = x
    span = 1
    while span < ahead:
        fwd = fwd + _shift_rows(fwd, span)
        span *= 2
    bwd = x
    span = 1
    while span < behind:
        bwd = bwd + _shift_rows(bwd, -span)
        span *= 2
    total = fwd + _shift_rows(bwd, -1)
    row = lax.broadcasted_iota(jnp.int32, (n, 1), 0)
    cnt = jnp.minimum(row + ahead, n) - jnp.maximum(row - behind, 0)
    return total / cnt.astype(F32) - x


def _pool_kernel(u_ref, z_ref, w_ref, ls_ref, o_ref, r_scr, *, n_ctx):
    j = pl.program_id(1)
    T = u_ref.shape[1]
    for jj, win in enumerate(POOL_WINDOWS):
        @pl.when(j == jj)
        def _(win=win):
            for lo, hi in ((0, n_ctx), (n_ctx, T)):
                r_scr[lo:hi, :] = _pool_minus_identity(u_ref[0, lo:hi, :].astype(F32), win).astype(BF16)

    y = jnp.dot(r_scr[...], w_ref[0], preferred_element_type=F32)
    z = z_ref[0].astype(F32)
    o_ref[0] = (y * ls_ref[...] * (z * _sigmoid(z))).astype(o_ref.dtype)


def _pool_mixer(p, w_pool, ls, n_ctx):
    B, T, two_e = p.shape
    G, gw, _ = w_pool.shape
    assert G == len(POOL_WINDOWS) and two_e == 2 * G * gw
    return pl.pallas_call(
        functools.partial(_pool_kernel, n_ctx=n_ctx),
        grid=(B, G),
        in_specs=[pl.BlockSpec((1, T, gw), lambda b, j: (b, 0, j)),
                  pl.BlockSpec((1, T, gw), lambda b, j: (b, 0, G + j)),
                  pl.BlockSpec((1, gw, gw), lambda b, j: (j, 0, 0)),
                  pl.BlockSpec((1, gw), lambda b, j: (0, j))],
        out_specs=pl.BlockSpec((1, T, gw), lambda b, j: (b, 0, j)),
        out_shape=jax.ShapeDtypeStruct((B, T, G * gw), BF16),
        scratch_shapes=[pltpu.VMEM((T, gw), BF16)],
        compiler_params=_cparams(("parallel", "parallel")),
        name="pool_mixer",
    )(p, p, w_pool, ls.reshape(1, G * gw))


def _out_proj_kernel(a_ref, b_ref, w_ref, x_ref, mod_ref, g_ref, o_ref, *, tm, n_ctx, row_off):
    ka = a_ref.shape[2]
    y = jnp.dot(a_ref[0], w_ref[:ka, :], preferred_element_type=F32)
    y = y + jnp.dot(b_ref[0], w_ref[ka:, :], preferred_element_type=F32)
    yn = y * lax.rsqrt(jnp.mean(y * y, axis=-1, keepdims=True) + EPS) * g_ref[...]
    gt = _row_mod(mod_ref, 2, 5, row_off + pl.program_id(1) * tm, tm, n_ctx)
    o_ref[0] = x_ref[0] + gt * yn


def _out_proj(a, b, a_blk, b_blk, w, x, modrows, g, n_ctx, tm, latents_only):
    B, T, D = x.shape
    half_k = w.shape[0] // 2
    off = n_ctx // tm if latents_only else 0
    rows_out = T - n_ctx if latents_only else T
    return pl.pallas_call(
        functools.partial(_out_proj_kernel, tm=tm, n_ctx=n_ctx, row_off=off * tm),
        grid=(B, rows_out // tm),
        in_specs=[pl.BlockSpec((1, tm, half_k), lambda bb, r: (bb, r + off, a_blk)),
                  pl.BlockSpec((1, tm, half_k), lambda bb, r: (bb, r + off, b_blk)),
                  pl.BlockSpec(w.shape, lambda bb, r: (0, 0)),
                  pl.BlockSpec((1, tm, D), lambda bb, r: (bb, r + off, 0)),
                  pl.BlockSpec((1, SUBLANES, D), lambda bb, r: (bb, 0, 0)),
                  pl.BlockSpec((1, D), lambda bb, r: (0, 0))],
        out_specs=pl.BlockSpec((1, tm, D), lambda bb, r: (bb, r, 0)),
        out_shape=jax.ShapeDtypeStruct((B, rows_out, D), F32),
        compiler_params=_cparams(("parallel", "parallel")),
        name="out_proj",
    )(a, b, w, x, modrows, g.reshape(1, D))


def _row_tile(T, cap):
    best = NORM_SLAB
    for t in range(NORM_SLAB, min(T, cap) + 1, NORM_SLAB):
        if T % t == 0:
            best = t
    return best


def _col_tile(N, cap):
    best = 128
    for t in range(128, min(N, cap) + 1, 128):
        if N % t == 0:
            best = t
    return best


def kernel(x, c, ctx, c_ctx, w_mod, b_mod, g_pre, g_post, ev_w_in, ev_w_out, ev_lambda, ev_subln_g,
           ev_hg_lb_logits, ev_hg_norm_g, od_w_in, od_w_pool, od_scale, od_w_out):
    B, S, D = x.shape
    n_ctx = ctx.shape[1]
    depth = w_mod.shape[0]
    T = n_ctx + S
    W = DA_HEADS * HEAD_W

    xc = jnp.concatenate([ctx, x], axis=1)

    n_rows = -(-(B + 1) // SUBLANES) * SUBLANES
    cvec = jnp.zeros((n_rows, D), F32).at[:B].set(c).at[B].set(c_ctx)
    mod = _modulation(cvec, w_mod, b_mod)

    lb_cum = jnp.cumsum(jax.nn.softmax(ev_hg_lb_logits.astype(F32), axis=0), axis=0)
    lb_all = lb_cum - lb_cum[0]
    cos, sin = _rope_tables(n_ctx, S)

    tm_in = _row_tile(T, 1152)
    tm_out = _row_tile(T, 384)

    for l in range(depth):
        last = l == depth - 1
        ml = mod[l]
        lat = ml[:B].reshape(B, 3, D)
        cx = jnp.broadcast_to(ml[B].reshape(1, 3, D), (B, 3, D))
        modrows = jnp.concatenate([lat, cx, jnp.zeros((B, SUBLANES - 6, D), F32)], axis=1)

        if l % 2 == 0:
            e = l // 2
            w_in = ev_w_in[e]
            w_main = jnp.concatenate([w_in[:, :5 * W], w_in[:, 7 * W:]], axis=1).astype(BF16)
            w_f = w_in[:, 5 * W:7 * W].astype(BF16)
            p = _norm_proj(xc, modrows, g_pre[l], w_main, BF16, n_ctx, tm_in, _col_tile(w_main.shape[1], 1024))
            pf = _norm_proj(xc, modrows, g_pre[l], w_f, F32, n_ctx, tm_in, _col_tile(w_f.shape[1], 1024))

            lam_init = 0.8 - 0.6 * math.exp(-0.3 * l)
            lv = ev_lambda[e].astype(F32)
            lam = jnp.exp(jnp.sum(lv[0] * lv[1])) - jnp.exp(jnp.sum(lv[2] * lv[3])) + lam_init
            a = _diff_attention(p, lam.reshape(1), cos, sin, ev_subln_g[e], 1.0 - lam_init, n_ctx)
            lb = lb_all[e].reshape(2, HG_HEADS, HG_K).transpose(1, 0, 2)
            bh = _hgrn(p, pf, lb, ev_hg_norm_g[e], n_ctx, 4)
            w_out = ev_w_out[e].astype(BF16)
            ya, yb, ia, ib = a, bh, 0, 0
        else:
            o = l // 2
            p = _norm_proj(xc, modrows, g_pre[l], od_w_in[o].astype(BF16), BF16, n_ctx, tm_in,
                           _col_tile(od_w_in.shape[2], 1024))
            y = _pool_mixer(p, od_w_pool[o].astype(BF16), od_scale[o], n_ctx)
            w_out = od_w_out[o].astype(BF16)
            ya, yb, ia, ib = y, y, 0, 1

        xc = _out_proj(ya, yb, ia, ib, w_out, xc, modrows, g_post[l], n_ctx,
                       n_ctx if last else tm_out, latents_only=last)
    return xc
```

```python
import functools
import math

import jax
import jax.numpy as jnp
from jax import lax
from jax.experimental import pallas as pl
from jax.experimental.pallas import tpu as pltpu

DA_HEADS = 8
DA_D = 64
HEAD_W = 2 * DA_D
HG_HEADS = 8
HG_K = 128
GRID_W = 64
ROPE_BASE = 10000.0
EPS = 1e-6
POOL_WINDOWS = (2, 4, 8, 16)
HG_CHUNK = 16
HG_GROUP = 8
HG_PREP_UNROLL = 4
SUBLANES = 8
NORM_SLAB = 128

F32 = jnp.float32
BF16 = jnp.bfloat16
VMEM_LIMIT = 56 * 1024 * 1024


def _cparams(sem):
    return pltpu.CompilerParams(dimension_semantics=sem, vmem_limit_bytes=VMEM_LIMIT)


def _sigmoid(x):
    return 1.0 / (1.0 + jnp.exp(-x))


def _mod_kernel(c_ref, w_ref, b_ref, o_ref):
    c = c_ref[...]
    s = c * _sigmoid(c)
    o_ref[0] = jnp.dot(s, w_ref[0], preferred_element_type=F32,
                       precision=lax.Precision.HIGHEST) + b_ref[0]


def _modulation(cvec, w_mod, b_mod):
    L, D, N = w_mod.shape
    R = cvec.shape[0]
    tn = 1024 if N % 1024 == 0 else N
    return pl.pallas_call(
        _mod_kernel,
        grid=(L, N // tn),
        in_specs=[pl.BlockSpec((R, D), lambda l, j: (0, 0)),
                  pl.BlockSpec((1, D, tn), lambda l, j: (l, 0, j)),
                  pl.BlockSpec((1, 1, tn), lambda l, j: (l, 0, j))],
        out_specs=pl.BlockSpec((1, R, tn), lambda l, j: (l, 0, j)),
        out_shape=jax.ShapeDtypeStruct((L, R, N), F32),
        compiler_params=_cparams(("parallel", "parallel")),
        name="modulation",
    )(cvec, w_mod, b_mod.reshape(L, 1, N))


def _row_mod(mod_ref, lat_row, ctx_row, row0, tm, n_ctx):
    rows = row0 + lax.broadcasted_iota(jnp.int32, (tm, 1), 0)
    return jnp.where(rows < n_ctx, mod_ref[0, ctx_row:ctx_row + 1, :], mod_ref[0, lat_row:lat_row + 1, :])


def _norm_proj_kernel(x_ref, mod_ref, g_ref, w_ref, o_ref, h_scr, *, tm, n_ctx):
    r = pl.program_id(1)

    @pl.when(pl.program_id(2) == 0)
    def _():
        def slab(k, carry):
            r0 = pl.multiple_of(k * NORM_SLAB, NORM_SLAB)
            x = x_ref[0, pl.ds(r0, NORM_SLAB), :]
            y = x * lax.rsqrt(jnp.mean(x * x, axis=-1, keepdims=True) + EPS) * g_ref[...]
            sh = _row_mod(mod_ref, 0, 3, r * tm + r0, NORM_SLAB, n_ctx)
            sc = _row_mod(mod_ref, 1, 4, r * tm + r0, NORM_SLAB, n_ctx)
            h_scr[pl.ds(r0, NORM_SLAB), :] = (y * (1.0 + sc) + sh).astype(BF16)
            return carry

        lax.fori_loop(0, tm // NORM_SLAB, slab, 0)

    o_ref[0] = jnp.dot(h_scr[...], w_ref[...], preferred_element_type=F32).astype(o_ref.dtype)


def _norm_proj(x, modrows, g, w, out_dtype, n_ctx, tm, tn):
    B, T, D = x.shape
    N = w.shape[1]
    return pl.pallas_call(
        functools.partial(_norm_proj_kernel, tm=tm, n_ctx=n_ctx),
        grid=(B, T // tm, N // tn),
        in_specs=[pl.BlockSpec((1, tm, D), lambda b, r, j: (b, r, 0)),
                  pl.BlockSpec((1, SUBLANES, D), lambda b, r, j: (b, 0, 0)),
                  pl.BlockSpec((1, D), lambda b, r, j: (0, 0)),
                  pl.BlockSpec((D, tn), lambda b, r, j: (0, j))],
        out_specs=pl.BlockSpec((1, tm, tn), lambda b, r, j: (b, r, j)),
        out_shape=jax.ShapeDtypeStruct((B, T, N), out_dtype),
        scratch_shapes=[pltpu.VMEM((tm, D), BF16)],
        compiler_params=_cparams(("parallel", "parallel", "arbitrary")),
        name="norm_proj",
    )(x, modrows, g.reshape(1, D), w)


def _rope(x, cos, sin_signed):
    lane = lax.broadcasted_iota(jnp.int32, x.shape, 1)
    n = x.shape[1]
    partner = jnp.where(lane % 32 < 16, pltpu.roll(x, n - 16, axis=1), pltpu.roll(x, 16, axis=1))
    return x * cos + partner * sin_signed


def _attn_kernel(lam_ref, q_ref, k_ref, v_ref, gate_ref, cq_ref, sq_ref, ck_ref, sk_ref, g_ref,
                 o_ref, kt_scr, *, tq, n_ctx, out_scale):
    i = pl.program_id(2)

    @pl.when(i == 0)
    def _():
        kr = _rope(k_ref[0].astype(F32), ck_ref[...], sk_ref[...])
        kt_scr[...] = kr.T.astype(BF16)

    qr = _rope(q_ref[0].astype(F32), cq_ref[...], sq_ref[...]) * (DA_D ** -0.5)
    first_map = lax.broadcasted_iota(jnp.int32, qr.shape, 1) < DA_D
    q2 = jnp.concatenate([jnp.where(first_map, qr, 0.0), jnp.where(first_map, 0.0, qr)], axis=0).astype(BF16)
    lam = lam_ref[0]

    def attend(nk):
        s = jnp.dot(q2, kt_scr[:, :nk], preferred_element_type=F32)
        e = jnp.exp(s - jnp.max(s, axis=-1, keepdims=True))
        rinv = 1.0 / jnp.sum(e, axis=-1, keepdims=True)
        a = e[:tq] * rinv[:tq] - e[tq:] * (lam * rinv[tq:])
        o = jnp.dot(a.astype(BF16), v_ref[0, :nk, :], preferred_element_type=F32)
        y = o * lax.rsqrt(jnp.mean(o * o, axis=-1, keepdims=True) + EPS) * g_ref[...] * out_scale
        gate = gate_ref[0].astype(F32)
        o_ref[0] = (y * (gate * _sigmoid(gate))).astype(o_ref.dtype)

    @pl.when(i == 0)
    def _():
        attend(n_ctx)

    @pl.when(i != 0)
    def _():
        attend(kt_scr.shape[1])


def _diff_attention(p, lam, cos, sin, subln_g, out_scale, n_ctx):
    B, T, _ = p.shape
    tq = n_ctx
    H = DA_HEADS
    head = lambda part: (lambda b, h, i: (b, i, part * H + h))
    head_all = lambda part: (lambda b, h, i: (b, 0, part * H + h))
    return pl.pallas_call(
        functools.partial(_attn_kernel, tq=tq, n_ctx=n_ctx, out_scale=out_scale),
        grid=(B, H, T // tq),
        in_specs=[pl.BlockSpec(memory_space=pltpu.SMEM),
                  pl.BlockSpec((1, tq, HEAD_W), head(0)),
                  pl.BlockSpec((1, T, HEAD_W), head_all(1)),
                  pl.BlockSpec((1, T, HEAD_W), head_all(2)),
                  pl.BlockSpec((1, tq, HEAD_W), head(3)),
                  pl.BlockSpec((tq, HEAD_W), lambda b, h, i: (i, 0)),
                  pl.BlockSpec((tq, HEAD_W), lambda b, h, i: (i, 0)),
                  pl.BlockSpec((T, HEAD_W), lambda b, h, i: (0, 0)),
                  pl.BlockSpec((T, HEAD_W), lambda b, h, i: (0, 0)),
                  pl.BlockSpec((1, HEAD_W), lambda b, h, i: (0, 0))],
        out_specs=pl.BlockSpec((1, tq, HEAD_W), lambda b, h, i: (b, i, h)),
        out_shape=jax.ShapeDtypeStruct((B, T, H * HEAD_W), BF16),
        scratch_shapes=[pltpu.VMEM((HEAD_W, T), BF16)],
        compiler_params=_cparams(("parallel", "parallel", "arbitrary")),
        name="diff_attention",
    )(lam, p, p, p, p, cos, sin, cos, sin, subln_g.reshape(1, HEAD_W))


def _rope_tables(n_ctx, seq):
    n_freq = DA_D // 4
    inv = ROPE_BASE ** (-jnp.arange(n_freq, dtype=F32) / n_freq)
    t = jnp.arange(seq)
    lane = jnp.arange(HEAD_W)
    pos = jnp.where((lane % DA_D < DA_D // 2)[None, :], (t // GRID_W)[:, None], (t % GRID_W)[:, None]).astype(F32)
    ang = pos * inv[lane % n_freq][None, :]
    sign = jnp.where(lane % 32 < 16, -1.0, 1.0).astype(F32)[None, :]
    cos = jnp.concatenate([jnp.ones((n_ctx, HEAD_W), F32), jnp.cos(ang)], axis=0)
    sin = jnp.concatenate([jnp.zeros((n_ctx, HEAD_W), F32), jnp.sin(ang) * sign], axis=0)
    return cos, sin


def _chunk_cumsum(g, reverse):
    n = g.shape[0]
    row = lax.broadcasted_iota(jnp.int32, g.shape, 0)
    b = g
    sh = 1
    while sh < n:
        if reverse:
            b = b + jnp.where(row < n - sh, pltpu.roll(b, n - sh, axis=0), 0.0)
        else:
            b = b + jnp.where(row >= sh, pltpu.roll(b, sh, axis=0), 0.0)
        sh *= 2
    return b


def _score_tiles(qq, b2, c2, reverse):
    C = qq.shape[0]
    half = SUBLANES
    sub = lax.broadcasted_iota(jnp.int32, (half, qq.shape[1]), 0)
    tiles = []
    for s in range(C):
        sb = s // half
        blocks = range(sb + 1) if reverse else range(sb, C // half)
        for tb in blocks:
            rows = slice(tb * half, (tb + 1) * half)
            w = qq[rows] * jnp.exp2(b2[rows] - c2[s:s + 1])
            if tb == sb:
                keep = (sub <= s - sb * half) if reverse else (sub >= s - sb * half)
                w = jnp.where(keep, w, 0.0)
            tiles.append((s, tb, w))
    return tiles


def _lane_sums(tiles, ones_ref):
    n = len(tiles)
    assert n % 2 == 0
    lhs = jnp.concatenate([jnp.concatenate([tiles[2 * i], tiles[2 * i + 1]], axis=1) for i in range(n // 2)], axis=0)
    r = jnp.dot(lhs.astype(BF16), ones_ref[...], preferred_element_type=F32)
    w = tiles[0].shape[1]
    out = []
    for i in range(n // 2):
        rows = slice(i * SUBLANES, (i + 1) * SUBLANES)
        out += [r[rows, :w], r[rows, w:]]
    return out


def _hgrn_kernel(q_ref, v_ref, gate_ref, zf_ref, zb_ref, lb_ref, g_ref, o_ref,
                 qt_scr, oi_scr, ox_scr, dl_scr, u_scr, st_scr, ones_scr, *, n_ctx):
    T = q_ref.shape[1]
    C = HG_CHUNK
    K = HG_K
    nc = T // C
    nc_ctx = n_ctx // C
    same_half = (lax.broadcasted_iota(jnp.int32, (2 * K, 2 * K), 0) < K) == (lax.broadcasted_iota(jnp.int32, (2 * K, 2 * K), 1) < K)
    ones_scr[...] = jnp.where(same_half, 1.0, 0.0).astype(BF16)

    def prep(c, carry):
        r0 = pl.multiple_of(c * C, C)
        hq = q_ref[0, pl.ds(r0, C), :].astype(F32)
        qq = hq * _sigmoid(hq)
        vb = v_ref[0, pl.ds(r0, C), :]
        v = vb.astype(F32)
        tiles, keys = [], []
        for d, z_ref in enumerate((zf_ref, zb_ref)):
            reverse = d == 1
            z = z_ref[0, pl.ds(r0, C), :]
            lb = lb_ref[0, d:d + 1, :]
            e = jnp.exp(-jnp.abs(z))
            inv = 1.0 / (1.0 + e)
            sig = jnp.where(z >= 0, inv, e * inv)
            nsig = jnp.where(z >= 0, e * inv, inv)
            lk2 = jnp.log2(1.0 - lb) + jnp.log2(nsig)
            b2 = _chunk_cumsum(jnp.log2(lb + (1.0 - lb) * sig), reverse)
            b2_last = b2[0:1] if reverse else b2[C - 1:C]
            qt_scr[d, pl.ds(r0, C), :] = (qq * jnp.exp2(b2)).astype(BF16)
            keys.append(jnp.exp2(lk2 + (b2_last - b2)).astype(BF16))
            dl_scr[d, pl.ds(c, 1), :] = jnp.exp2(b2_last)
            tiles.append(_score_tiles(qq, b2, b2 - lk2, reverse))
        sums = _lane_sums([w for per_dir in tiles for _, _, w in per_dir], ones_scr)
        k = 0
        for d in range(2):
            parts = [jnp.zeros((SUBLANES, v.shape[1]), F32) for _ in range(C // SUBLANES)]
            for s, tb, _ in tiles[d]:
                parts[tb] = parts[tb] + sums[k] * v[s:s + 1]
                k += 1
            oi_scr[d, pl.ds(r0, C), :] = jnp.concatenate(parts, axis=0)
        upd = lax.dot_general(vb, jnp.concatenate(keys, axis=1), (((0,), (0,)), ((), ())),
                              preferred_element_type=F32)
        u_scr[0, c] = upd[:, :K]
        u_scr[1, c] = upd[:, K:]
        return carry

    lax.fori_loop(0, nc, prep, 0, unroll=HG_PREP_UNROLL)

    st_scr[...] = jnp.zeros_like(st_scr)

    def scan(m, carry):
        i0 = m * HG_GROUP
        cb0 = jnp.where(i0 < nc_ctx, nc_ctx - 1 - i0, nc + nc_ctx - 1 - i0)
        for j in range(HG_GROUP):
            for d, c in enumerate((i0 + j, cb0 - j)):
                r0 = pl.multiple_of(c * C, C)
                st = st_scr[d]
                ox_scr[d, pl.ds(r0, C), :] = lax.dot_general(qt_scr[d, pl.ds(r0, C), :], st.astype(BF16),
                                                             (((1,), (1,)), ((), ())), preferred_element_type=F32)
                st_scr[d] = dl_scr[d, pl.ds(c, 1), :] * st + u_scr[d, c]
        return carry

    assert nc % HG_GROUP == 0 and nc_ctx % HG_GROUP == 0
    lax.fori_loop(0, nc // HG_GROUP, scan, 0)

    o = (oi_scr[0] + ox_scr[0]) + (oi_scr[1] + ox_scr[1])
    y = o * lax.rsqrt(jnp.mean(o * o, axis=-1, keepdims=True) + EPS) * g_ref[...]
    gate = gate_ref[0].astype(F32)
    o_ref[0] = (y * (gate * _sigmoid(gate))).astype(o_ref.dtype)


def _hgrn(p, pf, lb, norm_g, n_ctx, part0):
    B, T, _ = p.shape
    H = HG_HEADS
    nc = T // HG_CHUNK
    part = lambda k: (lambda b, h: (b, 0, k * H + h))
    blk = (1, T, HEAD_W)
    return pl.pallas_call(
        functools.partial(_hgrn_kernel, n_ctx=n_ctx),
        grid=(B, H),
        in_specs=[pl.BlockSpec(blk, part(part0)), pl.BlockSpec(blk, part(part0 + 1)),
                  pl.BlockSpec(blk, part(part0 + 2)),
                  pl.BlockSpec(blk, part(0)), pl.BlockSpec(blk, part(1)),
                  pl.BlockSpec((1, 2, HG_K), lambda b, h: (h, 0, 0)),
                  pl.BlockSpec((1, HEAD_W), lambda b, h: (0, 0))],
        out_specs=pl.BlockSpec(blk, lambda b, h: (b, 0, h)),
        out_shape=jax.ShapeDtypeStruct((B, T, H * HEAD_W), BF16),
        scratch_shapes=[pltpu.VMEM((2, T, HG_K), BF16),
                        pltpu.VMEM((2, T, HEAD_W), F32), pltpu.VMEM((2, T, HEAD_W), F32),
                        pltpu.VMEM((2, nc, HG_K), F32),
                        pltpu.VMEM((2, nc, HEAD_W, HG_K), F32),
                        pltpu.VMEM((2, HEAD_W, HG_K), F32),
                        pltpu.VMEM((2 * HG_K, 2 * HG_K), BF16)],
        compiler_params=_cparams(("parallel", "parallel")),
        name="hgrn2",
    )(p, p, p, pf, pf, lb, norm_g.reshape(1, HEAD_W))


def _shift_rows(x, d):
    n = x.shape[0]
    row = lax.broadcasted_iota(jnp.int32, (n, 1), 0)
    rolled = pltpu.roll(x, (-d) % n, axis=0)
    keep = (row + d >= 0) & (row + d < n)
    return jnp.where(keep, rolled, 0.0)


def _pool_minus_identity(x, w):
    n = x.shape[0]
    ahead = w - w // 2
    behind = w // 2
    fwd = x
    span = 1
    while span < ahead:
        fwd = fwd + _shift_rows(fwd, span)
        span *= 2
    bwd = x
    span = 1
    while span < behind:
        bwd = bwd + _shift_rows(bwd, -span)
        span *= 2
    total = fwd + _shift_rows(bwd, -1)
    row = lax.broadcasted_iota(jnp.int32, (n, 1), 0)
    cnt = jnp.minimum(row + ahead, n) - jnp.maximum(row - behind, 0)
    return total / cnt.astype(F32) - x


def _pool_kernel(u_ref, z_ref, w_ref, ls_ref, o_ref, r_scr, *, n_ctx):
    j = pl.program_id(1)
    T = u_ref.shape[1]
    for jj, win in enumerate(POOL_WINDOWS):
        @pl.when(j == jj)
        def _(win=win):
            for lo, hi in ((0, n_ctx), (n_ctx, T)):
                r_scr[lo:hi, :] = _pool_minus_identity(u_ref[0, lo:hi, :].astype(F32), win).astype(BF16)

    y = jnp.dot(r_scr[...], w_ref[0], preferred_element_type=F32)
    z = z_ref[0].astype(F32)
    o_ref[0] = (y * ls_ref[...] * (z * _sigmoid(z))).astype(o_ref.dtype)


def _pool_mixer(p, w_pool, ls, n_ctx):
    B, T, two_e = p.shape
    G, gw, _ = w_pool.shape
    assert G == len(POOL_WINDOWS) and two_e == 2 * G * gw
    return pl.pallas_call(
        functools.partial(_pool_kernel, n_ctx=n_ctx),
        grid=(B, G),
        in_specs=[pl.BlockSpec((1, T, gw), lambda b, j: (b, 0, j)),
                  pl.BlockSpec((1, T, gw), lambda b, j: (b, 0, G + j)),
                  pl.BlockSpec((1, gw, gw), lambda b, j: (j, 0, 0)),
                  pl.BlockSpec((1, gw), lambda b, j: (0, j))],
        out_specs=pl.BlockSpec((1, T, gw), lambda b, j: (b, 0, j)),
        out_shape=jax.ShapeDtypeStruct((B, T, G * gw), BF16),
        scratch_shapes=[pltpu.VMEM((T, gw), BF16)],
        compiler_params=_cparams(("parallel", "parallel")),
        name="pool_mixer",
    )(p, p, w_pool, ls.reshape(1, G * gw))


def _out_proj_kernel(a_ref, b_ref, w_ref, x_ref, mod_ref, g_ref, o_ref, *, tm, n_ctx, row_off):
    ka = a_ref.shape[2]
    y = jnp.dot(a_ref[0], w_ref[:ka, :], preferred_element_type=F32)
    y = y + jnp.dot(b_ref[0], w_ref[ka:, :], preferred_element_type=F32)
    yn = y * lax.rsqrt(jnp.mean(y * y, axis=-1, keepdims=True) + EPS) * g_ref[...]
    gt = _row_mod(mod_ref, 2, 5, row_off + pl.program_id(1) * tm, tm, n_ctx)
    o_ref[0] = x_ref[0] + gt * yn


def _out_proj(a, b, a_blk, b_blk, w, x, modrows, g, n_ctx, tm, latents_only):
    B, T, D = x.shape
    half_k = w.shape[0] // 2
    off = n_ctx // tm if latents_only else 0
    rows_out = T - n_ctx if latents_only else T
    return pl.pallas_call(
        functools.partial(_out_proj_kernel, tm=tm, n_ctx=n_ctx, row_off=off * tm),
        grid=(B, rows_out // tm),
        in_specs=[pl.BlockSpec((1, tm, half_k), lambda bb, r: (bb, r + off, a_blk)),
                  pl.BlockSpec((1, tm, half_k), lambda bb, r: (bb, r + off, b_blk)),
                  pl.BlockSpec(w.shape, lambda bb, r: (0, 0)),
                  pl.BlockSpec((1, tm, D), lambda bb, r: (bb, r + off, 0)),
                  pl.BlockSpec((1, SUBLANES, D), lambda bb, r: (bb, 0, 0)),
                  pl.BlockSpec((1, D), lambda bb, r: (0, 0))],
        out_specs=pl.BlockSpec((1, tm, D), lambda bb, r: (bb, r, 0)),
        out_shape=jax.ShapeDtypeStruct((B, rows_out, D), F32),
        compiler_params=_cparams(("parallel", "parallel")),
        name="out_proj",
    )(a, b, w, x, modrows, g.reshape(1, D))


def _row_tile(T, cap):
    best = NORM_SLAB
    for t in range(NORM_SLAB, min(T, cap) + 1, NORM_SLAB):
        if T % t == 0:
            best = t
    return best


def _col_tile(N, cap):
    best = 128
    for t in range(128, min(N, cap) + 1, 128):
        if N % t == 0:
            best = t
    return best


def kernel(x, c, ctx, c_ctx, w_mod, b_mod, g_pre, g_post, ev_w_in, ev_w_out, ev_lambda, ev_subln_g,
           ev_hg_lb_logits, ev_hg_norm_g, od_w_in, od_w_pool, od_scale, od_w_out):
    B, S, D = x.shape
    n_ctx = ctx.shape[1]
    depth = w_mod.shape[0]
    T = n_ctx + S
    W = DA_HEADS * HEAD_W

    xc = jnp.concatenate([ctx, x], axis=1)

    n_rows = -(-(B + 1) // SUBLANES) * SUBLANES
    cvec = jnp.zeros((n_rows, D), F32).at[:B].set(c).at[B].set(c_ctx)
    mod = _modulation(cvec, w_mod, b_mod)

    lb_cum = jnp.cumsum(jax.nn.softmax(ev_hg_lb_logits.astype(F32), axis=0), axis=0)
    lb_all = lb_cum - lb_cum[0]
    cos, sin = _rope_tables(n_ctx, S)

    tm_in = _row_tile(T, 1152)
    tm_out = _row_tile(T, 384)

    for l in range(depth):
        last = l == depth - 1
        ml = mod[l]
        lat = ml[:B].reshape(B, 3, D)
        cx = jnp.broadcast_to(ml[B].reshape(1, 3, D), (B, 3, D))
        modrows = jnp.concatenate([lat, cx, jnp.zeros((B, SUBLANES - 6, D), F32)], axis=1)

        if l % 2 == 0:
            e = l // 2
            w_in = ev_w_in[e]
            w_main = jnp.concatenate([w_in[:, :5 * W], w_in[:, 7 * W:]], axis=1).astype(BF16)
            w_f = w_in[:, 5 * W:7 * W].astype(BF16)
            p = _norm_proj(xc, modrows, g_pre[l], w_main, BF16, n_ctx, tm_in, _col_tile(w_main.shape[1], 1024))
            pf = _norm_proj(xc, modrows, g_pre[l], w_f, F32, n_ctx, tm_in, _col_tile(w_f.shape[1], 1024))

            lam_init = 0.8 - 0.6 * math.exp(-0.3 * l)
            lv = ev_lambda[e].astype(F32)
            lam = jnp.exp(jnp.sum(lv[0] * lv[1])) - jnp.exp(jnp.sum(lv[2] * lv[3])) + lam_init
            a = _diff_attention(p, lam.reshape(1), cos, sin, ev_subln_g[e], 1.0 - lam_init, n_ctx)
            lb = lb_all[e].reshape(2, HG_HEADS, HG_K).transpose(1, 0, 2)
            bh = _hgrn(p, pf, lb, ev_hg_norm_g[e], n_ctx, 4)
            w_out = ev_w_out[e].astype(BF16)
            ya, yb, ia, ib = a, bh, 0, 0
        else:
            o = l // 2
            p = _norm_proj(xc, modrows, g_pre[l], od_w_in[o].astype(BF16), BF16, n_ctx, tm_in,
                           _col_tile(od_w_in.shape[2], 1024))
            y = _pool_mixer(p, od_w_pool[o].astype(BF16), od_scale[o], n_ctx)
            w_out = od_w_out[o].astype(BF16)
            ya, yb, ia, ib = y, y, 0, 1

        xc = _out_proj(ya, yb, ia, ib, w_out, xc, modrows, g_post[l], n_ctx,
                       n_ctx if last else tm_out, latents_only=last)
    return xc
```

```python
import functools
import math

import jax
import jax.numpy as jnp
from jax import lax
from jax.experimental import pallas as pl
from jax.experimental.pallas import tpu as pltpu

DA_HEADS = 8
DA_D = 64
HEAD_W = 2 * DA_D
HG_HEADS = 8
HG_K = 128
GRID_W = 64
ROPE_BASE = 10000.0
EPS = 1e-6
POOL_WINDOWS = (2, 4, 8, 16)
HG_CHUNK = 16
LOG2E = math.log2(math.e)
ATTN_Q_BLOCK = 768
ATTN_Q_SUB = 64
ATTN_AHEAD = 3
HG_GROUP = 16
HG_PREP_UNROLL = 8
SUBLANES = 8
NORM_SLAB = 128

F32 = jnp.float32
BF16 = jnp.bfloat16
VMEM_LIMIT = 56 * 1024 * 1024


def _cparams(sem):
    return pltpu.CompilerParams(dimension_semantics=sem, vmem_limit_bytes=VMEM_LIMIT)


def _sigmoid(x):
    return 1.0 / (1.0 + jnp.exp(-x))


def _mod_kernel(c_ref, w_ref, b_ref, o_ref):
    c = c_ref[...]
    s = c * _sigmoid(c)
    o_ref[0] = jnp.dot(s, w_ref[0], preferred_element_type=F32,
                       precision=lax.Precision.HIGHEST) + b_ref[0]


def _modulation(cvec, w_mod, b_mod):
    L, D, N = w_mod.shape
    R = cvec.shape[0]
    tn = 1024 if N % 1024 == 0 else N
    return pl.pallas_call(
        _mod_kernel,
        grid=(L, N // tn),
        in_specs=[pl.BlockSpec((R, D), lambda l, j: (0, 0)),
                  pl.BlockSpec((1, D, tn), lambda l, j: (l, 0, j)),
                  pl.BlockSpec((1, 1, tn), lambda l, j: (l, 0, j))],
        out_specs=pl.BlockSpec((1, R, tn), lambda l, j: (l, 0, j)),
        out_shape=jax.ShapeDtypeStruct((L, R, N), F32),
        compiler_params=_cparams(("parallel", "parallel")),
        name="modulation",
    )(cvec, w_mod, b_mod.reshape(L, 1, N))


def _row_mod(mod_ref, lat_row, ctx_row, row0, tm, n_ctx):
    rows = row0 + lax.broadcasted_iota(jnp.int32, (tm, 1), 0)
    return jnp.where(rows < n_ctx, mod_ref[0, ctx_row:ctx_row + 1, :], mod_ref[0, lat_row:lat_row + 1, :])


def _norm_proj_kernel(x_ref, mod_ref, g_ref, w_ref, o_ref, *rest, tm, n_ctx, n_lo_tiles):
    h_scr = rest[-1]
    r = pl.program_id(1)
    j = pl.program_id(2)

    @pl.when(j == 0)
    def _():
        def slab(k, carry):
            r0 = pl.multiple_of(k * NORM_SLAB, NORM_SLAB)
            x = x_ref[0, pl.ds(r0, NORM_SLAB), :]
            y = x * lax.rsqrt(jnp.mean(x * x, axis=-1, keepdims=True) + EPS) * g_ref[...]
            sh = _row_mod(mod_ref, 0, 3, r * tm + r0, NORM_SLAB, n_ctx)
            sc = _row_mod(mod_ref, 1, 4, r * tm + r0, NORM_SLAB, n_ctx)
            h_scr[pl.ds(r0, NORM_SLAB), :] = (y * (1.0 + sc) + sh).astype(BF16)
            return carry

        lax.fori_loop(0, tm // NORM_SLAB, slab, 0)

    if len(rest) == 1:
        o_ref[0] = jnp.dot(h_scr[...], w_ref[...], preferred_element_type=F32).astype(o_ref.dtype)
    else:
        @pl.when(j < n_lo_tiles)
        def _():
            o_ref[0] = jnp.dot(h_scr[...], w_ref[...], preferred_element_type=F32).astype(o_ref.dtype)

        @pl.when(j >= n_lo_tiles)
        def _():
            rest[0][0] = jnp.dot(h_scr[...], w_ref[...], preferred_element_type=F32)


def _norm_proj(x, modrows, g, w, n_f32, n_ctx, tm, tn):
    B, T, D = x.shape
    n_lo = w.shape[1] - n_f32
    assert n_lo % tn == 0 and n_f32 % tn == 0
    lo_tiles, hi_tiles = n_lo // tn, n_f32 // tn
    out_specs = [pl.BlockSpec((1, tm, tn), lambda b, r, j: (b, r, jnp.minimum(j, lo_tiles - 1)))]
    out_shape = [jax.ShapeDtypeStruct((B, T, n_lo), BF16)]
    if hi_tiles:
        out_specs.append(pl.BlockSpec((1, tm, tn), lambda b, r, j: (b, r, jnp.maximum(j - lo_tiles, 0))))
        out_shape.append(jax.ShapeDtypeStruct((B, T, n_f32), F32))
    outs = pl.pallas_call(
        functools.partial(_norm_proj_kernel, tm=tm, n_ctx=n_ctx, n_lo_tiles=lo_tiles),
        grid=(B, T // tm, lo_tiles + hi_tiles),
        in_specs=[pl.BlockSpec((1, tm, D), lambda b, r, j: (b, r, 0)),
                  pl.BlockSpec((1, SUBLANES, D), lambda b, r, j: (b, 0, 0)),
                  pl.BlockSpec((1, D), lambda b, r, j: (0, 0)),
                  pl.BlockSpec((D, tn), lambda b, r, j: (0, j))],
        out_specs=out_specs,
        out_shape=out_shape,
        scratch_shapes=[pltpu.VMEM((tm, D), BF16)],
        compiler_params=_cparams(("parallel", "parallel", "arbitrary")),
        name="norm_proj",
    )(x, modrows, g.reshape(1, D), w)
    return outs if hi_tiles else outs[0]


def _rope(x, cos, sin_signed):
    lane = lax.broadcasted_iota(jnp.int32, x.shape, 1)
    n = x.shape[1]
    partner = jnp.where(lane % 32 < 16, pltpu.roll(x, n - 16, axis=1), pltpu.roll(x, 16, axis=1))
    return x * cos + partner * sin_signed


def _attn_kernel(lam_ref, q_ref, k_ref, v_ref, gate_ref, cq_ref, sq_ref, ck_ref, sk_ref, g_ref,
                 o_ref, kt_scr, *, tq, n_ctx, out_scale):
    i = pl.program_id(2)

    @pl.when(i == 0)
    def _():
        kr = _rope(k_ref[0].astype(F32), ck_ref[...], sk_ref[...])
        kt_scr[...] = kr.T.astype(BF16)

    qr = _rope(q_ref[0].astype(F32), cq_ref[...], sq_ref[...]) * (DA_D ** -0.5 * LOG2E)
    first_map = lax.broadcasted_iota(jnp.int32, qr.shape, 1) < DA_D
    q_maps = (jnp.where(first_map, qr, 0.0).astype(BF16), jnp.where(first_map, 0.0, qr).astype(BF16))
    lam = lam_ref[0]
    qs = ATTN_Q_SUB

    def attend(n_ctx_rows):
        def scores(r0):
            nk = n_ctx if r0 < n_ctx_rows else kt_scr.shape[1]
            q2 = jnp.concatenate([q_maps[0][r0:r0 + qs], q_maps[1][r0:r0 + qs]], axis=0)
            return jnp.dot(q2, kt_scr[:, :nk], preferred_element_type=F32)

        starts = list(range(0, tq, qs))
        pending = [scores(r0) for r0 in starts[:ATTN_AHEAD]]
        for n, r0 in enumerate(starts):
            rows = slice(r0, r0 + qs)
            s = pending.pop(0)
            nk = s.shape[1]
            if n + ATTN_AHEAD < len(starts):
                pending.append(scores(starts[n + ATTN_AHEAD]))
            e = jnp.exp2(s - jnp.max(s, axis=-1, keepdims=True))
            rinv = 1.0 / jnp.sum(e, axis=-1, keepdims=True)
            ov = jnp.dot(e.astype(BF16), v_ref[0, :nk, :], preferred_element_type=F32) * rinv
            o = ov[:qs] - lam * ov[qs:]
            y = o * lax.rsqrt(jnp.mean(o * o, axis=-1, keepdims=True) + EPS) * g_ref[...] * out_scale
            gate = gate_ref[0, rows, :].astype(F32)
            o_ref[0, rows, :] = (y * (gate * _sigmoid(gate))).astype(o_ref.dtype)

    @pl.when(i == 0)
    def _():
        attend(n_ctx)

    @pl.when(i != 0)
    def _():
        attend(0)


def _diff_attention(p, lam, cos, sin, subln_g, out_scale, n_ctx):
    B, T, _ = p.shape
    tq = max(t for t in range(n_ctx, ATTN_Q_BLOCK + 1, n_ctx) if T % t == 0)
    assert n_ctx % ATTN_Q_SUB == 0
    H = DA_HEADS
    head = lambda part: (lambda b, h, i: (b, i, part * H + h))
    head_all = lambda part: (lambda b, h, i: (b, 0, part * H + h))
    return pl.pallas_call(
        functools.partial(_attn_kernel, tq=tq, n_ctx=n_ctx, out_scale=out_scale),
        grid=(B, H, T // tq),
        in_specs=[pl.BlockSpec(memory_space=pltpu.SMEM),
                  pl.BlockSpec((1, tq, HEAD_W), head(0)),
                  pl.BlockSpec((1, T, HEAD_W), head_all(1)),
                  pl.BlockSpec((1, T, HEAD_W), head_all(2)),
                  pl.BlockSpec((1, tq, HEAD_W), head(3)),
                  pl.BlockSpec((tq, HEAD_W), lambda b, h, i: (i, 0)),
                  pl.BlockSpec((tq, HEAD_W), lambda b, h, i: (i, 0)),
                  pl.BlockSpec((T, HEAD_W), lambda b, h, i: (0, 0)),
                  pl.BlockSpec((T, HEAD_W), lambda b, h, i: (0, 0)),
                  pl.BlockSpec((1, HEAD_W), lambda b, h, i: (0, 0))],
        out_specs=pl.BlockSpec((1, tq, HEAD_W), lambda b, h, i: (b, i, h)),
        out_shape=jax.ShapeDtypeStruct((B, T, H * HEAD_W), BF16),
        scratch_shapes=[pltpu.VMEM((HEAD_W, T), BF16)],
        compiler_params=_cparams(("parallel", "parallel", "arbitrary")),
        name="diff_attention",
    )(lam, p, p, p, p, cos, sin, cos, sin, subln_g.reshape(1, HEAD_W))


def _rope_tables(n_ctx, seq):
    n_freq = DA_D // 4
    inv = ROPE_BASE ** (-jnp.arange(n_freq, dtype=F32) / n_freq)
    t = jnp.arange(seq)
    lane = jnp.arange(HEAD_W)
    pos = jnp.where((lane % DA_D < DA_D // 2)[None, :], (t // GRID_W)[:, None], (t % GRID_W)[:, None]).astype(F32)
    ang = pos * inv[lane % n_freq][None, :]
    sign = jnp.where(lane % 32 < 16, -1.0, 1.0).astype(F32)[None, :]
    cos = jnp.concatenate([jnp.ones((n_ctx, HEAD_W), F32), jnp.cos(ang)], axis=0)
    sin = jnp.concatenate([jnp.zeros((n_ctx, HEAD_W), F32), jnp.sin(ang) * sign], axis=0)
    return cos, sin


def _chunk_cumsum(g, reverse):
    n = g.shape[0]
    row = lax.broadcasted_iota(jnp.int32, g.shape, 0)
    b = g
    sh = 1
    while sh < n:
        if reverse:
            b = b + jnp.where(row < n - sh, pltpu.roll(b, n - sh, axis=0), 0.0)
        else:
            b = b + jnp.where(row >= sh, pltpu.roll(b, sh, axis=0), 0.0)
        sh *= 2
    return b


def _score_tiles(qq, b2, c2, reverse):
    C = qq.shape[0]
    half = SUBLANES
    sub = lax.broadcasted_iota(jnp.int32, (half, qq.shape[1]), 0)
    tiles = []
    for s in range(C):
        sb = s // half
        blocks = range(sb + 1) if reverse else range(sb, C // half)
        for tb in blocks:
            rows = slice(tb * half, (tb + 1) * half)
            w = qq[rows] * jnp.exp2(b2[rows] - c2[s:s + 1])
            if tb == sb:
                keep = (sub <= s - sb * half) if reverse else (sub >= s - sb * half)
                w = jnp.where(keep, w, 0.0)
            tiles.append((s, tb, w))
    return tiles


def _lane_sums(tiles, ones_ref):
    n = len(tiles)
    assert n % 2 == 0
    lhs = jnp.concatenate([jnp.concatenate([tiles[2 * i], tiles[2 * i + 1]], axis=1) for i in range(n // 2)], axis=0)
    r = jnp.dot(lhs.astype(BF16), ones_ref[...], preferred_element_type=F32)
    w = tiles[0].shape[1]
    out = []
    for i in range(n // 2):
        rows = slice(i * SUBLANES, (i + 1) * SUBLANES)
        out += [r[rows, :w], r[rows, w:]]
    return out


def _hgrn_kernel(q_ref, v_ref, gate_ref, zf_ref, zb_ref, lb_ref, g_ref, o_ref,
                 qt_scr, oi_scr, ox_scr, dl_scr, u_scr, st_scr, ones_scr, *, n_ctx):
    T = q_ref.shape[1]
    C = HG_CHUNK
    K = HG_K
    nc = T // C
    nc_ctx = n_ctx // C
    same_half = (lax.broadcasted_iota(jnp.int32, (2 * K, 2 * K), 0) < K) == (lax.broadcasted_iota(jnp.int32, (2 * K, 2 * K), 1) < K)
    ones_scr[...] = jnp.where(same_half, 1.0, 0.0).astype(BF16)

    def prep(c, carry):
        r0 = pl.multiple_of(c * C, C)
        hq = q_ref[0, pl.ds(r0, C), :].astype(F32)
        qq = hq * _sigmoid(hq)
        vb = v_ref[0, pl.ds(r0, C), :]
        v = vb.astype(F32)
        tiles, keys = [], []
        for d, z_ref in enumerate((zf_ref, zb_ref)):
            reverse = d == 1
            z = z_ref[0, pl.ds(r0, C), :]
            lb = lb_ref[0, d:d + 1, :]
            e = jnp.exp(-jnp.abs(z))
            inv = 1.0 / (1.0 + e)
            sig = jnp.where(z >= 0, inv, e * inv)
            nsig = jnp.where(z >= 0, e * inv, inv)
            lk2 = jnp.log2(1.0 - lb) + jnp.log2(nsig)
            b2 = _chunk_cumsum(jnp.log2(lb + (1.0 - lb) * sig), reverse)
            b2_last = b2[0:1] if reverse else b2[C - 1:C]
            qt_scr[d, pl.ds(r0, C), :] = (qq * jnp.exp2(b2)).astype(BF16)
            keys.append(jnp.exp2(lk2 + (b2_last - b2)).astype(BF16))
            dl_scr[d, pl.ds(c, 1), :] = jnp.exp2(b2_last)
            tiles.append(_score_tiles(qq, b2, b2 - lk2, reverse))
        sums = _lane_sums([w for per_dir in tiles for _, _, w in per_dir], ones_scr)
        k = 0
        for d in range(2):
            parts = [jnp.zeros((SUBLANES, v.shape[1]), F32) for _ in range(C // SUBLANES)]
            for s, tb, _ in tiles[d]:
                parts[tb] = parts[tb] + sums[k] * v[s:s + 1]
                k += 1
            oi_scr[d, pl.ds(r0, C), :] = jnp.concatenate(parts, axis=0)
        upd = lax.dot_general(vb, jnp.concatenate(keys, axis=1), (((0,), (0,)), ((), ())),
                              preferred_element_type=F32)
        u_scr[0, c] = upd[:, :K]
        u_scr[1, c] = upd[:, K:]
        return carry

    lax.fori_loop(0, nc, prep, 0, unroll=HG_PREP_UNROLL)

    st_scr[...] = jnp.zeros_like(st_scr)

    def scan(m, carry):
        i0 = m * HG_GROUP
        cb0 = jnp.where(i0 < nc_ctx, nc_ctx - 1 - i0, nc + nc_ctx - 1 - i0)
        for j in range(HG_GROUP):
            for d, c in enumerate((i0 + j, cb0 - j)):
                r0 = pl.multiple_of(c * C, C)
                st = st_scr[d]
                ox_scr[d, pl.ds(r0, C), :] = lax.dot_general(qt_scr[d, pl.ds(r0, C), :], st.astype(BF16),
                                                             (((1,), (1,)), ((), ())), preferred_element_type=F32)
                st_scr[d] = dl_scr[d, pl.ds(c, 1), :] * st + u_scr[d, c]
        return carry

    assert nc % HG_GROUP == 0 and nc_ctx % HG_GROUP == 0
    lax.fori_loop(0, nc // HG_GROUP, scan, 0)

    o = (oi_scr[0] + ox_scr[0]) + (oi_scr[1] + ox_scr[1])
    y = o * lax.rsqrt(jnp.mean(o * o, axis=-1, keepdims=True) + EPS) * g_ref[...]
    gate = gate_ref[0].astype(F32)
    o_ref[0] = (y * (gate * _sigmoid(gate))).astype(o_ref.dtype)


def _hgrn(p, pf, lb, norm_g, n_ctx, part0):
    B, T, _ = p.shape
    H = HG_HEADS
    nc = T // HG_CHUNK
    part = lambda k: (lambda b, h: (b, 0, k * H + h))
    blk = (1, T, HEAD_W)
    return pl.pallas_call(
        functools.partial(_hgrn_kernel, n_ctx=n_ctx),
        grid=(B, H),
        in_specs=[pl.BlockSpec(blk, part(part0)), pl.BlockSpec(blk, part(part0 + 1)),
                  pl.BlockSpec(blk, part(part0 + 2)),
                  pl.BlockSpec(blk, part(0)), pl.BlockSpec(blk, part(1)),
                  pl.BlockSpec((1, 2, HG_K), lambda b, h: (h, 0, 0)),
                  pl.BlockSpec((1, HEAD_W), lambda b, h: (0, 0))],
        out_specs=pl.BlockSpec(blk, lambda b, h: (b, 0, h)),
        out_shape=jax.ShapeDtypeStruct((B, T, H * HEAD_W), BF16),
        scratch_shapes=[pltpu.VMEM((2, T, HG_K), BF16),
                        pltpu.VMEM((2, T, HEAD_W), F32), pltpu.VMEM((2, T, HEAD_W), F32),
                        pltpu.VMEM((2, nc, HG_K), F32),
                        pltpu.VMEM((2, nc, HEAD_W, HG_K), F32),
                        pltpu.VMEM((2, HEAD_W, HG_K), F32),
                        pltpu.VMEM((2 * HG_K, 2 * HG_K), BF16)],
        compiler_params=_cparams(("parallel", "parallel")),
        name="hgrn2",
    )(p, p, p, pf, pf, lb, norm_g.reshape(1, HEAD_W))


def _shift_rows(x, d):
    n = x.shape[0]
    row = lax.broadcasted_iota(jnp.int32, (n, 1), 0)
    rolled = pltpu.roll(x, (-d) % n, axis=0)
    keep = (row + d >= 0) & (row + d < n)
    return jnp.where(keep, rolled, 0.0)


def _pool_minus_identity(x, w):
    n = x.shape[0]
    ahead = w - w // 2
    behind = w // 2
    fwd = x
    span = 1
    while span < ahead:
        fwd = fwd + _shift_rows(fwd, span)
        span *= 2
    bwd = x
    span = 1
    while span < behind:
        bwd = bwd + _shift_rows(bwd, -span)
        span *= 2
    total = fwd + _shift_rows(bwd, -1)
    row = lax.broadcasted_iota(jnp.int32, (n, 1), 0)
    cnt = jnp.minimum(row + ahead, n) - jnp.maximum(row - behind, 0)
    return total / cnt.astype(F32) - x


def _pool_kernel(u_ref, z_ref, w_ref, ls_ref, o_ref, r_scr, *, n_ctx):
    j = pl.program_id(1)
    T = u_ref.shape[1]
    for jj, win in enumerate(POOL_WINDOWS):
        @pl.when(j == jj)
        def _(win=win):
            for lo, hi in ((0, n_ctx), (n_ctx, T)):
                r_scr[lo:hi, :] = _pool_minus_identity(u_ref[0, lo:hi, :].astype(F32), win).astype(BF16)

    y = jnp.dot(r_scr[...], w_ref[0], preferred_element_type=F32)
    z = z_ref[0].astype(F32)
    o_ref[0] = (y * ls_ref[...] * (z * _sigmoid(z))).astype(o_ref.dtype)


def _pool_mixer(p, w_pool, ls, n_ctx):
    B, T, two_e = p.shape
    G, gw, _ = w_pool.shape
    assert G == len(POOL_WINDOWS) and two_e == 2 * G * gw
    return pl.pallas_call(
        functools.partial(_pool_kernel, n_ctx=n_ctx),
        grid=(B, G),
        in_specs=[pl.BlockSpec((1, T, gw), lambda b, j: (b, 0, j)),
                  pl.BlockSpec((1, T, gw), lambda b, j: (b, 0, G + j)),
                  pl.BlockSpec((1, gw, gw), lambda b, j: (j, 0, 0)),
                  pl.BlockSpec((1, gw), lambda b, j: (0, j))],
        out_specs=pl.BlockSpec((1, T, gw), lambda b, j: (b, 0, j)),
        out_shape=jax.ShapeDtypeStruct((B, T, G * gw), BF16),
        scratch_shapes=[pltpu.VMEM((T, gw), BF16)],
        compiler_params=_cparams(("parallel", "parallel")),
        name="pool_mixer",
    )(p, p, w_pool, ls.reshape(1, G * gw))


def _out_proj_kernel(a_ref, b_ref, w_ref, x_ref, mod_ref, g_ref, o_ref, *, tm, n_ctx, row_off):
    ka = a_ref.shape[2]
    y = jnp.dot(a_ref[0], w_ref[:ka, :], preferred_element_type=F32)
    y = y + jnp.dot(b_ref[0], w_ref[ka:, :], preferred_element_type=F32)
    yn = y * lax.rsqrt(jnp.mean(y * y, axis=-1, keepdims=True) + EPS) * g_ref[...]
    gt = _row_mod(mod_ref, 2, 5, row_off + pl.program_id(1) * tm, tm, n_ctx)
    o_ref[0] = x_ref[0] + gt * yn


def _out_proj(a, b, a_blk, b_blk, w, x, modrows, g, n_ctx, tm, latents_only):
    B, T, D = x.shape
    half_k = w.shape[0] // 2
    off = n_ctx // tm if latents_only else 0
    rows_out = T - n_ctx if latents_only else T
    return pl.pallas_call(
        functools.partial(_out_proj_kernel, tm=tm, n_ctx=n_ctx, row_off=off * tm),
        grid=(B, rows_out // tm),
        in_specs=[pl.BlockSpec((1, tm, half_k), lambda bb, r: (bb, r + off, a_blk)),
                  pl.BlockSpec((1, tm, half_k), lambda bb, r: (bb, r + off, b_blk)),
                  pl.BlockSpec(w.shape, lambda bb, r: (0, 0)),
                  pl.BlockSpec((1, tm, D), lambda bb, r: (bb, r + off, 0)),
                  pl.BlockSpec((1, SUBLANES, D), lambda bb, r: (bb, 0, 0)),
                  pl.BlockSpec((1, D), lambda bb, r: (0, 0))],
        out_specs=pl.BlockSpec((1, tm, D), lambda bb, r: (bb, r, 0)),
        out_shape=jax.ShapeDtypeStruct((B, rows_out, D), F32),
        compiler_params=_cparams(("parallel", "parallel")),
        name="out_proj",
    )(a, b, w, x, modrows, g.reshape(1, D))


def _row_tile(T, cap):
    best = NORM_SLAB
    for t in range(NORM_SLAB, min(T, cap) + 1, NORM_SLAB):
        if T % t == 0:
            best = t
    return best


def _col_tile(N, cap):
    best = 128
    for t in range(128, min(N, cap) + 1, 128):
        if N % t == 0:
            best = t
    return best


def kernel(x, c, ctx, c_ctx, w_mod, b_mod, g_pre, g_post, ev_w_in, ev_w_out, ev_lambda, ev_subln_g,
           ev_hg_lb_logits, ev_hg_norm_g, od_w_in, od_w_pool, od_scale, od_w_out):
    B, S, D = x.shape
    n_ctx = ctx.shape[1]
    depth = w_mod.shape[0]
    T = n_ctx + S
    W = DA_HEADS * HEAD_W

    xc = jnp.concatenate([ctx, x], axis=1)

    n_rows = -(-(B + 1) // SUBLANES) * SUBLANES
    cvec = jnp.zeros((n_rows, D), F32).at[:B].set(c).at[B].set(c_ctx)
    mod = _modulation(cvec, w_mod, b_mod)

    lb_cum = jnp.cumsum(jax.nn.softmax(ev_hg_lb_logits.astype(F32), axis=0), axis=0)
    lb_all = lb_cum - lb_cum[0]
    cos, sin = _rope_tables(n_ctx, S)

    tm_in = _row_tile(T, 1152)
    tm_out = _row_tile(T, 384)

    for l in range(depth):
        last = l == depth - 1
        ml = mod[l]
        lat = ml[:B].reshape(B, 3, D)
        cx = jnp.broadcast_to(ml[B].reshape(1, 3, D), (B, 3, D))
        modrows = jnp.concatenate([lat, cx, jnp.zeros((B, SUBLANES - 6, D), F32)], axis=1)

        if l % 2 == 0:
            e = l // 2
            w_in = ev_w_in[e]
            w_perm = jnp.concatenate([w_in[:, :5 * W], w_in[:, 7 * W:], w_in[:, 5 * W:7 * W]], axis=1).astype(BF16)
            p, pf = _norm_proj(xc, modrows, g_pre[l], w_perm, 2 * W, n_ctx, tm_in, _col_tile(W, 1024))

            lam_init = 0.8 - 0.6 * math.exp(-0.3 * l)
            lv = ev_lambda[e].astype(F32)
            lam = jnp.exp(jnp.sum(lv[0] * lv[1])) - jnp.exp(jnp.sum(lv[2] * lv[3])) + lam_init
            a = _diff_attention(p, lam.reshape(1), cos, sin, ev_subln_g[e], 1.0 - lam_init, n_ctx)
            lb = lb_all[e].reshape(2, HG_HEADS, HG_K).transpose(1, 0, 2)
            bh = _hgrn(p, pf, lb, ev_hg_norm_g[e], n_ctx, 4)
            w_out = ev_w_out[e].astype(BF16)
            ya, yb, ia, ib = a, bh, 0, 0
        else:
            o = l // 2
            p = _norm_proj(xc, modrows, g_pre[l], od_w_in[o].astype(BF16), 0, n_ctx, tm_in,
                           _col_tile(od_w_in.shape[2], 1024))
            y = _pool_mixer(p, od_w_pool[o].astype(BF16), od_scale[o], n_ctx)
            w_out = od_w_out[o].astype(BF16)
            ya, yb, ia, ib = y, y, 0, 1

        xc = _out_proj(ya, yb, ia, ib, w_out, xc, modrows, g_post[l], n_ctx,
                       n_ctx if last else tm_out, latents_only=last)
    return xc
```

```python
import functools
import math

import jax
import jax.numpy as jnp
from jax import lax
from jax.experimental import pallas as pl
from jax.experimental.pallas import tpu as pltpu

DA_HEADS = 8
DA_D = 64
HEAD_W = 2 * DA_D
HG_HEADS = 8
HG_K = 128
GRID_W = 64
ROPE_BASE = 10000.0
EPS = 1e-6
POOL_WINDOWS = (2, 4, 8, 16)
HG_CHUNK = 16
LOG2E = math.log2(math.e)
ATTN_Q_BLOCK = 768
ATTN_Q_SUB = 128
ATTN_AHEAD = 3
HG_GROUP = 16
HG_PREP_UNROLL = 8
SUBLANES = 8
OUT_SUB = 256
NORM_SLAB = 16
NORM_UNROLL = 4

F32 = jnp.float32
BF16 = jnp.bfloat16
VMEM_LIMIT = 56 * 1024 * 1024


def _cparams(sem):
    return pltpu.CompilerParams(dimension_semantics=sem, vmem_limit_bytes=VMEM_LIMIT)


def _sigmoid(x):
    return 1.0 / (1.0 + jnp.exp(-x))


def _mod_kernel(c_ref, w_ref, b_ref, o_ref):
    c = c_ref[...]
    s = c * _sigmoid(c)
    o_ref[0] = jnp.dot(s, w_ref[0], preferred_element_type=F32,
                       precision=lax.Precision.HIGHEST) + b_ref[0]


def _modulation(cvec, w_mod, b_mod):
    L, D, N = w_mod.shape
    R = cvec.shape[0]
    tn = 1024 if N % 1024 == 0 else N
    return pl.pallas_call(
        _mod_kernel,
        grid=(L, N // tn),
        in_specs=[pl.BlockSpec((R, D), lambda l, j: (0, 0)),
                  pl.BlockSpec((1, D, tn), lambda l, j: (l, 0, j)),
                  pl.BlockSpec((1, 1, tn), lambda l, j: (l, 0, j))],
        out_specs=pl.BlockSpec((1, R, tn), lambda l, j: (l, 0, j)),
        out_shape=jax.ShapeDtypeStruct((L, R, N), F32),
        compiler_params=_cparams(("parallel", "parallel")),
        name="modulation",
    )(cvec, w_mod, b_mod.reshape(L, 1, N))


def _row_mod(mod_ref, lat_row, ctx_row, row0, tm, n_ctx):
    rows = row0 + lax.broadcasted_iota(jnp.int32, (tm, 1), 0)
    return jnp.where(rows < n_ctx, mod_ref[0, ctx_row:ctx_row + 1, :], mod_ref[0, lat_row:lat_row + 1, :])


def _norm_proj_kernel(x_ref, mod_ref, g_ref, w_ref, o_ref, *rest, tm, n_ctx, n_lo_tiles):
    h_scr, gs_scr = rest[-2:]
    r = pl.program_id(1)
    j = pl.program_id(2)

    @pl.when(j == 0)
    def _():
        gs_scr[0:1, :] = g_ref[...] * (1.0 + mod_ref[0, 1:2, :])
        gs_scr[1:2, :] = mod_ref[0, 0:1, :]
        gs_scr[2:3, :] = g_ref[...] * (1.0 + mod_ref[0, 4:5, :])
        gs_scr[3:4, :] = mod_ref[0, 3:4, :]

        def slab(k, carry):
            r0 = pl.multiple_of(k * NORM_SLAB, NORM_SLAB)
            base = jnp.where(r * tm + r0 < n_ctx, 2, 0)
            x = x_ref[0, pl.ds(r0, NORM_SLAB), :]
            y = x * lax.rsqrt(jnp.mean(x * x, axis=-1, keepdims=True) + EPS)
            h_scr[pl.ds(r0, NORM_SLAB), :] = (y * gs_scr[pl.ds(base, 1), :] + gs_scr[pl.ds(base + 1, 1), :]).astype(BF16)
            return carry

        lax.fori_loop(0, tm // NORM_SLAB, slab, 0, unroll=NORM_UNROLL)

    if len(rest) == 2:
        o_ref[0] = jnp.dot(h_scr[...], w_ref[...], preferred_element_type=F32).astype(o_ref.dtype)
    else:
        @pl.when(j < n_lo_tiles)
        def _():
            o_ref[0] = jnp.dot(h_scr[...], w_ref[...], preferred_element_type=F32).astype(o_ref.dtype)

        @pl.when(j >= n_lo_tiles)
        def _():
            rest[0][0] = jnp.dot(h_scr[...], w_ref[...], preferred_element_type=F32)


def _norm_proj(x, modrows, g, w, n_f32, n_ctx, tm, tn):
    B, T, D = x.shape
    n_lo = w.shape[1] - n_f32
    assert n_lo % tn == 0 and n_f32 % tn == 0
    lo_tiles, hi_tiles = n_lo // tn, n_f32 // tn
    out_specs = [pl.BlockSpec((1, tm, tn), lambda b, r, j: (b, r, jnp.minimum(j, lo_tiles - 1)))]
    out_shape = [jax.ShapeDtypeStruct((B, T, n_lo), BF16)]
    if hi_tiles:
        out_specs.append(pl.BlockSpec((1, tm, tn), lambda b, r, j: (b, r, jnp.maximum(j - lo_tiles, 0))))
        out_shape.append(jax.ShapeDtypeStruct((B, T, n_f32), F32))
    outs = pl.pallas_call(
        functools.partial(_norm_proj_kernel, tm=tm, n_ctx=n_ctx, n_lo_tiles=lo_tiles),
        grid=(B, T // tm, lo_tiles + hi_tiles),
        in_specs=[pl.BlockSpec((1, tm, D), lambda b, r, j: (b, r, 0)),
                  pl.BlockSpec((1, SUBLANES, D), lambda b, r, j: (b, 0, 0)),
                  pl.BlockSpec((1, D), lambda b, r, j: (0, 0)),
                  pl.BlockSpec((D, tn), lambda b, r, j: (0, j))],
        out_specs=out_specs,
        out_shape=out_shape,
        scratch_shapes=[pltpu.VMEM((tm, D), BF16), pltpu.VMEM((4, D), F32)],
        compiler_params=_cparams(("parallel", "parallel", "arbitrary")),
        name="norm_proj",
    )(x, modrows, g.reshape(1, D), w)
    return outs if hi_tiles else outs[0]


def _rope(x, cos, sin_signed):
    lane = lax.broadcasted_iota(jnp.int32, x.shape, 1)
    n = x.shape[1]
    partner = jnp.where(lane % 32 < 16, pltpu.roll(x, n - 16, axis=1), pltpu.roll(x, 16, axis=1))
    return x * cos + partner * sin_signed


def _attn_kernel(lam_ref, q_ref, k_ref, v_ref, gate_ref, cq_ref, sq_ref, ck_ref, sk_ref, g_ref,
                 o_ref, k_scr, vt_scr, *, tq, n_ctx, out_scale):
    i = pl.program_id(2)
    n_keys = k_scr.shape[0]

    @pl.when(i == 0)
    def _():
        k_scr[...] = _rope(k_ref[0].astype(F32), ck_ref[...], sk_ref[...]).astype(BF16)
        vt_scr[...] = v_ref[0].astype(F32).T.astype(BF16)

    qr = _rope(q_ref[0].astype(F32), cq_ref[...], sq_ref[...]) * (DA_D ** -0.5 * LOG2E)
    first_map = lax.broadcasted_iota(jnp.int32, qr.shape, 1) < DA_D
    q_maps = (jnp.where(first_map, qr, 0.0).astype(BF16), jnp.where(first_map, 0.0, qr).astype(BF16))
    lam = lam_ref[0]
    qs = ATTN_Q_SUB

    def attend(n_ctx_rows):
        def scores(r0):
            nk = n_ctx if r0 < n_ctx_rows else n_keys
            q2 = jnp.concatenate([q_maps[0][r0:r0 + qs], q_maps[1][r0:r0 + qs]], axis=0)
            return lax.dot_general(k_scr[:nk, :], q2, (((1,), (1,)), ((), ())),
                                   preferred_element_type=F32)

        starts = list(range(0, tq, qs))
        pending = [scores(r0) for r0 in starts[:ATTN_AHEAD]]
        for n, r0 in enumerate(starts):
            rows = slice(r0, r0 + qs)
            s = pending.pop(0)
            nk = s.shape[0]
            if n + ATTN_AHEAD < len(starts):
                pending.append(scores(starts[n + ATTN_AHEAD]))
            e = jnp.exp2(s - jnp.max(s, axis=0, keepdims=True))
            rinv = 1.0 / jnp.sum(e, axis=0, keepdims=True)
            ov = jnp.dot(vt_scr[:, :nk], e.astype(BF16), preferred_element_type=F32) * rinv
            o = (ov[:, :qs] - lam * ov[:, qs:]).T
            y = o * lax.rsqrt(jnp.mean(o * o, axis=-1, keepdims=True) + EPS) * g_ref[...] * out_scale
            gate = gate_ref[0, rows, :].astype(F32)
            o_ref[0, rows, :] = (y * (gate * _sigmoid(gate))).astype(o_ref.dtype)

    @pl.when(i == 0)
    def _():
        attend(n_ctx)

    @pl.when(i != 0)
    def _():
        attend(0)


def _diff_attention(p, lam, cos, sin, subln_g, out_scale, n_ctx):
    B, T, _ = p.shape
    tq = max(t for t in range(n_ctx, ATTN_Q_BLOCK + 1, n_ctx) if T % t == 0)
    assert n_ctx % ATTN_Q_SUB == 0
    H = DA_HEADS
    head = lambda part: (lambda b, h, i: (b, i, part * H + h))
    head_all = lambda part: (lambda b, h, i: (b, 0, part * H + h))
    return pl.pallas_call(
        functools.partial(_attn_kernel, tq=tq, n_ctx=n_ctx, out_scale=out_scale),
        grid=(B, H, T // tq),
        in_specs=[pl.BlockSpec(memory_space=pltpu.SMEM),
                  pl.BlockSpec((1, tq, HEAD_W), head(0)),
                  pl.BlockSpec((1, T, HEAD_W), head_all(1)),
                  pl.BlockSpec((1, T, HEAD_W), head_all(2)),
                  pl.BlockSpec((1, tq, HEAD_W), head(3)),
                  pl.BlockSpec((tq, HEAD_W), lambda b, h, i: (i, 0)),
                  pl.BlockSpec((tq, HEAD_W), lambda b, h, i: (i, 0)),
                  pl.BlockSpec((T, HEAD_W), lambda b, h, i: (0, 0)),
                  pl.BlockSpec((T, HEAD_W), lambda b, h, i: (0, 0)),
                  pl.BlockSpec((1, HEAD_W), lambda b, h, i: (0, 0))],
        out_specs=pl.BlockSpec((1, tq, HEAD_W), lambda b, h, i: (b, i, h)),
        out_shape=jax.ShapeDtypeStruct((B, T, H * HEAD_W), BF16),
        scratch_shapes=[pltpu.VMEM((T, HEAD_W), BF16), pltpu.VMEM((HEAD_W, T), BF16)],
        compiler_params=_cparams(("parallel", "parallel", "arbitrary")),
        name="diff_attention",
    )(lam, p, p, p, p, cos, sin, cos, sin, subln_g.reshape(1, HEAD_W))


def _rope_tables(n_ctx, seq):
    n_freq = DA_D // 4
    inv = ROPE_BASE ** (-jnp.arange(n_freq, dtype=F32) / n_freq)
    t = jnp.arange(seq)
    lane = jnp.arange(HEAD_W)
    pos = jnp.where((lane % DA_D < DA_D // 2)[None, :], (t // GRID_W)[:, None], (t % GRID_W)[:, None]).astype(F32)
    ang = pos * inv[lane % n_freq][None, :]
    sign = jnp.where(lane % 32 < 16, -1.0, 1.0).astype(F32)[None, :]
    cos = jnp.concatenate([jnp.ones((n_ctx, HEAD_W), F32), jnp.cos(ang)], axis=0)
    sin = jnp.concatenate([jnp.zeros((n_ctx, HEAD_W), F32), jnp.sin(ang) * sign], axis=0)
    return cos, sin


def _chunk_cumsum(g, reverse):
    n = g.shape[0]
    row = lax.broadcasted_iota(jnp.int32, g.shape, 0)
    b = g
    sh = 1
    while sh < n:
        if reverse:
            b = b + jnp.where(row < n - sh, pltpu.roll(b, n - sh, axis=0), 0.0)
        else:
            b = b + jnp.where(row >= sh, pltpu.roll(b, sh, axis=0), 0.0)
        sh *= 2
    return b


def _score_tiles(qq, b2, c2, reverse):
    C = qq.shape[0]
    half = SUBLANES
    sub = lax.broadcasted_iota(jnp.int32, (half, qq.shape[1]), 0)
    tiles = []
    for s in range(C):
        sb = s // half
        blocks = range(sb + 1) if reverse else range(sb, C // half)
        for tb in blocks:
            rows = slice(tb * half, (tb + 1) * half)
            w = qq[rows] * jnp.exp2(b2[rows] - c2[s:s + 1])
            if tb == sb:
                keep = (sub <= s - sb * half) if reverse else (sub >= s - sb * half)
                w = jnp.where(keep, w, 0.0)
            tiles.append((s, tb, w))
    return tiles


def _lane_sums(tiles, ones_ref):
    n = len(tiles)
    assert n % 2 == 0
    lhs = jnp.concatenate([jnp.concatenate([tiles[2 * i], tiles[2 * i + 1]], axis=1) for i in range(n // 2)], axis=0)
    r = jnp.dot(lhs.astype(BF16), ones_ref[...], preferred_element_type=F32)
    w = tiles[0].shape[1]
    out = []
    for i in range(n // 2):
        rows = slice(i * SUBLANES, (i + 1) * SUBLANES)
        out += [r[rows, :w], r[rows, w:]]
    return out


def _hgrn_kernel(q_ref, v_ref, gate_ref, zf_ref, zb_ref, lb_ref, g_ref, o_ref,
                 qt_scr, oi_scr, ox_scr, dl_scr, u_scr, st_scr, ones_scr, *, n_ctx):
    T = q_ref.shape[1]
    C = HG_CHUNK
    K = HG_K
    nc = T // C
    nc_ctx = n_ctx // C
    same_half = (lax.broadcasted_iota(jnp.int32, (2 * K, 2 * K), 0) < K) == (lax.broadcasted_iota(jnp.int32, (2 * K, 2 * K), 1) < K)
    ones_scr[...] = jnp.where(same_half, 1.0, 0.0).astype(BF16)

    def prep(c, carry):
        r0 = pl.multiple_of(c * C, C)
        hq = q_ref[0, pl.ds(r0, C), :].astype(F32)
        qq = hq * _sigmoid(hq)
        vb = v_ref[0, pl.ds(r0, C), :]
        v = vb.astype(F32)
        tiles, keys = [], []
        for d, z_ref in enumerate((zf_ref, zb_ref)):
            reverse = d == 1
            z = z_ref[0, pl.ds(r0, C), :]
            lb = lb_ref[0, d:d + 1, :]
            e = jnp.exp(-jnp.abs(z))
            inv = 1.0 / (1.0 + e)
            sig = jnp.where(z >= 0, inv, e * inv)
            nsig = jnp.where(z >= 0, e * inv, inv)
            lk2 = jnp.log2(1.0 - lb) + jnp.log2(nsig)
            b2 = _chunk_cumsum(jnp.log2(lb + (1.0 - lb) * sig), reverse)
            b2_last = b2[0:1] if reverse else b2[C - 1:C]
            qt_scr[d, pl.ds(r0, C), :] = (qq * jnp.exp2(b2)).astype(BF16)
            keys.append(jnp.exp2(lk2 + (b2_last - b2)).astype(BF16))
            dl_scr[d, pl.ds(c, 1), :] = jnp.exp2(b2_last)
            tiles.append(_score_tiles(qq, b2, b2 - lk2, reverse))
        sums = _lane_sums([w for per_dir in tiles for _, _, w in per_dir], ones_scr)
        k = 0
        for d in range(2):
            parts = [jnp.zeros((SUBLANES, v.shape[1]), F32) for _ in range(C // SUBLANES)]
            for s, tb, _ in tiles[d]:
                parts[tb] = parts[tb] + sums[k] * v[s:s + 1]
                k += 1
            oi_scr[d, pl.ds(r0, C), :] = jnp.concatenate(parts, axis=0)
        upd = lax.dot_general(vb, jnp.concatenate(keys, axis=1), (((0,), (0,)), ((), ())),
                              preferred_element_type=F32)
        u_scr[0, c] = upd[:, :K]
        u_scr[1, c] = upd[:, K:]
        return carry

    lax.fori_loop(0, nc, prep, 0, unroll=HG_PREP_UNROLL)

    st_scr[...] = jnp.zeros_like(st_scr)

    def scan(m, carry):
        i0 = m * HG_GROUP
        cb0 = jnp.where(i0 < nc_ctx, nc_ctx - 1 - i0, nc + nc_ctx - 1 - i0)
        for j in range(HG_GROUP):
            for d, c in enumerate((i0 + j, cb0 - j)):
                r0 = pl.multiple_of(c * C, C)
                st = st_scr[d]
                ox_scr[d, pl.ds(r0, C), :] = lax.dot_general(qt_scr[d, pl.ds(r0, C), :], st.astype(BF16),
                                                             (((1,), (1,)), ((), ())), preferred_element_type=F32)
                st_scr[d] = dl_scr[d, pl.ds(c, 1), :] * st + u_scr[d, c]
        return carry

    assert nc % HG_GROUP == 0 and nc_ctx % HG_GROUP == 0
    lax.fori_loop(0, nc // HG_GROUP, scan, 0)

    o = (oi_scr[0] + ox_scr[0]) + (oi_scr[1] + ox_scr[1])
    y = o * lax.rsqrt(jnp.mean(o * o, axis=-1, keepdims=True) + EPS) * g_ref[...]
    gate = gate_ref[0].astype(F32)
    o_ref[0] = (y * (gate * _sigmoid(gate))).astype(o_ref.dtype)


def _hgrn(p, pf, lb, norm_g, n_ctx, part0):
    B, T, _ = p.shape
    H = HG_HEADS
    nc = T // HG_CHUNK
    part = lambda k: (lambda b, h: (b, 0, k * H + h))
    blk = (1, T, HEAD_W)
    return pl.pallas_call(
        functools.partial(_hgrn_kernel, n_ctx=n_ctx),
        grid=(B, H),
        in_specs=[pl.BlockSpec(blk, part(part0)), pl.BlockSpec(blk, part(part0 + 1)),
                  pl.BlockSpec(blk, part(part0 + 2)),
                  pl.BlockSpec(blk, part(0)), pl.BlockSpec(blk, part(1)),
                  pl.BlockSpec((1, 2, HG_K), lambda b, h: (h, 0, 0)),
                  pl.BlockSpec((1, HEAD_W), lambda b, h: (0, 0))],
        out_specs=pl.BlockSpec(blk, lambda b, h: (b, 0, h)),
        out_shape=jax.ShapeDtypeStruct((B, T, H * HEAD_W), BF16),
        scratch_shapes=[pltpu.VMEM((2, T, HG_K), BF16),
                        pltpu.VMEM((2, T, HEAD_W), F32), pltpu.VMEM((2, T, HEAD_W), F32),
                        pltpu.VMEM((2, nc, HG_K), F32),
                        pltpu.VMEM((2, nc, HEAD_W, HG_K), F32),
                        pltpu.VMEM((2, HEAD_W, HG_K), F32),
                        pltpu.VMEM((2 * HG_K, 2 * HG_K), BF16)],
        compiler_params=_cparams(("parallel", "parallel")),
        name="hgrn2",
    )(p, p, p, pf, pf, lb, norm_g.reshape(1, HEAD_W))


def _shift_rows(x, d):
    n = x.shape[0]
    row = lax.broadcasted_iota(jnp.int32, (n, 1), 0)
    rolled = pltpu.roll(x, (-d) % n, axis=0)
    keep = (row + d >= 0) & (row + d < n)
    return jnp.where(keep, rolled, 0.0)


def _pool_minus_identity(x, w):
    n = x.shape[0]
    ahead = w - w // 2
    behind = w // 2
    fwd = x
    span = 1
    while span < ahead:
        fwd = fwd + _shift_rows(fwd, span)
        span *= 2
    bwd = x
    span = 1
    while span < behind:
        bwd = bwd + _shift_rows(bwd, -span)
        span *= 2
    total = fwd + _shift_rows(bwd, -1)
    row = lax.broadcasted_iota(jnp.int32, (n, 1), 0)
    cnt = jnp.minimum(row + ahead, n) - jnp.maximum(row - behind, 0)
    return total / cnt.astype(F32) - x


def _pool_kernel(u_ref, z_ref, w_ref, ls_ref, o_ref, r_scr, *, n_ctx):
    j = pl.program_id(1)
    T = u_ref.shape[1]
    for jj, win in enumerate(POOL_WINDOWS):
        @pl.when(j == jj)
        def _(win=win):
            for lo, hi in ((0, n_ctx), (n_ctx, T)):
                r_scr[lo:hi, :] = _pool_minus_identity(u_ref[0, lo:hi, :].astype(F32), win).astype(BF16)

    y = jnp.dot(r_scr[...], w_ref[0], preferred_element_type=F32)
    z = z_ref[0].astype(F32)
    o_ref[0] = (y * ls_ref[...] * (z * _sigmoid(z))).astype(o_ref.dtype)


def _pool_mixer(p, w_pool, ls, n_ctx):
    B, T, two_e = p.shape
    G, gw, _ = w_pool.shape
    assert G == len(POOL_WINDOWS) and two_e == 2 * G * gw
    return pl.pallas_call(
        functools.partial(_pool_kernel, n_ctx=n_ctx),
        grid=(B, G),
        in_specs=[pl.BlockSpec((1, T, gw), lambda b, j: (b, 0, j)),
                  pl.BlockSpec((1, T, gw), lambda b, j: (b, 0, G + j)),
                  pl.BlockSpec((1, gw, gw), lambda b, j: (j, 0, 0)),
                  pl.BlockSpec((1, gw), lambda b, j: (0, j))],
        out_specs=pl.BlockSpec((1, T, gw), lambda b, j: (b, 0, j)),
        out_shape=jax.ShapeDtypeStruct((B, T, G * gw), BF16),
        scratch_shapes=[pltpu.VMEM((T, gw), BF16)],
        compiler_params=_cparams(("parallel", "parallel")),
        name="pool_mixer",
    )(p, p, w_pool, ls.reshape(1, G * gw))


def _out_proj_kernel(a_ref, b_ref, w_ref, x_ref, mod_ref, g_ref, o_ref, *, tm, n_ctx, row_off):
    ka = a_ref.shape[2]
    sub = OUT_SUB if tm % OUT_SUB == 0 else tm

    def project(r0):
        y = jnp.dot(a_ref[0, r0:r0 + sub, :], w_ref[:ka, :], preferred_element_type=F32)
        return y + jnp.dot(b_ref[0, r0:r0 + sub, :], w_ref[ka:, :], preferred_element_type=F32)

    y_next = project(0)
    for r0 in range(0, tm, sub):
        y = y_next
        if r0 + sub < tm:
            y_next = project(r0 + sub)
        yn = y * lax.rsqrt(jnp.mean(y * y, axis=-1, keepdims=True) + EPS) * g_ref[...]
        gt = _row_mod(mod_ref, 2, 5, row_off + pl.program_id(1) * tm + r0, sub, n_ctx)
        o_ref[0, r0:r0 + sub, :] = x_ref[0, r0:r0 + sub, :] + gt * yn


def _out_proj(a, b, a_blk, b_blk, w, x, modrows, g, n_ctx, tm, latents_only):
    B, T, D = x.shape
    half_k = w.shape[0] // 2
    off = n_ctx // tm if latents_only else 0
    rows_out = T - n_ctx if latents_only else T
    return pl.pallas_call(
        functools.partial(_out_proj_kernel, tm=tm, n_ctx=n_ctx, row_off=off * tm),
        grid=(B, rows_out // tm),
        in_specs=[pl.BlockSpec((1, tm, half_k), lambda bb, r: (bb, r + off, a_blk)),
                  pl.BlockSpec((1, tm, half_k), lambda bb, r: (bb, r + off, b_blk)),
                  pl.BlockSpec(w.shape, lambda bb, r: (0, 0), pipeline_mode=pl.Buffered(1)),
                  pl.BlockSpec((1, tm, D), lambda bb, r: (bb, r + off, 0)),
                  pl.BlockSpec((1, SUBLANES, D), lambda bb, r: (bb, 0, 0)),
                  pl.BlockSpec((1, D), lambda bb, r: (0, 0))],
        out_specs=pl.BlockSpec((1, tm, D), lambda bb, r: (bb, r, 0)),
        out_shape=jax.ShapeDtypeStruct((B, rows_out, D), F32),
        compiler_params=_cparams(("parallel", "parallel")),
        name="out_proj",
    )(a, b, w, x, modrows, g.reshape(1, D))


def _row_tile(T, cap):
    best = NORM_SLAB
    for t in range(NORM_SLAB, min(T, cap) + 1, NORM_SLAB):
        if T % t == 0:
            best = t
    return best


def _col_tile(N, cap):
    best = 128
    for t in range(128, min(N, cap) + 1, 128):
        if N % t == 0:
            best = t
    return best


def kernel(x, c, ctx, c_ctx, w_mod, b_mod, g_pre, g_post, ev_w_in, ev_w_out, ev_lambda, ev_subln_g,
           ev_hg_lb_logits, ev_hg_norm_g, od_w_in, od_w_pool, od_scale, od_w_out):
    B, S, D = x.shape
    n_ctx = ctx.shape[1]
    depth = w_mod.shape[0]
    T = n_ctx + S
    W = DA_HEADS * HEAD_W

    xc = jnp.concatenate([ctx, x], axis=1)

    n_rows = -(-(B + 1) // SUBLANES) * SUBLANES
    cvec = jnp.zeros((n_rows, D), F32).at[:B].set(c).at[B].set(c_ctx)
    mod = _modulation(cvec, w_mod, b_mod)

    lb_cum = jnp.cumsum(jax.nn.softmax(ev_hg_lb_logits.astype(F32), axis=0), axis=0)
    lb_all = lb_cum - lb_cum[0]
    cos, sin = _rope_tables(n_ctx, S)

    tm_in = _row_tile(T, 1152)
    tm_out = _row_tile(T, 768)

    for l in range(depth):
        last = l == depth - 1
        ml = mod[l]
        lat = ml[:B].reshape(B, 3, D)
        cx = jnp.broadcast_to(ml[B].reshape(1, 3, D), (B, 3, D))
        modrows = jnp.concatenate([lat, cx, jnp.zeros((B, SUBLANES - 6, D), F32)], axis=1)

        if l % 2 == 0:
            e = l // 2
            w_in = ev_w_in[e]
            w_perm = jnp.concatenate([w_in[:, :5 * W], w_in[:, 7 * W:], w_in[:, 5 * W:7 * W]], axis=1).astype(BF16)
            p, pf = _norm_proj(xc, modrows, g_pre[l], w_perm, 2 * W, n_ctx, tm_in, _col_tile(W, 1024))

            lam_init = 0.8 - 0.6 * math.exp(-0.3 * l)
            lv = ev_lambda[e].astype(F32)
            lam = jnp.exp(jnp.sum(lv[0] * lv[1])) - jnp.exp(jnp.sum(lv[2] * lv[3])) + lam_init
            a = _diff_attention(p, lam.reshape(1), cos, sin, ev_subln_g[e], 1.0 - lam_init, n_ctx)
            lb = lb_all[e].reshape(2, HG_HEADS, HG_K).transpose(1, 0, 2)
            bh = _hgrn(p, pf, lb, ev_hg_norm_g[e], n_ctx, 4)
            w_out = ev_w_out[e].astype(BF16)
            ya, yb, ia, ib = a, bh, 0, 0
        else:
            o = l // 2
            p = _norm_proj(xc, modrows, g_pre[l], od_w_in[o].astype(BF16), 0, n_ctx, tm_in,
                           _col_tile(od_w_in.shape[2], 1024))
            y = _pool_mixer(p, od_w_pool[o].astype(BF16), od_scale[o], n_ctx)
            w_out = od_w_out[o].astype(BF16)
            ya, yb, ia, ib = y, y, 0, 1

        xc = _out_proj(ya, yb, ia, ib, w_out, xc, modrows, g_post[l], n_ctx,
                       n_ctx if last else tm_out, latents_only=last)
    return xc
```

```python
import functools
import math

import jax
import jax.numpy as jnp
from jax import lax
from jax.experimental import pallas as pl
from jax.experimental.pallas import tpu as pltpu

DA_HEADS = 8
DA_D = 64
HEAD_W = 2 * DA_D
HG_HEADS = 8
HG_K = 128
GRID_W = 64
ROPE_BASE = 10000.0
EPS = 1e-6
POOL_WINDOWS = (2, 4, 8, 16)
HG_CHUNK = 16
LOG2E = math.log2(math.e)
ATTN_Q_BLOCK = 768
ATTN_Q_SUB = 128
ATTN_AHEAD = 3
HG_GROUP = 16
HG_PREP_UNROLL = 8
SUBLANES = 8
OUT_SUB = 256
NORM_SLAB = 16
NORM_UNROLL = 4

F32 = jnp.float32
BF16 = jnp.bfloat16
VMEM_LIMIT = 56 * 1024 * 1024


def _cparams(sem):
    return pltpu.CompilerParams(dimension_semantics=sem, vmem_limit_bytes=VMEM_LIMIT)


def _sigmoid(x):
    return 1.0 / (1.0 + jnp.exp(-x))


def _mod_kernel(c_ref, w_ref, b_ref, o_ref):
    c = c_ref[...]
    s = c * _sigmoid(c)
    o_ref[0] = jnp.dot(s, w_ref[0], preferred_element_type=F32,
                       precision=lax.Precision.HIGHEST) + b_ref[0]


def _modulation(cvec, w_mod, b_mod):
    L, D, N = w_mod.shape
    R = cvec.shape[0]
    tn = 1024 if N % 1024 == 0 else N
    return pl.pallas_call(
        _mod_kernel,
        grid=(L, N // tn),
        in_specs=[pl.BlockSpec((R, D), lambda l, j: (0, 0)),
                  pl.BlockSpec((1, D, tn), lambda l, j: (l, 0, j)),
                  pl.BlockSpec((1, 1, tn), lambda l, j: (l, 0, j))],
        out_specs=pl.BlockSpec((1, R, tn), lambda l, j: (l, 0, j)),
        out_shape=jax.ShapeDtypeStruct((L, R, N), F32),
        compiler_params=_cparams(("parallel", "parallel")),
        name="modulation",
    )(cvec, w_mod, b_mod.reshape(L, 1, N))


def _row_mod(mod_ref, lat_row, ctx_row, row0, tm, n_ctx):
    rows = row0 + lax.broadcasted_iota(jnp.int32, (tm, 1), 0)
    return jnp.where(rows < n_ctx, mod_ref[0, ctx_row:ctx_row + 1, :], mod_ref[0, lat_row:lat_row + 1, :])


def _norm_proj_kernel(x_ref, mod_ref, g_ref, w_ref, o_ref, *rest, tm, n_ctx, n_lo_tiles):
    h_scr, gs_scr = rest[-2:]
    r = pl.program_id(1)
    j = pl.program_id(2)

    @pl.when(j == 0)
    def _():
        gs_scr[0:1, :] = g_ref[...] * (1.0 + mod_ref[0, 1:2, :])
        gs_scr[1:2, :] = mod_ref[0, 0:1, :]
        gs_scr[2:3, :] = g_ref[...] * (1.0 + mod_ref[0, 4:5, :])
        gs_scr[3:4, :] = mod_ref[0, 3:4, :]

        def slab(k, carry):
            r0 = pl.multiple_of(k * NORM_SLAB, NORM_SLAB)
            base = jnp.where(r * tm + r0 < n_ctx, 2, 0)
            x = x_ref[0, pl.ds(r0, NORM_SLAB), :]
            y = x * lax.rsqrt(jnp.mean(x * x, axis=-1, keepdims=True) + EPS)
            h_scr[pl.ds(r0, NORM_SLAB), :] = (y * gs_scr[pl.ds(base, 1), :] + gs_scr[pl.ds(base + 1, 1), :]).astype(BF16)
            return carry

        lax.fori_loop(0, tm // NORM_SLAB, slab, 0, unroll=NORM_UNROLL)

    if len(rest) == 2:
        o_ref[0] = jnp.dot(h_scr[...], w_ref[...], preferred_element_type=F32).astype(o_ref.dtype)
    else:
        @pl.when(j < n_lo_tiles)
        def _():
            o_ref[0] = jnp.dot(h_scr[...], w_ref[...], preferred_element_type=F32).astype(o_ref.dtype)

        @pl.when(j >= n_lo_tiles)
        def _():
            rest[0][0] = jnp.dot(h_scr[...], w_ref[...], preferred_element_type=F32)


def _norm_proj(x, modrows, g, w, n_f32, n_ctx, tm, tn):
    B, T, D = x.shape
    n_lo = w.shape[1] - n_f32
    assert n_lo % tn == 0 and n_f32 % tn == 0
    lo_tiles, hi_tiles = n_lo // tn, n_f32 // tn
    out_specs = [pl.BlockSpec((1, tm, tn), lambda b, r, j: (b, r, jnp.minimum(j, lo_tiles - 1)))]
    out_shape = [jax.ShapeDtypeStruct((B, T, n_lo), BF16)]
    if hi_tiles:
        out_specs.append(pl.BlockSpec((1, tm, tn), lambda b, r, j: (b, r, jnp.maximum(j - lo_tiles, 0))))
        out_shape.append(jax.ShapeDtypeStruct((B, T, n_f32), F32))
    outs = pl.pallas_call(
        functools.partial(_norm_proj_kernel, tm=tm, n_ctx=n_ctx, n_lo_tiles=lo_tiles),
        grid=(B, T // tm, lo_tiles + hi_tiles),
        in_specs=[pl.BlockSpec((1, tm, D), lambda b, r, j: (b, r, 0)),
                  pl.BlockSpec((1, SUBLANES, D), lambda b, r, j: (b, 0, 0)),
                  pl.BlockSpec((1, D), lambda b, r, j: (0, 0)),
                  pl.BlockSpec((D, tn), lambda b, r, j: (0, j))],
        out_specs=out_specs,
        out_shape=out_shape,
        scratch_shapes=[pltpu.VMEM((tm, D), BF16), pltpu.VMEM((4, D), F32)],
        compiler_params=_cparams(("parallel", "parallel", "arbitrary")),
        name="norm_proj",
    )(x, modrows, g.reshape(1, D), w)
    return outs if hi_tiles else outs[0]


def _rope(x, cos, sin_signed):
    lane = lax.broadcasted_iota(jnp.int32, x.shape, 1)
    n = x.shape[1]
    partner = jnp.where(lane % 32 < 16, pltpu.roll(x, n - 16, axis=1), pltpu.roll(x, 16, axis=1))
    return x * cos + partner * sin_signed


def _attn_kernel(lam_ref, q_ref, k_ref, v_ref, gate_ref, cq_ref, sq_ref, ck_ref, sk_ref, g_ref,
                 o_ref, k_scr, vt_scr, *, tq, n_ctx, out_scale):
    i = pl.program_id(2)
    n_keys = k_scr.shape[0]

    @pl.when(i == 0)
    def _():
        k_scr[...] = _rope(k_ref[0].astype(F32), ck_ref[...], sk_ref[...]).astype(BF16)
        vt_scr[...] = v_ref[0].astype(F32).T.astype(BF16)

    qr = _rope(q_ref[0].astype(F32), cq_ref[...], sq_ref[...]) * (DA_D ** -0.5 * LOG2E)
    first_map = lax.broadcasted_iota(jnp.int32, qr.shape, 1) < DA_D
    q_maps = (jnp.where(first_map, qr, 0.0).astype(BF16), jnp.where(first_map, 0.0, qr).astype(BF16))
    lam = lam_ref[0]
    qs = ATTN_Q_SUB

    def attend(n_ctx_rows):
        def scores(r0):
            nk = n_ctx if r0 < n_ctx_rows else n_keys
            q2 = jnp.concatenate([q_maps[0][r0:r0 + qs], q_maps[1][r0:r0 + qs]], axis=0)
            return lax.dot_general(k_scr[:nk, :], q2, (((1,), (1,)), ((), ())),
                                   preferred_element_type=F32)

        starts = list(range(0, tq, qs))
        pending = [scores(r0) for r0 in starts[:ATTN_AHEAD]]
        for n, r0 in enumerate(starts):
            rows = slice(r0, r0 + qs)
            s = pending.pop(0)
            nk = s.shape[0]
            if n + ATTN_AHEAD < len(starts):
                pending.append(scores(starts[n + ATTN_AHEAD]))
            e = jnp.exp2(s - jnp.max(s, axis=0, keepdims=True))
            rinv = 1.0 / jnp.sum(e, axis=0, keepdims=True)
            ov = jnp.dot(vt_scr[:, :nk], e.astype(BF16), preferred_element_type=F32) * rinv
            o = (ov[:, :qs] - lam * ov[:, qs:]).T
            y = o * lax.rsqrt(jnp.mean(o * o, axis=-1, keepdims=True) + EPS) * g_ref[...] * out_scale
            gate = gate_ref[0, rows, :].astype(F32)
            o_ref[0, rows, :] = (y * (gate * _sigmoid(gate))).astype(o_ref.dtype)

    @pl.when(i == 0)
    def _():
        attend(n_ctx)

    @pl.when(i != 0)
    def _():
        attend(0)


def _diff_attention(p, lam, cos, sin, subln_g, out_scale, n_ctx):
    B, T, _ = p.shape
    tq = max(t for t in range(n_ctx, ATTN_Q_BLOCK + 1, n_ctx) if T % t == 0)
    assert n_ctx % ATTN_Q_SUB == 0
    H = DA_HEADS
    head = lambda part: (lambda b, h, i: (b, i, part * H + h))
    head_all = lambda part: (lambda b, h, i: (b, 0, part * H + h))
    return pl.pallas_call(
        functools.partial(_attn_kernel, tq=tq, n_ctx=n_ctx, out_scale=out_scale),
        grid=(B, H, T // tq),
        in_specs=[pl.BlockSpec(memory_space=pltpu.SMEM),
                  pl.BlockSpec((1, tq, HEAD_W), head(0)),
                  pl.BlockSpec((1, T, HEAD_W), head_all(1)),
                  pl.BlockSpec((1, T, HEAD_W), head_all(2)),
                  pl.BlockSpec((1, tq, HEAD_W), head(3)),
                  pl.BlockSpec((tq, HEAD_W), lambda b, h, i: (i, 0)),
                  pl.BlockSpec((tq, HEAD_W), lambda b, h, i: (i, 0)),
                  pl.BlockSpec((T, HEAD_W), lambda b, h, i: (0, 0)),
                  pl.BlockSpec((T, HEAD_W), lambda b, h, i: (0, 0)),
                  pl.BlockSpec((1, HEAD_W), lambda b, h, i: (0, 0))],
        out_specs=pl.BlockSpec((1, tq, HEAD_W), lambda b, h, i: (b, i, h)),
        out_shape=jax.ShapeDtypeStruct((B, T, H * HEAD_W), BF16),
        scratch_shapes=[pltpu.VMEM((T, HEAD_W), BF16), pltpu.VMEM((HEAD_W, T), BF16)],
        compiler_params=_cparams(("parallel", "parallel", "arbitrary")),
        name="diff_attention",
    )(lam, p, p, p, p, cos, sin, cos, sin, subln_g.reshape(1, HEAD_W))


def _rope_tables(n_ctx, seq):
    n_freq = DA_D // 4
    inv = ROPE_BASE ** (-jnp.arange(n_freq, dtype=F32) / n_freq)
    t = jnp.arange(seq)
    lane = jnp.arange(HEAD_W)
    pos = jnp.where((lane % DA_D < DA_D // 2)[None, :], (t // GRID_W)[:, None], (t % GRID_W)[:, None]).astype(F32)
    ang = pos * inv[lane % n_freq][None, :]
    sign = jnp.where(lane % 32 < 16, -1.0, 1.0).astype(F32)[None, :]
    cos = jnp.concatenate([jnp.ones((n_ctx, HEAD_W), F32), jnp.cos(ang)], axis=0)
    sin = jnp.concatenate([jnp.zeros((n_ctx, HEAD_W), F32), jnp.sin(ang) * sign], axis=0)
    return cos, sin


def _chunk_cumsum(g, reverse):
    n = g.shape[0]
    row = lax.broadcasted_iota(jnp.int32, g.shape, 0)
    b = g
    sh = 1
    while sh < n:
        if reverse:
            b = b + jnp.where(row < n - sh, pltpu.roll(b, n - sh, axis=0), 0.0)
        else:
            b = b + jnp.where(row >= sh, pltpu.roll(b, sh, axis=0), 0.0)
        sh *= 2
    return b


def _score_tiles(qq, b2, c2, reverse):
    C = qq.shape[0]
    half = SUBLANES
    sub = lax.broadcasted_iota(jnp.int32, (half, qq.shape[1]), 0)
    tiles = []
    for s in range(C):
        sb = s // half
        blocks = range(sb + 1) if reverse else range(sb, C // half)
        for tb in blocks:
            rows = slice(tb * half, (tb + 1) * half)
            w = qq[rows] * jnp.exp2(b2[rows] - c2[s:s + 1])
            if tb == sb:
                keep = (sub <= s - sb * half) if reverse else (sub >= s - sb * half)
                w = jnp.where(keep, w, 0.0)
            tiles.append((s, tb, w))
    return tiles


def _lane_sums(tiles, ones_ref):
    n = len(tiles)
    assert n % 2 == 0
    lhs = jnp.concatenate([jnp.concatenate([tiles[2 * i], tiles[2 * i + 1]], axis=1) for i in range(n // 2)], axis=0)
    r = jnp.dot(lhs.astype(BF16), ones_ref[...], preferred_element_type=F32)
    w = tiles[0].shape[1]
    out = []
    for i in range(n // 2):
        rows = slice(i * SUBLANES, (i + 1) * SUBLANES)
        out += [r[rows, :w], r[rows, w:]]
    return out


def _hgrn_kernel(q_ref, v_ref, gate_ref, zf_ref, zb_ref, lb_ref, g_ref, o_ref,
                 qt_scr, oi_scr, ox_scr, dl_scr, u_scr, st_scr, ones_scr, *, n_ctx):
    T = q_ref.shape[1]
    C = HG_CHUNK
    K = HG_K
    nc = T // C
    nc_ctx = n_ctx // C
    same_half = (lax.broadcasted_iota(jnp.int32, (2 * K, 2 * K), 0) < K) == (lax.broadcasted_iota(jnp.int32, (2 * K, 2 * K), 1) < K)
    ones_scr[...] = jnp.where(same_half, 1.0, 0.0).astype(BF16)

    def prep(c, carry):
        r0 = pl.multiple_of(c * C, C)
        hq = q_ref[0, pl.ds(r0, C), :].astype(F32)
        qq = hq * _sigmoid(hq)
        vb = v_ref[0, pl.ds(r0, C), :]
        v = vb.astype(F32)
        tiles, keys = [], []
        for d, z_ref in enumerate((zf_ref, zb_ref)):
            reverse = d == 1
            z = z_ref[0, pl.ds(r0, C), :]
            lb = lb_ref[0, d:d + 1, :]
            e = jnp.exp(-jnp.abs(z))
            inv = 1.0 / (1.0 + e)
            sig = jnp.where(z >= 0, inv, e * inv)
            nsig = jnp.where(z >= 0, e * inv, inv)
            lk2 = jnp.log2(1.0 - lb) + jnp.log2(nsig)
            b2 = _chunk_cumsum(jnp.log2(lb + (1.0 - lb) * sig), reverse)
            b2_last = b2[0:1] if reverse else b2[C - 1:C]
            qt_scr[d, pl.ds(r0, C), :] = (qq * jnp.exp2(b2)).astype(BF16)
            keys.append(jnp.exp2(lk2 + (b2_last - b2)).astype(BF16))
            dl_scr[d, pl.ds(c, 1), :] = jnp.exp2(b2_last)
            tiles.append(_score_tiles(qq, b2, b2 - lk2, reverse))
        sums = _lane_sums([w for per_dir in tiles for _, _, w in per_dir], ones_scr)
        k = 0
        for d in range(2):
            parts = [jnp.zeros((SUBLANES, v.shape[1]), F32) for _ in range(C // SUBLANES)]
            for s, tb, _ in tiles[d]:
                parts[tb] = parts[tb] + sums[k] * v[s:s + 1]
                k += 1
            oi_scr[d, pl.ds(r0, C), :] = jnp.concatenate(parts, axis=0)
        upd = lax.dot_general(vb, jnp.concatenate(keys, axis=1), (((0,), (0,)), ((), ())),
                              preferred_element_type=F32)
        u_scr[0, c] = upd[:, :K]
        u_scr[1, c] = upd[:, K:]
        return carry

    lax.fori_loop(0, nc, prep, 0, unroll=HG_PREP_UNROLL)

    st_scr[...] = jnp.zeros_like(st_scr)

    def scan(m, carry):
        i0 = m * HG_GROUP
        cb0 = jnp.where(i0 < nc_ctx, nc_ctx - 1 - i0, nc + nc_ctx - 1 - i0)
        for j in range(HG_GROUP):
            for d, c in enumerate((i0 + j, cb0 - j)):
                r0 = pl.multiple_of(c * C, C)
                st = st_scr[d]
                ox_scr[d, pl.ds(r0, C), :] = jnp.dot(qt_scr[d, pl.ds(r0, C), :], st.T.astype(BF16),
                                                     preferred_element_type=F32)
                st_scr[d] = dl_scr[d, pl.ds(c, 1), :] * st + u_scr[d, c]
        return carry

    assert nc % HG_GROUP == 0 and nc_ctx % HG_GROUP == 0
    lax.fori_loop(0, nc // HG_GROUP, scan, 0)

    o = (oi_scr[0] + ox_scr[0]) + (oi_scr[1] + ox_scr[1])
    y = o * lax.rsqrt(jnp.mean(o * o, axis=-1, keepdims=True) + EPS) * g_ref[...]
    gate = gate_ref[0].astype(F32)
    o_ref[0] = (y * (gate * _sigmoid(gate))).astype(o_ref.dtype)


def _hgrn(p, pf, lb, norm_g, n_ctx, part0):
    B, T, _ = p.shape
    H = HG_HEADS
    nc = T // HG_CHUNK
    part = lambda k: (lambda b, h: (b, 0, k * H + h))
    blk = (1, T, HEAD_W)
    return pl.pallas_call(
        functools.partial(_hgrn_kernel, n_ctx=n_ctx),
        grid=(B, H),
        in_specs=[pl.BlockSpec(blk, part(part0)), pl.BlockSpec(blk, part(part0 + 1)),
                  pl.BlockSpec(blk, part(part0 + 2)),
                  pl.BlockSpec(blk, part(0)), pl.BlockSpec(blk, part(1)),
                  pl.BlockSpec((1, 2, HG_K), lambda b, h: (h, 0, 0)),
                  pl.BlockSpec((1, HEAD_W), lambda b, h: (0, 0))],
        out_specs=pl.BlockSpec(blk, lambda b, h: (b, 0, h)),
        out_shape=jax.ShapeDtypeStruct((B, T, H * HEAD_W), BF16),
        scratch_shapes=[pltpu.VMEM((2, T, HG_K), BF16),
                        pltpu.VMEM((2, T, HEAD_W), F32), pltpu.VMEM((2, T, HEAD_W), F32),
                        pltpu.VMEM((2, nc, HG_K), F32),
                        pltpu.VMEM((2, nc, HEAD_W, HG_K), F32),
                        pltpu.VMEM((2, HEAD_W, HG_K), F32),
                        pltpu.VMEM((2 * HG_K, 2 * HG_K), BF16)],
        compiler_params=_cparams(("parallel", "parallel")),
        name="hgrn2",
    )(p, p, p, pf, pf, lb, norm_g.reshape(1, HEAD_W))


def _shift_rows(x, d):
    n = x.shape[0]
    row = lax.broadcasted_iota(jnp.int32, (n, 1), 0)
    rolled = pltpu.roll(x, (-d) % n, axis=0)
    keep = (row + d >= 0) & (row + d < n)
    return jnp.where(keep, rolled, 0.0)


def _pool_minus_identity(x, w):
    n = x.shape[0]
    ahead = w - w // 2
    behind = w // 2
    fwd = x
    span = 1
    while span < ahead:
        fwd = fwd + _shift_rows(fwd, span)
        span *= 2
    bwd = x
    span = 1
    while span < behind:
        bwd = bwd + _shift_rows(bwd, -span)
        span *= 2
    total = fwd + _shift_rows(bwd, -1)
    row = lax.broadcasted_iota(jnp.int32, (n, 1), 0)
    cnt = jnp.minimum(row + ahead, n) - jnp.maximum(row - behind, 0)
    return total / cnt.astype(F32) - x


def _pool_kernel(u_ref, z_ref, w_ref, ls_ref, o_ref, r_scr, *, n_ctx):
    j = pl.program_id(1)
    T = u_ref.shape[1]
    for jj, win in enumerate(POOL_WINDOWS):
        @pl.when(j == jj)
        def _(win=win):
            for lo, hi in ((0, n_ctx), (n_ctx, T)):
                r_scr[lo:hi, :] = _pool_minus_identity(u_ref[0, lo:hi, :].astype(F32), win).astype(BF16)

    y = jnp.dot(r_scr[...], w_ref[0], preferred_element_type=F32)
    z = z_ref[0].astype(F32)
    o_ref[0] = (y * ls_ref[...] * (z * _sigmoid(z))).astype(o_ref.dtype)


def _pool_mixer(p, w_pool, ls, n_ctx):
    B, T, two_e = p.shape
    G, gw, _ = w_pool.shape
    assert G == len(POOL_WINDOWS) and two_e == 2 * G * gw
    return pl.pallas_call(
        functools.partial(_pool_kernel, n_ctx=n_ctx),
        grid=(B, G),
        in_specs=[pl.BlockSpec((1, T, gw), lambda b, j: (b, 0, j)),
                  pl.BlockSpec((1, T, gw), lambda b, j: (b, 0, G + j)),
                  pl.BlockSpec((1, gw, gw), lambda b, j: (j, 0, 0)),
                  pl.BlockSpec((1, gw), lambda b, j: (0, j))],
        out_specs=pl.BlockSpec((1, T, gw), lambda b, j: (b, 0, j)),
        out_shape=jax.ShapeDtypeStruct((B, T, G * gw), BF16),
        scratch_shapes=[pltpu.VMEM((T, gw), BF16)],
        compiler_params=_cparams(("parallel", "parallel")),
        name="pool_mixer",
    )(p, p, w_pool, ls.reshape(1, G * gw))


def _out_proj_kernel(a_ref, b_ref, w_ref, x_ref, mod_ref, g_ref, o_ref, *, tm, n_ctx, row_off):
    ka = a_ref.shape[2]
    sub = OUT_SUB if tm % OUT_SUB == 0 else tm

    def project(r0):
        y = jnp.dot(a_ref[0, r0:r0 + sub, :], w_ref[:ka, :], preferred_element_type=F32)
        return y + jnp.dot(b_ref[0, r0:r0 + sub, :], w_ref[ka:, :], preferred_element_type=F32)

    y_next = project(0)
    for r0 in range(0, tm, sub):
        y = y_next
        if r0 + sub < tm:
            y_next = project(r0 + sub)
        yn = y * lax.rsqrt(jnp.mean(y * y, axis=-1, keepdims=True) + EPS) * g_ref[...]
        gt = _row_mod(mod_ref, 2, 5, row_off + pl.program_id(1) * tm + r0, sub, n_ctx)
        o_ref[0, r0:r0 + sub, :] = x_ref[0, r0:r0 + sub, :] + gt * yn


def _out_proj(a, b, a_blk, b_blk, w, x, modrows, g, n_ctx, tm, latents_only):
    B, T, D = x.shape
    half_k = w.shape[0] // 2
    off = n_ctx // tm if latents_only else 0
    rows_out = T - n_ctx if latents_only else T
    return pl.pallas_call(
        functools.partial(_out_proj_kernel, tm=tm, n_ctx=n_ctx, row_off=off * tm),
        grid=(B, rows_out // tm),
        in_specs=[pl.BlockSpec((1, tm, half_k), lambda bb, r: (bb, r + off, a_blk)),
                  pl.BlockSpec((1, tm, half_k), lambda bb, r: (bb, r + off, b_blk)),
                  pl.BlockSpec(w.shape, lambda bb, r: (0, 0), pipeline_mode=pl.Buffered(1)),
                  pl.BlockSpec((1, tm, D), lambda bb, r: (bb, r + off, 0)),
                  pl.BlockSpec((1, SUBLANES, D), lambda bb, r: (bb, 0, 0)),
                  pl.BlockSpec((1, D), lambda bb, r: (0, 0))],
        out_specs=pl.BlockSpec((1, tm, D), lambda bb, r: (bb, r, 0)),
        out_shape=jax.ShapeDtypeStruct((B, rows_out, D), F32),
        compiler_params=_cparams(("parallel", "parallel")),
        name="out_proj",
    )(a, b, w, x, modrows, g.reshape(1, D))


def _row_tile(T, cap):
    best = NORM_SLAB
    for t in range(NORM_SLAB, min(T, cap) + 1, NORM_SLAB):
        if T % t == 0:
            best = t
    return best


def _col_tile(N, cap):
    best = 128
    for t in range(128, min(N, cap) + 1, 128):
        if N % t == 0:
            best = t
    return best


def kernel(x, c, ctx, c_ctx, w_mod, b_mod, g_pre, g_post, ev_w_in, ev_w_out, ev_lambda, ev_subln_g,
           ev_hg_lb_logits, ev_hg_norm_g, od_w_in, od_w_pool, od_scale, od_w_out):
    B, S, D = x.shape
    n_ctx = ctx.shape[1]
    depth = w_mod.shape[0]
    T = n_ctx + S
    W = DA_HEADS * HEAD_W

    xc = jnp.concatenate([ctx, x], axis=1)

    n_rows = -(-(B + 1) // SUBLANES) * SUBLANES
    cvec = jnp.zeros((n_rows, D), F32).at[:B].set(c).at[B].set(c_ctx)
    mod = _modulation(cvec, w_mod, b_mod)

    lb_cum = jnp.cumsum(jax.nn.softmax(ev_hg_lb_logits.astype(F32), axis=0), axis=0)
    lb_all = lb_cum - lb_cum[0]
    cos, sin = _rope_tables(n_ctx, S)

    tm_in = _row_tile(T, 1152)
    tm_out = _row_tile(T, 768)

    for l in range(depth):
        last = l == depth - 1
        ml = mod[l]
        lat = ml[:B].reshape(B, 3, D)
        cx = jnp.broadcast_to(ml[B].reshape(1, 3, D), (B, 3, D))
        modrows = jnp.concatenate([lat, cx, jnp.zeros((B, SUBLANES - 6, D), F32)], axis=1)

        if l % 2 == 0:
            e = l // 2
            w_in = ev_w_in[e]
            w_perm = jnp.concatenate([w_in[:, :5 * W], w_in[:, 7 * W:], w_in[:, 5 * W:7 * W]], axis=1).astype(BF16)
            p, pf = _norm_proj(xc, modrows, g_pre[l], w_perm, 2 * W, n_ctx, tm_in, _col_tile(W, 1024))

            lam_init = 0.8 - 0.6 * math.exp(-0.3 * l)
            lv = ev_lambda[e].astype(F32)
            lam = jnp.exp(jnp.sum(lv[0] * lv[1])) - jnp.exp(jnp.sum(lv[2] * lv[3])) + lam_init
            a = _diff_attention(p, lam.reshape(1), cos, sin, ev_subln_g[e], 1.0 - lam_init, n_ctx)
            lb = lb_all[e].reshape(2, HG_HEADS, HG_K).transpose(1, 0, 2)
            bh = _hgrn(p, pf, lb, ev_hg_norm_g[e], n_ctx, 4)
            w_out = ev_w_out[e].astype(BF16)
            ya, yb, ia, ib = a, bh, 0, 0
        else:
            o = l // 2
            p = _norm_proj(xc, modrows, g_pre[l], od_w_in[o].astype(BF16), 0, n_ctx, tm_in,
                           _col_tile(od_w_in.shape[2], 1024))
            y = _pool_mixer(p, od_w_pool[o].astype(BF16), od_scale[o], n_ctx)
            w_out = od_w_out[o].astype(BF16)
            ya, yb, ia, ib = y, y, 0, 1

        xc = _out_proj(ya, yb, ia, ib, w_out, xc, modrows, g_post[l], n_ctx,
                       n_ctx if last else tm_out, latents_only=last)
    return xc
```

```python
import functools
import math

import jax
import jax.numpy as jnp
from jax import lax
from jax.experimental import pallas as pl
from jax.experimental.pallas import tpu as pltpu

DA_HEADS = 8
DA_D = 64
HEAD_W = 2 * DA_D
HG_HEADS = 8
HG_K = 128
GRID_W = 64
ROPE_BASE = 10000.0
EPS = 1e-6
POOL_WINDOWS = (2, 4, 8, 16)
HG_CHUNK = 16
LOG2E = math.log2(math.e)
ATTN_Q_BLOCK = 2304
ATTN_Q_SUB = 128
ATTN_AHEAD = 5
HG_GROUP = 16
HG_PREP_UNROLL = 8
SUBLANES = 8
OUT_SUB = 256
NORM_SLAB = 16
NORM_UNROLL = 4

F32 = jnp.float32
BF16 = jnp.bfloat16
VMEM_LIMIT = 56 * 1024 * 1024


def _cparams(sem):
    return pltpu.CompilerParams(dimension_semantics=sem, vmem_limit_bytes=VMEM_LIMIT)


def _sigmoid(x):
    return 1.0 / (1.0 + jnp.exp(-x))


def _mod_kernel(c_ref, w_ref, b_ref, o_ref):
    c = c_ref[...]
    s = c * _sigmoid(c)
    o_ref[0] = jnp.dot(s, w_ref[0], preferred_element_type=F32,
                       precision=lax.Precision.HIGHEST) + b_ref[0]


def _modulation(cvec, w_mod, b_mod):
    L, D, N = w_mod.shape
    R = cvec.shape[0]
    tn = 1024 if N % 1024 == 0 else N
    return pl.pallas_call(
        _mod_kernel,
        grid=(L, N // tn),
        in_specs=[pl.BlockSpec((R, D), lambda l, j: (0, 0)),
                  pl.BlockSpec((1, D, tn), lambda l, j: (l, 0, j)),
                  pl.BlockSpec((1, 1, tn), lambda l, j: (l, 0, j))],
        out_specs=pl.BlockSpec((1, R, tn), lambda l, j: (l, 0, j)),
        out_shape=jax.ShapeDtypeStruct((L, R, N), F32),
        compiler_params=_cparams(("parallel", "parallel")),
        name="modulation",
    )(cvec, w_mod, b_mod.reshape(L, 1, N))


def _row_mod(mod_ref, lat_row, ctx_row, row0, tm, n_ctx):
    rows = row0 + lax.broadcasted_iota(jnp.int32, (tm, 1), 0)
    return jnp.where(rows < n_ctx, mod_ref[0, ctx_row:ctx_row + 1, :], mod_ref[0, lat_row:lat_row + 1, :])


def _norm_proj_kernel(x_ref, mod_ref, g_ref, w_ref, o_ref, *rest, tm, n_ctx, n_lo_tiles):
    h_scr, gs_scr = rest[-2:]
    r = pl.program_id(1)
    j = pl.program_id(2)

    @pl.when(j == 0)
    def _():
        gs_scr[0:1, :] = g_ref[...] * (1.0 + mod_ref[0, 1:2, :])
        gs_scr[1:2, :] = mod_ref[0, 0:1, :]
        gs_scr[2:3, :] = g_ref[...] * (1.0 + mod_ref[0, 4:5, :])
        gs_scr[3:4, :] = mod_ref[0, 3:4, :]

        def slab(k, carry):
            r0 = pl.multiple_of(k * NORM_SLAB, NORM_SLAB)
            base = jnp.where(r * tm + r0 < n_ctx, 2, 0)
            x = x_ref[0, pl.ds(r0, NORM_SLAB), :]
            y = x * lax.rsqrt(jnp.mean(x * x, axis=-1, keepdims=True) + EPS)
            h_scr[pl.ds(r0, NORM_SLAB), :] = (y * gs_scr[pl.ds(base, 1), :] + gs_scr[pl.ds(base + 1, 1), :]).astype(BF16)
            return carry

        lax.fori_loop(0, tm // NORM_SLAB, slab, 0, unroll=NORM_UNROLL)

    if len(rest) == 2:
        o_ref[0] = jnp.dot(h_scr[...], w_ref[...], preferred_element_type=F32).astype(o_ref.dtype)
    else:
        @pl.when(j < n_lo_tiles)
        def _():
            o_ref[0] = jnp.dot(h_scr[...], w_ref[...], preferred_element_type=F32).astype(o_ref.dtype)

        @pl.when(j >= n_lo_tiles)
        def _():
            rest[0][0] = jnp.dot(h_scr[...], w_ref[...], preferred_element_type=F32)


def _norm_proj(x, modrows, g, w, n_f32, n_ctx, tm, tn):
    B, T, D = x.shape
    n_lo = w.shape[1] - n_f32
    assert n_lo % tn == 0 and n_f32 % tn == 0
    lo_tiles, hi_tiles = n_lo // tn, n_f32 // tn
    out_specs = [pl.BlockSpec((1, tm, tn), lambda b, r, j: (b, r, jnp.minimum(j, lo_tiles - 1)))]
    out_shape = [jax.ShapeDtypeStruct((B, T, n_lo), BF16)]
    if hi_tiles:
        out_specs.append(pl.BlockSpec((1, tm, tn), lambda b, r, j: (b, r, jnp.maximum(j - lo_tiles, 0))))
        out_shape.append(jax.ShapeDtypeStruct((B, T, n_f32), F32))
    outs = pl.pallas_call(
        functools.partial(_norm_proj_kernel, tm=tm, n_ctx=n_ctx, n_lo_tiles=lo_tiles),
        grid=(B, T // tm, lo_tiles + hi_tiles),
        in_specs=[pl.BlockSpec((1, tm, D), lambda b, r, j: (b, r, 0)),
                  pl.BlockSpec((1, SUBLANES, D), lambda b, r, j: (b, 0, 0)),
                  pl.BlockSpec((1, D), lambda b, r, j: (0, 0)),
                  pl.BlockSpec((D, tn), lambda b, r, j: (0, j))],
        out_specs=out_specs,
        out_shape=out_shape,
        scratch_shapes=[pltpu.VMEM((tm, D), BF16), pltpu.VMEM((4, D), F32)],
        compiler_params=_cparams(("parallel", "parallel", "arbitrary")),
        name="norm_proj",
    )(x, modrows, g.reshape(1, D), w)
    return outs if hi_tiles else outs[0]


def _rope(x, cos, sin_signed):
    lane = lax.broadcasted_iota(jnp.int32, x.shape, 1)
    n = x.shape[1]
    partner = jnp.where(lane % 32 < 16, pltpu.roll(x, n - 16, axis=1), pltpu.roll(x, 16, axis=1))
    return x * cos + partner * sin_signed


def _attn_kernel(lam_ref, q_ref, k_ref, v_ref, gate_ref, cq_ref, sq_ref, ck_ref, sk_ref, g_ref,
                 o_ref, k_scr, vt_scr, *, tq, n_ctx, out_scale):
    i = pl.program_id(2)
    n_keys = k_scr.shape[0]

    @pl.when(i == 0)
    def _():
        k_scr[...] = _rope(k_ref[0].astype(F32), ck_ref[...], sk_ref[...]).astype(BF16)
        vt_scr[...] = v_ref[0].astype(F32).T.astype(BF16)

    qr = _rope(q_ref[0].astype(F32), cq_ref[...], sq_ref[...]) * (DA_D ** -0.5 * LOG2E)
    first_map = lax.broadcasted_iota(jnp.int32, qr.shape, 1) < DA_D
    q_maps = (jnp.where(first_map, qr, 0.0).astype(BF16), jnp.where(first_map, 0.0, qr).astype(BF16))
    lam = lam_ref[0]
    qs = ATTN_Q_SUB

    def attend(n_ctx_rows):
        def scores(r0):
            nk = n_ctx if r0 < n_ctx_rows else n_keys
            q2 = jnp.concatenate([q_maps[0][r0:r0 + qs], q_maps[1][r0:r0 + qs]], axis=0)
            return lax.dot_general(k_scr[:nk, :], q2, (((1,), (1,)), ((), ())),
                                   preferred_element_type=F32)

        starts = list(range(0, tq, qs))
        pending = [scores(r0) for r0 in starts[:ATTN_AHEAD]]
        for n, r0 in enumerate(starts):
            rows = slice(r0, r0 + qs)
            s = pending.pop(0)
            nk = s.shape[0]
            if n + ATTN_AHEAD < len(starts):
                pending.append(scores(starts[n + ATTN_AHEAD]))
            e = jnp.exp2(s - jnp.max(s, axis=0, keepdims=True))
            rinv = 1.0 / jnp.sum(e, axis=0, keepdims=True)
            ov = jnp.dot(vt_scr[:, :nk], e.astype(BF16), preferred_element_type=F32) * rinv
            o = (ov[:, :qs] - lam * ov[:, qs:]).T
            y = o * lax.rsqrt(jnp.mean(o * o, axis=-1, keepdims=True) + EPS) * g_ref[...] * out_scale
            gate = gate_ref[0, rows, :].astype(F32)
            o_ref[0, rows, :] = (y * (gate * _sigmoid(gate))).astype(o_ref.dtype)

    @pl.when(i == 0)
    def _():
        attend(n_ctx)

    @pl.when(i != 0)
    def _():
        attend(0)


def _diff_attention(p, lam, cos, sin, subln_g, out_scale, n_ctx):
    B, T, _ = p.shape
    tq = max(t for t in range(n_ctx, ATTN_Q_BLOCK + 1, n_ctx) if T % t == 0)
    assert n_ctx % ATTN_Q_SUB == 0
    H = DA_HEADS
    head = lambda part: (lambda b, h, i: (b, i, part * H + h))
    head_all = lambda part: (lambda b, h, i: (b, 0, part * H + h))
    return pl.pallas_call(
        functools.partial(_attn_kernel, tq=tq, n_ctx=n_ctx, out_scale=out_scale),
        grid=(B, H, T // tq),
        in_specs=[pl.BlockSpec(memory_space=pltpu.SMEM),
                  pl.BlockSpec((1, tq, HEAD_W), head(0)),
                  pl.BlockSpec((1, T, HEAD_W), head_all(1)),
                  pl.BlockSpec((1, T, HEAD_W), head_all(2)),
                  pl.BlockSpec((1, tq, HEAD_W), head(3)),
                  pl.BlockSpec((tq, HEAD_W), lambda b, h, i: (i, 0)),
                  pl.BlockSpec((tq, HEAD_W), lambda b, h, i: (i, 0)),
                  pl.BlockSpec((T, HEAD_W), lambda b, h, i: (0, 0)),
                  pl.BlockSpec((T, HEAD_W), lambda b, h, i: (0, 0)),
                  pl.BlockSpec((1, HEAD_W), lambda b, h, i: (0, 0))],
        out_specs=pl.BlockSpec((1, tq, HEAD_W), lambda b, h, i: (b, i, h)),
        out_shape=jax.ShapeDtypeStruct((B, T, H * HEAD_W), BF16),
        scratch_shapes=[pltpu.VMEM((T, HEAD_W), BF16), pltpu.VMEM((HEAD_W, T), BF16)],
        compiler_params=_cparams(("parallel", "parallel", "arbitrary")),
        name="diff_attention",
    )(lam, p, p, p, p, cos, sin, cos, sin, subln_g.reshape(1, HEAD_W))


def _rope_tables(n_ctx, seq):
    n_freq = DA_D // 4
    inv = ROPE_BASE ** (-jnp.arange(n_freq, dtype=F32) / n_freq)
    t = jnp.arange(seq)
    lane = jnp.arange(HEAD_W)
    pos = jnp.where((lane % DA_D < DA_D // 2)[None, :], (t // GRID_W)[:, None], (t % GRID_W)[:, None]).astype(F32)
    ang = pos * inv[lane % n_freq][None, :]
    sign = jnp.where(lane % 32 < 16, -1.0, 1.0).astype(F32)[None, :]
    cos = jnp.concatenate([jnp.ones((n_ctx, HEAD_W), F32), jnp.cos(ang)], axis=0)
    sin = jnp.concatenate([jnp.zeros((n_ctx, HEAD_W), F32), jnp.sin(ang) * sign], axis=0)
    return cos, sin


def _chunk_cumsum(g, reverse):
    n = g.shape[0]
    row = lax.broadcasted_iota(jnp.int32, g.shape, 0)
    b = g
    sh = 1
    while sh < n:
        if reverse:
            b = b + jnp.where(row < n - sh, pltpu.roll(b, n - sh, axis=0), 0.0)
        else:
            b = b + jnp.where(row >= sh, pltpu.roll(b, sh, axis=0), 0.0)
        sh *= 2
    return b


def _score_tiles(qq, b2, c2, reverse):
    C = qq.shape[0]
    half = SUBLANES
    sub = lax.broadcasted_iota(jnp.int32, (half, qq.shape[1]), 0)
    tiles = []
    for s in range(C):
        sb = s // half
        blocks = range(sb + 1) if reverse else range(sb, C // half)
        for tb in blocks:
            rows = slice(tb * half, (tb + 1) * half)
            w = qq[rows] * jnp.exp2(b2[rows] - c2[s:s + 1])
            if tb == sb:
                keep = (sub <= s - sb * half) if reverse else (sub >= s - sb * half)
                w = jnp.where(keep, w, 0.0)
            tiles.append((s, tb, w))
    return tiles


def _lane_sums(tiles, ones_ref):
    n = len(tiles)
    assert n % 2 == 0
    lhs = jnp.concatenate([jnp.concatenate([tiles[2 * i], tiles[2 * i + 1]], axis=1) for i in range(n // 2)], axis=0)
    r = jnp.dot(lhs.astype(BF16), ones_ref[...], preferred_element_type=F32)
    w = tiles[0].shape[1]
    out = []
    for i in range(n // 2):
        rows = slice(i * SUBLANES, (i + 1) * SUBLANES)
        out += [r[rows, :w], r[rows, w:]]
    return out


def _hgrn_kernel(q_ref, v_ref, gate_ref, zf_ref, zb_ref, lb_ref, g_ref, o_ref,
                 qt_scr, oi_scr, ox_scr, dl_scr, u_scr, st_scr, ones_scr, *, n_ctx):
    T = q_ref.shape[1]
    C = HG_CHUNK
    K = HG_K
    nc = T // C
    nc_ctx = n_ctx // C
    same_half = (lax.broadcasted_iota(jnp.int32, (2 * K, 2 * K), 0) < K) == (lax.broadcasted_iota(jnp.int32, (2 * K, 2 * K), 1) < K)
    ones_scr[...] = jnp.where(same_half, 1.0, 0.0).astype(BF16)

    def prep(c, carry):
        r0 = pl.multiple_of(c * C, C)
        hq = q_ref[0, pl.ds(r0, C), :].astype(F32)
        qq = hq * _sigmoid(hq)
        vb = v_ref[0, pl.ds(r0, C), :]
        v = vb.astype(F32)
        tiles, keys = [], []
        for d, z_ref in enumerate((zf_ref, zb_ref)):
            reverse = d == 1
            z = z_ref[0, pl.ds(r0, C), :]
            lb = lb_ref[0, d:d + 1, :]
            e = jnp.exp(-jnp.abs(z))
            inv = 1.0 / (1.0 + e)
            sig = jnp.where(z >= 0, inv, e * inv)
            nsig = jnp.where(z >= 0, e * inv, inv)
            lk2 = jnp.log2(1.0 - lb) + jnp.log2(nsig)
            b2 = _chunk_cumsum(jnp.log2(lb + (1.0 - lb) * sig), reverse)
            b2_last = b2[0:1] if reverse else b2[C - 1:C]
            qt_scr[d, pl.ds(r0, C), :] = (qq * jnp.exp2(b2)).astype(BF16)
            keys.append(jnp.exp2(lk2 + (b2_last - b2)).astype(BF16))
            dl_scr[d, pl.ds(c, 1), :] = jnp.exp2(b2_last)
            tiles.append(_score_tiles(qq, b2, b2 - lk2, reverse))
        sums = _lane_sums([w for per_dir in tiles for _, _, w in per_dir], ones_scr)
        k = 0
        for d in range(2):
            parts = [jnp.zeros((SUBLANES, v.shape[1]), F32) for _ in range(C // SUBLANES)]
            for s, tb, _ in tiles[d]:
                parts[tb] = parts[tb] + sums[k] * v[s:s + 1]
                k += 1
            oi_scr[d, pl.ds(r0, C), :] = jnp.concatenate(parts, axis=0)
        upd = lax.dot_general(vb, jnp.concatenate(keys, axis=1), (((0,), (0,)), ((), ())),
                              preferred_element_type=F32)
        u_scr[0, c] = upd[:, :K]
        u_scr[1, c] = upd[:, K:]
        return carry

    lax.fori_loop(0, nc, prep, 0, unroll=HG_PREP_UNROLL)

    st_scr[...] = jnp.zeros_like(st_scr)

    def scan(m, carry):
        i0 = m * HG_GROUP
        cb0 = jnp.where(i0 < nc_ctx, nc_ctx - 1 - i0, nc + nc_ctx - 1 - i0)
        for j in range(HG_GROUP):
            for d, c in enumerate((i0 + j, cb0 - j)):
                r0 = pl.multiple_of(c * C, C)
                st = st_scr[d]
                ox_scr[d, pl.ds(r0, C), :] = jnp.dot(qt_scr[d, pl.ds(r0, C), :], st.T.astype(BF16),
                                                     preferred_element_type=F32)
                st_scr[d] = dl_scr[d, pl.ds(c, 1), :] * st + u_scr[d, c]
        return carry

    assert nc % HG_GROUP == 0 and nc_ctx % HG_GROUP == 0
    lax.fori_loop(0, nc // HG_GROUP, scan, 0)

    o = (oi_scr[0] + ox_scr[0]) + (oi_scr[1] + ox_scr[1])
    y = o * lax.rsqrt(jnp.mean(o * o, axis=-1, keepdims=True) + EPS) * g_ref[...]
    gate = gate_ref[0].astype(F32)
    o_ref[0] = (y * (gate * _sigmoid(gate))).astype(o_ref.dtype)


def _hgrn(p, pf, lb, norm_g, n_ctx, part0):
    B, T, _ = p.shape
    H = HG_HEADS
    nc = T // HG_CHUNK
    part = lambda k: (lambda b, h: (b, 0, k * H + h))
    blk = (1, T, HEAD_W)
    return pl.pallas_call(
        functools.partial(_hgrn_kernel, n_ctx=n_ctx),
        grid=(B, H),
        in_specs=[pl.BlockSpec(blk, part(part0)), pl.BlockSpec(blk, part(part0 + 1)),
                  pl.BlockSpec(blk, part(part0 + 2)),
                  pl.BlockSpec(blk, part(0)), pl.BlockSpec(blk, part(1)),
                  pl.BlockSpec((1, 2, HG_K), lambda b, h: (h, 0, 0)),
                  pl.BlockSpec((1, HEAD_W), lambda b, h: (0, 0))],
        out_specs=pl.BlockSpec(blk, lambda b, h: (b, 0, h)),
        out_shape=jax.ShapeDtypeStruct((B, T, H * HEAD_W), BF16),
        scratch_shapes=[pltpu.VMEM((2, T, HG_K), BF16),
                        pltpu.VMEM((2, T, HEAD_W), F32), pltpu.VMEM((2, T, HEAD_W), F32),
                        pltpu.VMEM((2, nc, HG_K), F32),
                        pltpu.VMEM((2, nc, HEAD_W, HG_K), F32),
                        pltpu.VMEM((2, HEAD_W, HG_K), F32),
                        pltpu.VMEM((2 * HG_K, 2 * HG_K), BF16)],
        compiler_params=_cparams(("parallel", "parallel")),
        name="hgrn2",
    )(p, p, p, pf, pf, lb, norm_g.reshape(1, HEAD_W))


def _shift_rows(x, d):
    n = x.shape[0]
    row = lax.broadcasted_iota(jnp.int32, (n, 1), 0)
    rolled = pltpu.roll(x, (-d) % n, axis=0)
    keep = (row + d >= 0) & (row + d < n)
    return jnp.where(keep, rolled, 0.0)


def _pool_minus_identity(x, w):
    n = x.shape[0]
    ahead = w - w // 2
    behind = w // 2
    fwd = x
    span = 1
    while span < ahead:
        fwd = fwd + _shift_rows(fwd, span)
        span *= 2
    bwd = x
    span = 1
    while span < behind:
        bwd = bwd + _shift_rows(bwd, -span)
        span *= 2
    total = fwd + _shift_rows(bwd, -1)
    row = lax.broadcasted_iota(jnp.int32, (n, 1), 0)
    cnt = jnp.minimum(row + ahead, n) - jnp.maximum(row - behind, 0)
    return total / cnt.astype(F32) - x


def _pool_kernel(u_ref, z_ref, w_ref, ls_ref, o_ref, r_scr, *, n_ctx):
    j = pl.program_id(1)
    T = u_ref.shape[1]
    for jj, win in enumerate(POOL_WINDOWS):
        @pl.when(j == jj)
        def _(win=win):
            for lo, hi in ((0, n_ctx), (n_ctx, T)):
                r_scr[lo:hi, :] = _pool_minus_identity(u_ref[0, lo:hi, :].astype(F32), win).astype(BF16)

    y = jnp.dot(r_scr[...], w_ref[0], preferred_element_type=F32)
    z = z_ref[0].astype(F32)
    o_ref[0] = (y * ls_ref[...] * (z * _sigmoid(z))).astype(o_ref.dtype)


def _pool_mixer(p, w_pool, ls, n_ctx):
    B, T, two_e = p.shape
    G, gw, _ = w_pool.shape
    assert G == len(POOL_WINDOWS) and two_e == 2 * G * gw
    return pl.pallas_call(
        functools.partial(_pool_kernel, n_ctx=n_ctx),
        grid=(B, G),
        in_specs=[pl.BlockSpec((1, T, gw), lambda b, j: (b, 0, j)),
                  pl.BlockSpec((1, T, gw), lambda b, j: (b, 0, G + j)),
                  pl.BlockSpec((1, gw, gw), lambda b, j: (j, 0, 0)),
                  pl.BlockSpec((1, gw), lambda b, j: (0, j))],
        out_specs=pl.BlockSpec((1, T, gw), lambda b, j: (b, 0, j)),
        out_shape=jax.ShapeDtypeStruct((B, T, G * gw), BF16),
        scratch_shapes=[pltpu.VMEM((T, gw), BF16)],
        compiler_params=_cparams(("parallel", "parallel")),
        name="pool_mixer",
    )(p, p, w_pool, ls.reshape(1, G * gw))


def _out_proj_kernel(a_ref, b_ref, w_ref, x_ref, mod_ref, g_ref, o_ref, *, tm, n_ctx, row_off):
    ka = a_ref.shape[2]
    sub = OUT_SUB if tm % OUT_SUB == 0 else tm

    def project(r0):
        y = jnp.dot(a_ref[0, r0:r0 + sub, :], w_ref[:ka, :], preferred_element_type=F32)
        return y + jnp.dot(b_ref[0, r0:r0 + sub, :], w_ref[ka:, :], preferred_element_type=F32)

    y_next = project(0)
    for r0 in range(0, tm, sub):
        y = y_next
        if r0 + sub < tm:
            y_next = project(r0 + sub)
        yn = y * lax.rsqrt(jnp.mean(y * y, axis=-1, keepdims=True) + EPS) * g_ref[...]
        gt = _row_mod(mod_ref, 2, 5, row_off + pl.program_id(1) * tm + r0, sub, n_ctx)
        o_ref[0, r0:r0 + sub, :] = x_ref[0, r0:r0 + sub, :] + gt * yn


def _out_proj(a, b, a_blk, b_blk, w, x, modrows, g, n_ctx, tm, latents_only):
    B, T, D = x.shape
    half_k = w.shape[0] // 2
    off = n_ctx // tm if latents_only else 0
    rows_out = T - n_ctx if latents_only else T
    return pl.pallas_call(
        functools.partial(_out_proj_kernel, tm=tm, n_ctx=n_ctx, row_off=off * tm),
        grid=(B, rows_out // tm),
        in_specs=[pl.BlockSpec((1, tm, half_k), lambda bb, r: (bb, r + off, a_blk)),
                  pl.BlockSpec((1, tm, half_k), lambda bb, r: (bb, r + off, b_blk)),
                  pl.BlockSpec(w.shape, lambda bb, r: (0, 0), pipeline_mode=pl.Buffered(1)),
                  pl.BlockSpec((1, tm, D), lambda bb, r: (bb, r + off, 0)),
                  pl.BlockSpec((1, SUBLANES, D), lambda bb, r: (bb, 0, 0)),
                  pl.BlockSpec((1, D), lambda bb, r: (0, 0))],
        out_specs=pl.BlockSpec((1, tm, D), lambda bb, r: (bb, r, 0)),
        out_shape=jax.ShapeDtypeStruct((B, rows_out, D), F32),
        compiler_params=_cparams(("parallel", "parallel")),
        name="out_proj",
    )(a, b, w, x, modrows, g.reshape(1, D))


def _row_tile(T, cap):
    best = NORM_SLAB
    for t in range(NORM_SLAB, min(T, cap) + 1, NORM_SLAB):
        if T % t == 0:
            best = t
    return best


def _col_tile(N, cap):
    best = 128
    for t in range(128, min(N, cap) + 1, 128):
        if N % t == 0:
            best = t
    return best


def kernel(x, c, ctx, c_ctx, w_mod, b_mod, g_pre, g_post, ev_w_in, ev_w_out, ev_lambda, ev_subln_g,
           ev_hg_lb_logits, ev_hg_norm_g, od_w_in, od_w_pool, od_scale, od_w_out):
    B, S, D = x.shape
    n_ctx = ctx.shape[1]
    depth = w_mod.shape[0]
    T = n_ctx + S
    W = DA_HEADS * HEAD_W

    xc = jnp.concatenate([ctx, x], axis=1)

    n_rows = -(-(B + 1) // SUBLANES) * SUBLANES
    cvec = jnp.zeros((n_rows, D), F32).at[:B].set(c).at[B].set(c_ctx)
    mod = _modulation(cvec, w_mod, b_mod)

    lb_cum = jnp.cumsum(jax.nn.softmax(ev_hg_lb_logits.astype(F32), axis=0), axis=0)
    lb_all = lb_cum - lb_cum[0]
    cos, sin = _rope_tables(n_ctx, S)

    tm_in = _row_tile(T, 1152)
    tm_out = _row_tile(T, 768)

    for l in range(depth):
        last = l == depth - 1
        ml = mod[l]
        lat = ml[:B].reshape(B, 3, D)
        cx = jnp.broadcast_to(ml[B].reshape(1, 3, D), (B, 3, D))
        modrows = jnp.concatenate([lat, cx, jnp.zeros((B, SUBLANES - 6, D), F32)], axis=1)

        if l % 2 == 0:
            e = l // 2
            w_in = ev_w_in[e]
            w_perm = jnp.concatenate([w_in[:, :5 * W], w_in[:, 7 * W:], w_in[:, 5 * W:7 * W]], axis=1).astype(BF16)
            p, pf = _norm_proj(xc, modrows, g_pre[l], w_perm, 2 * W, n_ctx, tm_in, _col_tile(W, 1024))

            lam_init = 0.8 - 0.6 * math.exp(-0.3 * l)
            lv = ev_lambda[e].astype(F32)
            lam = jnp.exp(jnp.sum(lv[0] * lv[1])) - jnp.exp(jnp.sum(lv[2] * lv[3])) + lam_init
            a = _diff_attention(p, lam.reshape(1), cos, sin, ev_subln_g[e], 1.0 - lam_init, n_ctx)
            lb = lb_all[e].reshape(2, HG_HEADS, HG_K).transpose(1, 0, 2)
            bh = _hgrn(p, pf, lb, ev_hg_norm_g[e], n_ctx, 4)
            w_out = ev_w_out[e].astype(BF16)
            ya, yb, ia, ib = a, bh, 0, 0
        else:
            o = l // 2
            p = _norm_proj(xc, modrows, g_pre[l], od_w_in[o].astype(BF16), 0, n_ctx, tm_in,
                           _col_tile(od_w_in.shape[2], 1024))
            y = _pool_mixer(p, od_w_pool[o].astype(BF16), od_scale[o], n_ctx)
            w_out = od_w_out[o].astype(BF16)
            ya, yb, ia, ib = y, y, 0, 1

        xc = _out_proj(ya, yb, ia, ib, w_out, xc, modrows, g_post[l], n_ctx,
                       n_ctx if last else tm_out, latents_only=last)
    return xc
```

```python
import functools
import math

import jax
import jax.numpy as jnp
from jax import lax
from jax.experimental import pallas as pl
from jax.experimental.pallas import tpu as pltpu

DA_HEADS = 8
DA_D = 64
HEAD_W = 2 * DA_D
HG_HEADS = 8
HG_K = 128
GRID_W = 64
ROPE_BASE = 10000.0
EPS = 1e-6
POOL_WINDOWS = (2, 4, 8, 16)
HG_CHUNK = 32
LOG2E = math.log2(math.e)
ATTN_Q_BLOCK = 2304
ATTN_Q_SUB = 128
ATTN_AHEAD = 5
HG_GROUP = 8
HG_PREP_UNROLL = 8
SUBLANES = 8
OUT_SUB = 256
NORM_SLAB = 16
NORM_UNROLL = 4

F32 = jnp.float32
BF16 = jnp.bfloat16
VMEM_LIMIT = 56 * 1024 * 1024


def _cparams(sem):
    return pltpu.CompilerParams(dimension_semantics=sem, vmem_limit_bytes=VMEM_LIMIT)


def _sigmoid(x):
    return 1.0 / (1.0 + jnp.exp(-x))


def _mod_kernel(c_ref, w_ref, b_ref, o_ref):
    c = c_ref[...]
    s = c * _sigmoid(c)
    o_ref[0] = jnp.dot(s, w_ref[0], preferred_element_type=F32,
                       precision=lax.Precision.HIGHEST) + b_ref[0]


def _modulation(cvec, w_mod, b_mod):
    L, D, N = w_mod.shape
    R = cvec.shape[0]
    tn = 1024 if N % 1024 == 0 else N
    return pl.pallas_call(
        _mod_kernel,
        grid=(L, N // tn),
        in_specs=[pl.BlockSpec((R, D), lambda l, j: (0, 0)),
                  pl.BlockSpec((1, D, tn), lambda l, j: (l, 0, j)),
                  pl.BlockSpec((1, 1, tn), lambda l, j: (l, 0, j))],
        out_specs=pl.BlockSpec((1, R, tn), lambda l, j: (l, 0, j)),
        out_shape=jax.ShapeDtypeStruct((L, R, N), F32),
        compiler_params=_cparams(("parallel", "parallel")),
        name="modulation",
    )(cvec, w_mod, b_mod.reshape(L, 1, N))


def _row_mod(mod_ref, lat_row, ctx_row, row0, tm, n_ctx):
    rows = row0 + lax.broadcasted_iota(jnp.int32, (tm, 1), 0)
    return jnp.where(rows < n_ctx, mod_ref[0, ctx_row:ctx_row + 1, :], mod_ref[0, lat_row:lat_row + 1, :])


def _norm_proj_kernel(x_ref, mod_ref, g_ref, w_ref, o_ref, *rest, tm, n_ctx, n_lo_tiles):
    h_scr, gs_scr = rest[-2:]
    r = pl.program_id(1)
    j = pl.program_id(2)

    @pl.when(j == 0)
    def _():
        gs_scr[0:1, :] = g_ref[...] * (1.0 + mod_ref[0, 1:2, :])
        gs_scr[1:2, :] = mod_ref[0, 0:1, :]
        gs_scr[2:3, :] = g_ref[...] * (1.0 + mod_ref[0, 4:5, :])
        gs_scr[3:4, :] = mod_ref[0, 3:4, :]

        def slab(k, carry):
            r0 = pl.multiple_of(k * NORM_SLAB, NORM_SLAB)
            base = jnp.where(r * tm + r0 < n_ctx, 2, 0)
            x = x_ref[0, pl.ds(r0, NORM_SLAB), :]
            y = x * lax.rsqrt(jnp.mean(x * x, axis=-1, keepdims=True) + EPS)
            h_scr[pl.ds(r0, NORM_SLAB), :] = (y * gs_scr[pl.ds(base, 1), :] + gs_scr[pl.ds(base + 1, 1), :]).astype(BF16)
            return carry

        lax.fori_loop(0, tm // NORM_SLAB, slab, 0, unroll=NORM_UNROLL)

    if len(rest) == 2:
        o_ref[0] = jnp.dot(h_scr[...], w_ref[...], preferred_element_type=F32).astype(o_ref.dtype)
    else:
        @pl.when(j < n_lo_tiles)
        def _():
            o_ref[0] = jnp.dot(h_scr[...], w_ref[...], preferred_element_type=F32).astype(o_ref.dtype)

        @pl.when(j >= n_lo_tiles)
        def _():
            rest[0][0] = jnp.dot(h_scr[...], w_ref[...], preferred_element_type=F32)


def _norm_proj(x, modrows, g, w, n_f32, n_ctx, tm, tn):
    B, T, D = x.shape
    n_lo = w.shape[1] - n_f32
    assert n_lo % tn == 0 and n_f32 % tn == 0
    lo_tiles, hi_tiles = n_lo // tn, n_f32 // tn
    out_specs = [pl.BlockSpec((1, tm, tn), lambda b, r, j: (b, r, jnp.minimum(j, lo_tiles - 1)))]
    out_shape = [jax.ShapeDtypeStruct((B, T, n_lo), BF16)]
    if hi_tiles:
        out_specs.append(pl.BlockSpec((1, tm, tn), lambda b, r, j: (b, r, jnp.maximum(j - lo_tiles, 0))))
        out_shape.append(jax.ShapeDtypeStruct((B, T, n_f32), F32))
    outs = pl.pallas_call(
        functools.partial(_norm_proj_kernel, tm=tm, n_ctx=n_ctx, n_lo_tiles=lo_tiles),
        grid=(B, T // tm, lo_tiles + hi_tiles),
        in_specs=[pl.BlockSpec((1, tm, D), lambda b, r, j: (b, r, 0)),
                  pl.BlockSpec((1, SUBLANES, D), lambda b, r, j: (b, 0, 0)),
                  pl.BlockSpec((1, D), lambda b, r, j: (0, 0)),
                  pl.BlockSpec((D, tn), lambda b, r, j: (0, j))],
        out_specs=out_specs,
        out_shape=out_shape,
        scratch_shapes=[pltpu.VMEM((tm, D), BF16), pltpu.VMEM((4, D), F32)],
        compiler_params=_cparams(("parallel", "parallel", "arbitrary")),
        name="norm_proj",
    )(x, modrows, g.reshape(1, D), w)
    return outs if hi_tiles else outs[0]


def _rope(x, cos, sin_signed):
    lane = lax.broadcasted_iota(jnp.int32, x.shape, 1)
    n = x.shape[1]
    partner = jnp.where(lane % 32 < 16, pltpu.roll(x, n - 16, axis=1), pltpu.roll(x, 16, axis=1))
    return x * cos + partner * sin_signed


def _attn_kernel(lam_ref, q_ref, k_ref, v_ref, gate_ref, cq_ref, sq_ref, ck_ref, sk_ref, g_ref,
                 o_ref, k_scr, vt_scr, *, tq, n_ctx, out_scale):
    i = pl.program_id(2)
    n_keys = k_scr.shape[0]

    @pl.when(i == 0)
    def _():
        k_scr[...] = _rope(k_ref[0].astype(F32), ck_ref[...], sk_ref[...]).astype(BF16)
        vt_scr[...] = v_ref[0].astype(F32).T.astype(BF16)

    qr = _rope(q_ref[0].astype(F32), cq_ref[...], sq_ref[...]) * (DA_D ** -0.5 * LOG2E)
    first_map = lax.broadcasted_iota(jnp.int32, qr.shape, 1) < DA_D
    q_maps = (jnp.where(first_map, qr, 0.0).astype(BF16), jnp.where(first_map, 0.0, qr).astype(BF16))
    lam = lam_ref[0]
    qs = ATTN_Q_SUB

    def attend(n_ctx_rows):
        def scores(r0):
            nk = n_ctx if r0 < n_ctx_rows else n_keys
            q2 = jnp.concatenate([q_maps[0][r0:r0 + qs], q_maps[1][r0:r0 + qs]], axis=0)
            return lax.dot_general(k_scr[:nk, :], q2, (((1,), (1,)), ((), ())),
                                   preferred_element_type=F32)

        starts = list(range(0, tq, qs))
        pending = [scores(r0) for r0 in starts[:ATTN_AHEAD]]
        for n, r0 in enumerate(starts):
            rows = slice(r0, r0 + qs)
            s = pending.pop(0)
            nk = s.shape[0]
            if n + ATTN_AHEAD < len(starts):
                pending.append(scores(starts[n + ATTN_AHEAD]))
            e = jnp.exp2(s - jnp.max(s, axis=0, keepdims=True))
            rinv = 1.0 / jnp.sum(e, axis=0, keepdims=True)
            ov = jnp.dot(vt_scr[:, :nk], e.astype(BF16), preferred_element_type=F32) * rinv
            o = (ov[:, :qs] - lam * ov[:, qs:]).T
            y = o * lax.rsqrt(jnp.mean(o * o, axis=-1, keepdims=True) + EPS) * g_ref[...] * out_scale
            gate = gate_ref[0, rows, :].astype(F32)
            o_ref[0, rows, :] = (y * (gate * _sigmoid(gate))).astype(o_ref.dtype)

    @pl.when(i == 0)
    def _():
        attend(n_ctx)

    @pl.when(i != 0)
    def _():
        attend(0)


def _diff_attention(p, lam, cos, sin, subln_g, out_scale, n_ctx):
    B, T, _ = p.shape
    tq = max(t for t in range(n_ctx, ATTN_Q_BLOCK + 1, n_ctx) if T % t == 0)
    assert n_ctx % ATTN_Q_SUB == 0
    H = DA_HEADS
    head = lambda part: (lambda b, h, i: (b, i, part * H + h))
    head_all = lambda part: (lambda b, h, i: (b, 0, part * H + h))
    return pl.pallas_call(
        functools.partial(_attn_kernel, tq=tq, n_ctx=n_ctx, out_scale=out_scale),
        grid=(B, H, T // tq),
        in_specs=[pl.BlockSpec(memory_space=pltpu.SMEM),
                  pl.BlockSpec((1, tq, HEAD_W), head(0)),
                  pl.BlockSpec((1, T, HEAD_W), head_all(1)),
                  pl.BlockSpec((1, T, HEAD_W), head_all(2)),
                  pl.BlockSpec((1, tq, HEAD_W), head(3)),
                  pl.BlockSpec((tq, HEAD_W), lambda b, h, i: (i, 0)),
                  pl.BlockSpec((tq, HEAD_W), lambda b, h, i: (i, 0)),
                  pl.BlockSpec((T, HEAD_W), lambda b, h, i: (0, 0)),
                  pl.BlockSpec((T, HEAD_W), lambda b, h, i: (0, 0)),
                  pl.BlockSpec((1, HEAD_W), lambda b, h, i: (0, 0))],
        out_specs=pl.BlockSpec((1, tq, HEAD_W), lambda b, h, i: (b, i, h)),
        out_shape=jax.ShapeDtypeStruct((B, T, H * HEAD_W), BF16),
        scratch_shapes=[pltpu.VMEM((T, HEAD_W), BF16), pltpu.VMEM((HEAD_W, T), BF16)],
        compiler_params=_cparams(("parallel", "parallel", "arbitrary")),
        name="diff_attention",
    )(lam, p, p, p, p, cos, sin, cos, sin, subln_g.reshape(1, HEAD_W))


def _rope_tables(n_ctx, seq):
    n_freq = DA_D // 4
    inv = ROPE_BASE ** (-jnp.arange(n_freq, dtype=F32) / n_freq)
    t = jnp.arange(seq)
    lane = jnp.arange(HEAD_W)
    pos = jnp.where((lane % DA_D < DA_D // 2)[None, :], (t // GRID_W)[:, None], (t % GRID_W)[:, None]).astype(F32)
    ang = pos * inv[lane % n_freq][None, :]
    sign = jnp.where(lane % 32 < 16, -1.0, 1.0).astype(F32)[None, :]
    cos = jnp.concatenate([jnp.ones((n_ctx, HEAD_W), F32), jnp.cos(ang)], axis=0)
    sin = jnp.concatenate([jnp.zeros((n_ctx, HEAD_W), F32), jnp.sin(ang) * sign], axis=0)
    return cos, sin


def _chunk_cumsum(g, reverse):
    n = g.shape[0]
    row = lax.broadcasted_iota(jnp.int32, g.shape, 0)
    b = g
    sh = 1
    while sh < n:
        if reverse:
            b = b + jnp.where(row < n - sh, pltpu.roll(b, n - sh, axis=0), 0.0)
        else:
            b = b + jnp.where(row >= sh, pltpu.roll(b, sh, axis=0), 0.0)
        sh *= 2
    return b


def _diag_tiles(qq, b2, c2, reverse):
    half = SUBLANES
    sub = lax.broadcasted_iota(jnp.int32, (half, qq.shape[1]), 0)
    tiles = []
    for s in range(qq.shape[0]):
        sb = s // half
        rows = slice(sb * half, (sb + 1) * half)
        w = qq[rows] * jnp.exp2(b2[rows] - c2[s:s + 1])
        keep = (sub <= s - sb * half) if reverse else (sub >= s - sb * half)
        tiles.append((s, sb, jnp.where(keep, w, 0.0)))
    return tiles


def _off_diag_pieces(qq, b2, c2, reverse):
    C = qq.shape[0]
    pieces = []
    span = SUBLANES
    while span < C:
        for u in range(0, C, 2 * span):
            lo, hi = slice(u, u + span), slice(u + span, u + 2 * span)
            if reverse:
                q_rows, k_rows, ref = lo, hi, b2[u + span:u + span + 1]
            else:
                q_rows, k_rows, ref = hi, lo, b2[u + span - 1:u + span]
            pieces.append((q_rows, k_rows, qq[q_rows] * jnp.exp2(b2[q_rows] - ref), jnp.exp2(ref - c2[k_rows])))
        span *= 2
    return pieces


def _lane_sums(tiles, ones_ref):
    n = len(tiles)
    assert n % 2 == 0
    lhs = jnp.concatenate([jnp.concatenate([tiles[2 * i], tiles[2 * i + 1]], axis=1) for i in range(n // 2)], axis=0)
    r = jnp.dot(lhs.astype(BF16), ones_ref[...], preferred_element_type=F32)
    w = tiles[0].shape[1]
    out = []
    for i in range(n // 2):
        rows = slice(i * SUBLANES, (i + 1) * SUBLANES)
        out += [r[rows, :w], r[rows, w:]]
    return out


def _hgrn_kernel(q_ref, v_ref, gate_ref, zf_ref, zb_ref, lb_ref, g_ref, o_ref,
                 qt_scr, oi_scr, ox_scr, dl_scr, u_scr, st_scr, ones_scr, blk_scr, *, n_ctx):
    T = q_ref.shape[1]
    C = HG_CHUNK
    K = HG_K
    nc = T // C
    nc_ctx = n_ctx // C
    same_half = (lax.broadcasted_iota(jnp.int32, (2 * K, 2 * K), 0) < K) == (lax.broadcasted_iota(jnp.int32, (2 * K, 2 * K), 1) < K)
    ones_scr[...] = jnp.where(same_half, 1.0, 0.0).astype(BF16)
    sizes = []
    span = SUBLANES
    while span < C:
        sizes += [span] * (C // (2 * span))
        span *= 2
    bounds = [sum((sizes * 2)[:i]) for i in range(1, 2 * len(sizes))]
    n_off = blk_scr.shape[0]
    row_i = lax.broadcasted_iota(jnp.int32, (n_off, n_off), 0)
    col_i = lax.broadcasted_iota(jnp.int32, (n_off, n_off), 1)
    piece_of_row = sum((row_i >= b).astype(jnp.int32) for b in bounds)
    piece_of_col = sum((col_i >= b).astype(jnp.int32) for b in bounds)
    blk_scr[...] = jnp.where(piece_of_row == piece_of_col, 1.0, 0.0)

    def stage1(c):
        r0 = pl.multiple_of(c * C, C)
        hq = q_ref[0, pl.ds(r0, C), :].astype(F32)
        qq = hq * _sigmoid(hq)
        vb = v_ref[0, pl.ds(r0, C), :]
        v = vb.astype(F32)
        tiles, keys, pieces = [], [], []
        for d, z_ref in enumerate((zf_ref, zb_ref)):
            reverse = d == 1
            z = z_ref[0, pl.ds(r0, C), :]
            lb = lb_ref[0, d:d + 1, :]
            e = jnp.exp(-jnp.abs(z))
            inv = 1.0 / (1.0 + e)
            sig = jnp.where(z >= 0, inv, e * inv)
            nsig = jnp.where(z >= 0, e * inv, inv)
            lk2 = jnp.log2(1.0 - lb) + jnp.log2(nsig)
            b2 = _chunk_cumsum(jnp.log2(lb + (1.0 - lb) * sig), reverse)
            b2_last = b2[0:1] if reverse else b2[C - 1:C]
            qt_scr[d, pl.ds(r0, C), :] = (qq * jnp.exp2(b2)).astype(BF16)
            keys.append(jnp.exp2(lk2 + (b2_last - b2)).astype(BF16))
            dl_scr[d, pl.ds(c, 1), :] = jnp.exp2(b2_last)
            tiles.append(_diag_tiles(qq, b2, b2 - lk2, reverse))
            pieces.append(_off_diag_pieces(qq, b2, b2 - lk2, reverse))
        sums = _lane_sums([w for per_dir in tiles for _, _, w in per_dir], ones_scr)
        q_off = jnp.concatenate([p[2] for per_dir in pieces for p in per_dir], axis=0).astype(BF16)
        k_off = jnp.concatenate([p[3] for per_dir in pieces for p in per_dir], axis=0).astype(BF16)
        cross = lax.dot_general(q_off, k_off, (((1,), (1,)), ((), ())), preferred_element_type=F32)
        upd = lax.dot_general(vb, jnp.concatenate(keys, axis=1), (((0,), (0,)), ((), ())),
                              preferred_element_type=F32)
        u_scr[0, c] = upd[:, :K]
        u_scr[1, c] = upd[:, K:]
        return r0, v, tiles, sums, pieces, cross

    def stage2(r0, v, tiles, sums, pieces, cross):
        v_off = jnp.concatenate([v[p[1]] for per_dir in pieces for p in per_dir], axis=0).astype(BF16)
        contrib = jnp.dot((cross * blk_scr[...]).astype(BF16), v_off, preferred_element_type=F32)
        k = 0
        off = 0
        for d in range(2):
            parts = [jnp.zeros((SUBLANES, v.shape[1]), F32) for _ in range(C // SUBLANES)]
            for s, blk, _ in tiles[d]:
                parts[blk] = parts[blk] + sums[k] * v[s:s + 1]
                k += 1
            for q_rows, _, qp, _ in pieces[d]:
                for i in range(qp.shape[0] // SUBLANES):
                    blk = q_rows.start // SUBLANES + i
                    parts[blk] = parts[blk] + contrib[off:off + SUBLANES]
                    off += SUBLANES
            oi_scr[d, pl.ds(r0, C), :] = jnp.concatenate(parts, axis=0)

    def prep(t, carry):
        staged = [stage1(t * HG_PREP_UNROLL + j) for j in range(HG_PREP_UNROLL)]
        for args in staged:
            stage2(*args)
        return carry

    assert nc % HG_PREP_UNROLL == 0
    lax.fori_loop(0, nc // HG_PREP_UNROLL, prep, 0)

    st_scr[...] = jnp.zeros_like(st_scr)

    def scan(m, carry):
        i0 = m * HG_GROUP
        cb0 = jnp.where(i0 < nc_ctx, nc_ctx - 1 - i0, nc + nc_ctx - 1 - i0)
        for j in range(HG_GROUP):
            for d, c in enumerate((i0 + j, cb0 - j)):
                r0 = pl.multiple_of(c * C, C)
                st = st_scr[d]
                ox_scr[d, pl.ds(r0, C), :] = jnp.dot(qt_scr[d, pl.ds(r0, C), :], st.T.astype(BF16),
                                                     preferred_element_type=F32)
                st_scr[d] = dl_scr[d, pl.ds(c, 1), :] * st + u_scr[d, c]
        return carry

    assert nc % HG_GROUP == 0 and nc_ctx % HG_GROUP == 0
    lax.fori_loop(0, nc // HG_GROUP, scan, 0)

    o = (oi_scr[0] + ox_scr[0]) + (oi_scr[1] + ox_scr[1])
    y = o * lax.rsqrt(jnp.mean(o * o, axis=-1, keepdims=True) + EPS) * g_ref[...]
    gate = gate_ref[0].astype(F32)
    o_ref[0] = (y * (gate * _sigmoid(gate))).astype(o_ref.dtype)


def _hgrn(p, pf, lb, norm_g, n_ctx, part0):
    B, T, _ = p.shape
    H = HG_HEADS
    nc = T // HG_CHUNK
    part = lambda k: (lambda b, h: (b, 0, k * H + h))
    blk = (1, T, HEAD_W)
    return pl.pallas_call(
        functools.partial(_hgrn_kernel, n_ctx=n_ctx),
        grid=(B, H),
        in_specs=[pl.BlockSpec(blk, part(part0)), pl.BlockSpec(blk, part(part0 + 1)),
                  pl.BlockSpec(blk, part(part0 + 2)),
                  pl.BlockSpec(blk, part(0)), pl.BlockSpec(blk, part(1)),
                  pl.BlockSpec((1, 2, HG_K), lambda b, h: (h, 0, 0)),
                  pl.BlockSpec((1, HEAD_W), lambda b, h: (0, 0))],
        out_specs=pl.BlockSpec(blk, lambda b, h: (b, 0, h)),
        out_shape=jax.ShapeDtypeStruct((B, T, H * HEAD_W), BF16),
        scratch_shapes=[pltpu.VMEM((2, T, HG_K), BF16),
                        pltpu.VMEM((2, T, HEAD_W), F32), pltpu.VMEM((2, T, HEAD_W), F32),
                        pltpu.VMEM((2, nc, HG_K), F32),
                        pltpu.VMEM((2, nc, HEAD_W, HG_K), F32),
                        pltpu.VMEM((2, HEAD_W, HG_K), F32),
                        pltpu.VMEM((2 * HG_K, 2 * HG_K), BF16),
                        pltpu.VMEM((2 * HG_CHUNK, 2 * HG_CHUNK), F32)],
        compiler_params=_cparams(("parallel", "parallel")),
        name="hgrn2",
    )(p, p, p, pf, pf, lb, norm_g.reshape(1, HEAD_W))


def _shift_rows(x, d):
    n = x.shape[0]
    row = lax.broadcasted_iota(jnp.int32, (n, 1), 0)
    rolled = pltpu.roll(x, (-d) % n, axis=0)
    keep = (row + d >= 0) & (row + d < n)
    return jnp.where(keep, rolled, 0.0)


def _pool_minus_identity(x, w):
    n = x.shape[0]
    ahead = w - w // 2
    behind = w // 2
    fwd = x
    span = 1
    while span < ahead:
        fwd = fwd + _shift_rows(fwd, span)
        span *= 2
    bwd = x
    span = 1
    while span < behind:
        bwd = bwd + _shift_rows(bwd, -span)
        span *= 2
    total = fwd + _shift_rows(bwd, -1)
    row = lax.broadcasted_iota(jnp.int32, (n, 1), 0)
    cnt = jnp.minimum(row + ahead, n) - jnp.maximum(row - behind, 0)
    return total / cnt.astype(F32) - x


def _pool_kernel(u_ref, z_ref, w_ref, ls_ref, o_ref, r_scr, *, n_ctx):
    j = pl.program_id(1)
    T = u_ref.shape[1]
    for jj, win in enumerate(POOL_WINDOWS):
        @pl.when(j == jj)
        def _(win=win):
            for lo, hi in ((0, n_ctx), (n_ctx, T)):
                r_scr[lo:hi, :] = _pool_minus_identity(u_ref[0, lo:hi, :].astype(F32), win).astype(BF16)

    y = jnp.dot(r_scr[...], w_ref[0], preferred_element_type=F32)
    z = z_ref[0].astype(F32)
    o_ref[0] = (y * ls_ref[...] * (z * _sigmoid(z))).astype(o_ref.dtype)


def _pool_mixer(p, w_pool, ls, n_ctx):
    B, T, two_e = p.shape
    G, gw, _ = w_pool.shape
    assert G == len(POOL_WINDOWS) and two_e == 2 * G * gw
    return pl.pallas_call(
        functools.partial(_pool_kernel, n_ctx=n_ctx),
        grid=(B, G),
        in_specs=[pl.BlockSpec((1, T, gw), lambda b, j: (b, 0, j)),
                  pl.BlockSpec((1, T, gw), lambda b, j: (b, 0, G + j)),
                  pl.BlockSpec((1, gw, gw), lambda b, j: (j, 0, 0)),
                  pl.BlockSpec((1, gw), lambda b, j: (0, j))],
        out_specs=pl.BlockSpec((1, T, gw), lambda b, j: (b, 0, j)),
        out_shape=jax.ShapeDtypeStruct((B, T, G * gw), BF16),
        scratch_shapes=[pltpu.VMEM((T, gw), BF16)],
        compiler_params=_cparams(("parallel", "parallel")),
        name="pool_mixer",
    )(p, p, w_pool, ls.reshape(1, G * gw))


def _out_proj_kernel(a_ref, b_ref, w_ref, x_ref, mod_ref, g_ref, o_ref, *, tm, n_ctx, row_off):
    ka = a_ref.shape[2]
    sub = OUT_SUB if tm % OUT_SUB == 0 else tm

    def project(r0):
        y = jnp.dot(a_ref[0, r0:r0 + sub, :], w_ref[:ka, :], preferred_element_type=F32)
        return y + jnp.dot(b_ref[0, r0:r0 + sub, :], w_ref[ka:, :], preferred_element_type=F32)

    y_next = project(0)
    for r0 in range(0, tm, sub):
        y = y_next
        if r0 + sub < tm:
            y_next = project(r0 + sub)
        yn = y * lax.rsqrt(jnp.mean(y * y, axis=-1, keepdims=True) + EPS) * g_ref[...]
        gt = _row_mod(mod_ref, 2, 5, row_off + pl.program_id(1) * tm + r0, sub, n_ctx)
        o_ref[0, r0:r0 + sub, :] = x_ref[0, r0:r0 + sub, :] + gt * yn


def _out_proj(a, b, a_blk, b_blk, w, x, modrows, g, n_ctx, tm, latents_only):
    B, T, D = x.shape
    half_k = w.shape[0] // 2
    off = n_ctx // tm if latents_only else 0
    rows_out = T - n_ctx if latents_only else T
    return pl.pallas_call(
        functools.partial(_out_proj_kernel, tm=tm, n_ctx=n_ctx, row_off=off * tm),
        grid=(B, rows_out // tm),
        in_specs=[pl.BlockSpec((1, tm, half_k), lambda bb, r: (bb, r + off, a_blk)),
                  pl.BlockSpec((1, tm, half_k), lambda bb, r: (bb, r + off, b_blk)),
                  pl.BlockSpec(w.shape, lambda bb, r: (0, 0), pipeline_mode=pl.Buffered(1)),
                  pl.BlockSpec((1, tm, D), lambda bb, r: (bb, r + off, 0)),
                  pl.BlockSpec((1, SUBLANES, D), lambda bb, r: (bb, 0, 0)),
                  pl.BlockSpec((1, D), lambda bb, r: (0, 0))],
        out_specs=pl.BlockSpec((1, tm, D), lambda bb, r: (bb, r, 0)),
        out_shape=jax.ShapeDtypeStruct((B, rows_out, D), F32),
        compiler_params=_cparams(("parallel", "parallel")),
        name="out_proj",
    )(a, b, w, x, modrows, g.reshape(1, D))


def _row_tile(T, cap):
    best = NORM_SLAB
    for t in range(NORM_SLAB, min(T, cap) + 1, NORM_SLAB):
        if T % t == 0:
            best = t
    return best


def _col_tile(N, cap):
    best = 128
    for t in range(128, min(N, cap) + 1, 128):
        if N % t == 0:
            best = t
    return best


def kernel(x, c, ctx, c_ctx, w_mod, b_mod, g_pre, g_post, ev_w_in, ev_w_out, ev_lambda, ev_subln_g,
           ev_hg_lb_logits, ev_hg_norm_g, od_w_in, od_w_pool, od_scale, od_w_out):
    B, S, D = x.shape
    n_ctx = ctx.shape[1]
    depth = w_mod.shape[0]
    T = n_ctx + S
    W = DA_HEADS * HEAD_W

    xc = jnp.concatenate([ctx, x], axis=1)

    n_rows = -(-(B + 1) // SUBLANES) * SUBLANES
    cvec = jnp.zeros((n_rows, D), F32).at[:B].set(c).at[B].set(c_ctx)
    mod = _modulation(cvec, w_mod, b_mod)

    lb_cum = jnp.cumsum(jax.nn.softmax(ev_hg_lb_logits.astype(F32), axis=0), axis=0)
    lb_all = lb_cum - lb_cum[0]
    cos, sin = _rope_tables(n_ctx, S)

    tm_in = _row_tile(T, 1152)
    tm_out = _row_tile(T, 768)

    for l in range(depth):
        last = l == depth - 1
        ml = mod[l]
        lat = ml[:B].reshape(B, 3, D)
        cx = jnp.broadcast_to(ml[B].reshape(1, 3, D), (B, 3, D))
        modrows = jnp.concatenate([lat, cx, jnp.zeros((B, SUBLANES - 6, D), F32)], axis=1)

        if l % 2 == 0:
            e = l // 2
            w_in = ev_w_in[e]
            w_perm = jnp.concatenate([w_in[:, :5 * W], w_in[:, 7 * W:], w_in[:, 5 * W:7 * W]], axis=1).astype(BF16)
            p, pf = _norm_proj(xc, modrows, g_pre[l], w_perm, 2 * W, n_ctx, tm_in, _col_tile(W, 1024))

            lam_init = 0.8 - 0.6 * math.exp(-0.3 * l)
            lv = ev_lambda[e].astype(F32)
            lam = jnp.exp(jnp.sum(lv[0] * lv[1])) - jnp.exp(jnp.sum(lv[2] * lv[3])) + lam_init
            a = _diff_attention(p, lam.reshape(1), cos, sin, ev_subln_g[e], 1.0 - lam_init, n_ctx)
            lb = lb_all[e].reshape(2, HG_HEADS, HG_K).transpose(1, 0, 2)
            bh = _hgrn(p, pf, lb, ev_hg_norm_g[e], n_ctx, 4)
            w_out = ev_w_out[e].astype(BF16)
            ya, yb, ia, ib = a, bh, 0, 0
        else:
            o = l // 2
            p = _norm_proj(xc, modrows, g_pre[l], od_w_in[o].astype(BF16), 0, n_ctx, tm_in,
                           _col_tile(od_w_in.shape[2], 1024))
            y = _pool_mixer(p, od_w_pool[o].astype(BF16), od_scale[o], n_ctx)
            w_out = od_w_out[o].astype(BF16)
            ya, yb, ia, ib = y, y, 0, 1

        xc = _out_proj(ya, yb, ia, ib, w_out, xc, modrows, g_post[l], n_ctx,
                       n_ctx if last else tm_out, latents_only=last)
    return xc
```

```python
import functools
import math

import jax
import jax.numpy as jnp
import numpy as np
from jax import lax
from jax.experimental import pallas as pl
from jax.experimental.pallas import tpu as pltpu

DA_HEADS = 8
DA_D = 64
HEAD_W = 2 * DA_D
HG_HEADS = 8
HG_K = 128
GRID_W = 64
ROPE_BASE = 10000.0
EPS = 1e-6
POOL_WINDOWS = (2, 4, 8, 16)
HG_CHUNK = 32
LOG2E = math.log2(math.e)
ATTN_Q_BLOCK = 2304
ATTN_Q_SUB = 128
ATTN_AHEAD = 5
HG_GROUP = 8
HG_PREP_UNROLL = 8
SUBLANES = 8
POOL_SUB = 768
OUT_SUB = 256
NORM_SLAB = 16
NORM_UNROLL = 4

F32 = jnp.float32
BF16 = jnp.bfloat16
VMEM_LIMIT = 56 * 1024 * 1024


def _cparams(sem):
    return pltpu.CompilerParams(dimension_semantics=sem, vmem_limit_bytes=VMEM_LIMIT)


def _sigmoid(x):
    return 1.0 / (1.0 + jnp.exp(-x))


def _mod_kernel(c_ref, w_ref, b_ref, o_ref):
    c = c_ref[...]
    s = c * _sigmoid(c)
    o_ref[0] = jnp.dot(s, w_ref[0], preferred_element_type=F32,
                       precision=lax.Precision.HIGHEST) + b_ref[0]


def _modulation(cvec, w_mod, b_mod):
    L, D, N = w_mod.shape
    R = cvec.shape[0]
    tn = 1024 if N % 1024 == 0 else N
    return pl.pallas_call(
        _mod_kernel,
        grid=(L, N // tn),
        in_specs=[pl.BlockSpec((R, D), lambda l, j: (0, 0)),
                  pl.BlockSpec((1, D, tn), lambda l, j: (l, 0, j)),
                  pl.BlockSpec((1, 1, tn), lambda l, j: (l, 0, j))],
        out_specs=pl.BlockSpec((1, R, tn), lambda l, j: (l, 0, j)),
        out_shape=jax.ShapeDtypeStruct((L, R, N), F32),
        compiler_params=_cparams(("parallel", "parallel")),
        name="modulation",
    )(cvec, w_mod, b_mod.reshape(L, 1, N))


def _row_mod(mod_ref, lat_row, ctx_row, row0, tm, n_ctx):
    rows = row0 + lax.broadcasted_iota(jnp.int32, (tm, 1), 0)
    return jnp.where(rows < n_ctx, mod_ref[0, ctx_row:ctx_row + 1, :], mod_ref[0, lat_row:lat_row + 1, :])


def _norm_proj_kernel(x_ref, mod_ref, g_ref, w_ref, o_ref, *rest, tm, n_ctx, n_lo_tiles):
    h_scr, gs_scr = rest[-2:]
    r = pl.program_id(1)
    j = pl.program_id(2)

    @pl.when(j == 0)
    def _():
        gs_scr[0:1, :] = g_ref[...] * (1.0 + mod_ref[0, 1:2, :])
        gs_scr[1:2, :] = mod_ref[0, 0:1, :]
        gs_scr[2:3, :] = g_ref[...] * (1.0 + mod_ref[0, 4:5, :])
        gs_scr[3:4, :] = mod_ref[0, 3:4, :]

        def slab(k, carry):
            r0 = pl.multiple_of(k * NORM_SLAB, NORM_SLAB)
            base = jnp.where(r * tm + r0 < n_ctx, 2, 0)
            x = x_ref[0, pl.ds(r0, NORM_SLAB), :]
            y = x * lax.rsqrt(jnp.mean(x * x, axis=-1, keepdims=True) + EPS)
            h_scr[pl.ds(r0, NORM_SLAB), :] = (y * gs_scr[pl.ds(base, 1), :] + gs_scr[pl.ds(base + 1, 1), :]).astype(BF16)
            return carry

        lax.fori_loop(0, tm // NORM_SLAB, slab, 0, unroll=NORM_UNROLL)

    if len(rest) == 2:
        o_ref[0] = jnp.dot(h_scr[...], w_ref[...], preferred_element_type=F32).astype(o_ref.dtype)
    else:
        @pl.when(j < n_lo_tiles)
        def _():
            o_ref[0] = jnp.dot(h_scr[...], w_ref[...], preferred_element_type=F32).astype(o_ref.dtype)

        @pl.when(j >= n_lo_tiles)
        def _():
            rest[0][0] = jnp.dot(h_scr[...], w_ref[...], preferred_element_type=F32)


def _norm_proj(x, modrows, g, w, n_f32, n_ctx, tm, tn):
    B, T, D = x.shape
    n_lo = w.shape[1] - n_f32
    assert n_lo % tn == 0 and n_f32 % tn == 0
    lo_tiles, hi_tiles = n_lo // tn, n_f32 // tn
    out_specs = [pl.BlockSpec((1, tm, tn), lambda b, r, j: (b, r, jnp.minimum(j, lo_tiles - 1)))]
    out_shape = [jax.ShapeDtypeStruct((B, T, n_lo), BF16)]
    if hi_tiles:
        out_specs.append(pl.BlockSpec((1, tm, tn), lambda b, r, j: (b, r, jnp.maximum(j - lo_tiles, 0))))
        out_shape.append(jax.ShapeDtypeStruct((B, T, n_f32), F32))
    outs = pl.pallas_call(
        functools.partial(_norm_proj_kernel, tm=tm, n_ctx=n_ctx, n_lo_tiles=lo_tiles),
        grid=(B, T // tm, lo_tiles + hi_tiles),
        in_specs=[pl.BlockSpec((1, tm, D), lambda b, r, j: (b, r, 0)),
                  pl.BlockSpec((1, SUBLANES, D), lambda b, r, j: (b, 0, 0)),
                  pl.BlockSpec((1, D), lambda b, r, j: (0, 0)),
                  pl.BlockSpec((D, tn), lambda b, r, j: (0, j))],
        out_specs=out_specs,
        out_shape=out_shape,
        scratch_shapes=[pltpu.VMEM((tm, D), BF16), pltpu.VMEM((4, D), F32)],
        compiler_params=_cparams(("parallel", "parallel", "arbitrary")),
        name="norm_proj",
    )(x, modrows, g.reshape(1, D), w)
    return outs if hi_tiles else outs[0]


def _rope(x, cos, sin_signed):
    lane = lax.broadcasted_iota(jnp.int32, x.shape, 1)
    n = x.shape[1]
    partner = jnp.where(lane % 32 < 16, pltpu.roll(x, n - 16, axis=1), pltpu.roll(x, 16, axis=1))
    return x * cos + partner * sin_signed


def _attn_kernel(lam_ref, q_ref, k_ref, v_ref, gate_ref, cq_ref, sq_ref, ck_ref, sk_ref, g_ref,
                 o_ref, k_scr, vt_scr, *, tq, n_ctx, out_scale):
    i = pl.program_id(2)
    n_keys = k_scr.shape[0]

    @pl.when(i == 0)
    def _():
        k_scr[...] = _rope(k_ref[0].astype(F32), ck_ref[...], sk_ref[...]).astype(BF16)
        vt_scr[...] = v_ref[0].astype(F32).T.astype(BF16)

    qr = _rope(q_ref[0].astype(F32), cq_ref[...], sq_ref[...]) * (DA_D ** -0.5 * LOG2E)
    first_map = lax.broadcasted_iota(jnp.int32, qr.shape, 1) < DA_D
    q_maps = (jnp.where(first_map, qr, 0.0).astype(BF16), jnp.where(first_map, 0.0, qr).astype(BF16))
    lam = lam_ref[0]
    qs = ATTN_Q_SUB

    def attend(n_ctx_rows):
        def scores(r0):
            nk = n_ctx if r0 < n_ctx_rows else n_keys
            q2 = jnp.concatenate([q_maps[0][r0:r0 + qs], q_maps[1][r0:r0 + qs]], axis=0)
            return lax.dot_general(k_scr[:nk, :], q2, (((1,), (1,)), ((), ())),
                                   preferred_element_type=F32)

        starts = list(range(0, tq, qs))
        pending = [scores(r0) for r0 in starts[:ATTN_AHEAD]]
        for n, r0 in enumerate(starts):
            rows = slice(r0, r0 + qs)
            s = pending.pop(0)
            nk = s.shape[0]
            if n + ATTN_AHEAD < len(starts):
                pending.append(scores(starts[n + ATTN_AHEAD]))
            e = jnp.exp2(s - jnp.max(s, axis=0, keepdims=True))
            rinv = 1.0 / jnp.sum(e, axis=0, keepdims=True)
            ov = jnp.dot(vt_scr[:, :nk], e.astype(BF16), preferred_element_type=F32) * rinv
            o = (ov[:, :qs] - lam * ov[:, qs:]).T
            y = o * lax.rsqrt(jnp.mean(o * o, axis=-1, keepdims=True) + EPS) * g_ref[...] * out_scale
            gate = gate_ref[0, rows, :].astype(F32)
            o_ref[0, rows, :] = (y * (gate * _sigmoid(gate))).astype(o_ref.dtype)

    @pl.when(i == 0)
    def _():
        attend(n_ctx)

    @pl.when(i != 0)
    def _():
        attend(0)


def _diff_attention(p, lam, cos, sin, subln_g, out_scale, n_ctx):
    B, T, _ = p.shape
    tq = max(t for t in range(n_ctx, ATTN_Q_BLOCK + 1, n_ctx) if T % t == 0)
    assert n_ctx % ATTN_Q_SUB == 0
    H = DA_HEADS
    head = lambda part: (lambda b, h, i: (b, i, part * H + h))
    head_all = lambda part: (lambda b, h, i: (b, 0, part * H + h))
    return pl.pallas_call(
        functools.partial(_attn_kernel, tq=tq, n_ctx=n_ctx, out_scale=out_scale),
        grid=(B, H, T // tq),
        in_specs=[pl.BlockSpec(memory_space=pltpu.SMEM),
                  pl.BlockSpec((1, tq, HEAD_W), head(0)),
                  pl.BlockSpec((1, T, HEAD_W), head_all(1)),
                  pl.BlockSpec((1, T, HEAD_W), head_all(2)),
                  pl.BlockSpec((1, tq, HEAD_W), head(3)),
                  pl.BlockSpec((tq, HEAD_W), lambda b, h, i: (i, 0)),
                  pl.BlockSpec((tq, HEAD_W), lambda b, h, i: (i, 0)),
                  pl.BlockSpec((T, HEAD_W), lambda b, h, i: (0, 0)),
                  pl.BlockSpec((T, HEAD_W), lambda b, h, i: (0, 0)),
                  pl.BlockSpec((1, HEAD_W), lambda b, h, i: (0, 0))],
        out_specs=pl.BlockSpec((1, tq, HEAD_W), lambda b, h, i: (b, i, h)),
        out_shape=jax.ShapeDtypeStruct((B, T, H * HEAD_W), BF16),
        scratch_shapes=[pltpu.VMEM((T, HEAD_W), BF16), pltpu.VMEM((HEAD_W, T), BF16)],
        compiler_params=_cparams(("parallel", "parallel", "arbitrary")),
        name="diff_attention",
    )(lam, p, p, p, p, cos, sin, cos, sin, subln_g.reshape(1, HEAD_W))


def _rope_tables(n_ctx, seq):
    n_freq = DA_D // 4
    inv = np.power(np.float32(ROPE_BASE), -np.arange(n_freq, dtype=np.float32) / np.float32(n_freq))
    t = np.arange(seq)
    lane = np.arange(HEAD_W)
    pos = np.where((lane % DA_D < DA_D // 2)[None, :], (t // GRID_W)[:, None], (t % GRID_W)[:, None]).astype(np.float32)
    ang = (pos * inv[lane % n_freq][None, :]).astype(np.float64)
    sign = np.where(lane % 32 < 16, -1.0, 1.0)[None, :]
    cos = np.concatenate([np.ones((n_ctx, HEAD_W)), np.cos(ang)], axis=0).astype(np.float32)
    sin = np.concatenate([np.zeros((n_ctx, HEAD_W)), np.sin(ang) * sign], axis=0).astype(np.float32)
    return jnp.asarray(cos), jnp.asarray(sin)


def _chunk_cumsum(g, reverse):
    n = g.shape[0]
    row = lax.broadcasted_iota(jnp.int32, g.shape, 0)
    b = g
    sh = 1
    while sh < n:
        if reverse:
            b = b + jnp.where(row < n - sh, pltpu.roll(b, n - sh, axis=0), 0.0)
        else:
            b = b + jnp.where(row >= sh, pltpu.roll(b, sh, axis=0), 0.0)
        sh *= 2
    return b


def _diag_tiles(qq, b2, c2, reverse):
    half = SUBLANES
    sub = lax.broadcasted_iota(jnp.int32, (half, qq.shape[1]), 0)
    tiles = []
    for s in range(qq.shape[0]):
        sb = s // half
        rows = slice(sb * half, (sb + 1) * half)
        w = qq[rows] * jnp.exp2(b2[rows] - c2[s:s + 1])
        keep = (sub <= s - sb * half) if reverse else (sub >= s - sb * half)
        tiles.append((s, sb, jnp.where(keep, w, 0.0)))
    return tiles


def _off_diag_pieces(qq, b2, c2, reverse):
    C = qq.shape[0]
    pieces = []
    span = SUBLANES
    while span < C:
        for u in range(0, C, 2 * span):
            lo, hi = slice(u, u + span), slice(u + span, u + 2 * span)
            if reverse:
                q_rows, k_rows, ref = lo, hi, b2[u + span:u + span + 1]
            else:
                q_rows, k_rows, ref = hi, lo, b2[u + span - 1:u + span]
            pieces.append((q_rows, k_rows, qq[q_rows] * jnp.exp2(b2[q_rows] - ref), jnp.exp2(ref - c2[k_rows])))
        span *= 2
    return pieces


def _off_diag_rows(chunk):
    levels = int(math.log2(chunk // SUBLANES))
    return chunk * levels


def _lane_sums(tiles, ones_ref):
    n = len(tiles)
    assert n % 2 == 0
    lhs = jnp.concatenate([jnp.concatenate([tiles[2 * i], tiles[2 * i + 1]], axis=1) for i in range(n // 2)], axis=0)
    r = jnp.dot(lhs.astype(BF16), ones_ref[...], preferred_element_type=F32)
    w = tiles[0].shape[1]
    out = []
    for i in range(n // 2):
        rows = slice(i * SUBLANES, (i + 1) * SUBLANES)
        out += [r[rows, :w], r[rows, w:]]
    return out


def _hgrn_kernel(q_ref, v_ref, gate_ref, zf_ref, zb_ref, lb_ref, g_ref, o_ref,
                 qt_scr, oi_scr, ox_scr, dl_scr, u_scr, st_scr, ones_scr, blk_scr, *, n_ctx):
    T = q_ref.shape[1]
    C = HG_CHUNK
    K = HG_K
    nc = T // C
    nc_ctx = n_ctx // C
    same_half = (lax.broadcasted_iota(jnp.int32, (2 * K, 2 * K), 0) < K) == (lax.broadcasted_iota(jnp.int32, (2 * K, 2 * K), 1) < K)
    ones_scr[...] = jnp.where(same_half, 1.0, 0.0).astype(BF16)
    sizes = []
    span = SUBLANES
    while span < C:
        sizes += [span] * (C // (2 * span))
        span *= 2
    bounds = [sum((sizes * 2)[:i]) for i in range(1, 2 * len(sizes))]
    n_off = blk_scr.shape[0]
    row_i = lax.broadcasted_iota(jnp.int32, (n_off, n_off), 0)
    col_i = lax.broadcasted_iota(jnp.int32, (n_off, n_off), 1)
    piece_of_row = sum((row_i >= b).astype(jnp.int32) for b in bounds)
    piece_of_col = sum((col_i >= b).astype(jnp.int32) for b in bounds)
    blk_scr[...] = jnp.where(piece_of_row == piece_of_col, 1.0, 0.0)

    def stage1(c):
        r0 = pl.multiple_of(c * C, C)
        hq = q_ref[0, pl.ds(r0, C), :].astype(F32)
        qq = hq * _sigmoid(hq)
        vb = v_ref[0, pl.ds(r0, C), :]
        v = vb.astype(F32)
        tiles, keys, pieces = [], [], []
        for d, z_ref in enumerate((zf_ref, zb_ref)):
            reverse = d == 1
            z = z_ref[0, pl.ds(r0, C), :]
            lb = lb_ref[0, d:d + 1, :]
            e = jnp.exp(-jnp.abs(z))
            inv = 1.0 / (1.0 + e)
            sig = jnp.where(z >= 0, inv, e * inv)
            nsig = jnp.where(z >= 0, e * inv, inv)
            lk2 = jnp.log2(1.0 - lb) + jnp.log2(nsig)
            b2 = _chunk_cumsum(jnp.log2(lb + (1.0 - lb) * sig), reverse)
            b2_last = b2[0:1] if reverse else b2[C - 1:C]
            qt_scr[d, pl.ds(r0, C), :] = (qq * jnp.exp2(b2)).astype(BF16)
            keys.append(jnp.exp2(lk2 + (b2_last - b2)).astype(BF16))
            dl_scr[d, pl.ds(c, 1), :] = jnp.exp2(b2_last)
            tiles.append(_diag_tiles(qq, b2, b2 - lk2, reverse))
            pieces.append(_off_diag_pieces(qq, b2, b2 - lk2, reverse))
        sums = _lane_sums([w for per_dir in tiles for _, _, w in per_dir], ones_scr)
        q_off = jnp.concatenate([p[2] for per_dir in pieces for p in per_dir], axis=0).astype(BF16)
        k_off = jnp.concatenate([p[3] for per_dir in pieces for p in per_dir], axis=0).astype(BF16)
        cross = lax.dot_general(q_off, k_off, (((1,), (1,)), ((), ())), preferred_element_type=F32)
        upd = lax.dot_general(vb, jnp.concatenate(keys, axis=1), (((0,), (0,)), ((), ())),
                              preferred_element_type=F32)
        u_scr[0, c] = upd[:, :K]
        u_scr[1, c] = upd[:, K:]
        return r0, v, tiles, sums, pieces, cross

    def stage2(r0, v, tiles, sums, pieces, cross):
        v_off = jnp.concatenate([v[p[1]] for per_dir in pieces for p in per_dir], axis=0).astype(BF16)
        contrib = jnp.dot((cross * blk_scr[...]).astype(BF16), v_off, preferred_element_type=F32)
        k = 0
        off = 0
        for d in range(2):
            parts = [jnp.zeros((SUBLANES, v.shape[1]), F32) for _ in range(C // SUBLANES)]
            for s, blk, _ in tiles[d]:
                parts[blk] = parts[blk] + sums[k] * v[s:s + 1]
                k += 1
            for q_rows, _, qp, _ in pieces[d]:
                for i in range(qp.shape[0] // SUBLANES):
                    blk = q_rows.start // SUBLANES + i
                    parts[blk] = parts[blk] + contrib[off:off + SUBLANES]
                    off += SUBLANES
            oi_scr[d, pl.ds(r0, C), :] = jnp.concatenate(parts, axis=0)

    def prep(t, carry):
        staged = [stage1(t * HG_PREP_UNROLL + j) for j in range(HG_PREP_UNROLL)]
        for args in staged:
            stage2(*args)
        return carry

    assert nc % HG_PREP_UNROLL == 0
    lax.fori_loop(0, nc // HG_PREP_UNROLL, prep, 0)

    st_scr[...] = jnp.zeros_like(st_scr)

    def scan(m, carry):
        i0 = m * HG_GROUP
        cb0 = jnp.where(i0 < nc_ctx, nc_ctx - 1 - i0, nc + nc_ctx - 1 - i0)
        for j in range(HG_GROUP):
            for d, c in enumerate((i0 + j, cb0 - j)):
                r0 = pl.multiple_of(c * C, C)
                st = st_scr[d]
                ox_scr[d, pl.ds(r0, C), :] = jnp.dot(qt_scr[d, pl.ds(r0, C), :], st.T.astype(BF16),
                                                     preferred_element_type=F32)
                st_scr[d] = dl_scr[d, pl.ds(c, 1), :] * st + u_scr[d, c]
        return carry

    assert nc % HG_GROUP == 0 and nc_ctx % HG_GROUP == 0
    lax.fori_loop(0, nc // HG_GROUP, scan, 0)

    o = (oi_scr[0] + ox_scr[0]) + (oi_scr[1] + ox_scr[1])
    y = o * lax.rsqrt(jnp.mean(o * o, axis=-1, keepdims=True) + EPS) * g_ref[...]
    gate = gate_ref[0].astype(F32)
    o_ref[0] = (y * (gate * _sigmoid(gate))).astype(o_ref.dtype)


def _hgrn(p, pf, lb, norm_g, n_ctx, part0):
    B, T, _ = p.shape
    H = HG_HEADS
    nc = T // HG_CHUNK
    part = lambda k: (lambda b, h: (b, 0, k * H + h))
    blk = (1, T, HEAD_W)
    return pl.pallas_call(
        functools.partial(_hgrn_kernel, n_ctx=n_ctx),
        grid=(B, H),
        in_specs=[pl.BlockSpec(blk, part(part0)), pl.BlockSpec(blk, part(part0 + 1)),
                  pl.BlockSpec(blk, part(part0 + 2)),
                  pl.BlockSpec(blk, part(0)), pl.BlockSpec(blk, part(1)),
                  pl.BlockSpec((1, 2, HG_K), lambda b, h: (h, 0, 0)),
                  pl.BlockSpec((1, HEAD_W), lambda b, h: (0, 0))],
        out_specs=pl.BlockSpec(blk, lambda b, h: (b, 0, h)),
        out_shape=jax.ShapeDtypeStruct((B, T, H * HEAD_W), BF16),
        scratch_shapes=[pltpu.VMEM((2, T, HG_K), BF16),
                        pltpu.VMEM((2, T, HEAD_W), F32), pltpu.VMEM((2, T, HEAD_W), F32),
                        pltpu.VMEM((2, nc, HG_K), F32),
                        pltpu.VMEM((2, nc, HEAD_W, HG_K), F32),
                        pltpu.VMEM((2, HEAD_W, HG_K), F32),
                        pltpu.VMEM((2 * HG_K, 2 * HG_K), BF16),
                        pltpu.VMEM((_off_diag_rows(HG_CHUNK), _off_diag_rows(HG_CHUNK)), F32)],
        compiler_params=_cparams(("parallel", "parallel")),
        name="hgrn2",
    )(p, p, p, pf, pf, lb, norm_g.reshape(1, HEAD_W))


def _shift_rows(x, d):
    n = x.shape[0]
    row = lax.broadcasted_iota(jnp.int32, (n, 1), 0)
    rolled = pltpu.roll(x, (-d) % n, axis=0)
    keep = (row + d >= 0) & (row + d < n)
    return jnp.where(keep, rolled, 0.0)


def _pool_minus_identity(x, w):
    n = x.shape[0]
    ahead = w - w // 2
    behind = w // 2
    fwd = x
    span = 1
    while span < ahead:
        fwd = fwd + _shift_rows(fwd, span)
        span *= 2
    bwd = x
    span = 1
    while span < behind:
        bwd = bwd + _shift_rows(bwd, -span)
        span *= 2
    total = fwd + _shift_rows(bwd, -1)
    row = lax.broadcasted_iota(jnp.int32, (n, 1), 0)
    cnt = jnp.minimum(row + ahead, n) - jnp.maximum(row - behind, 0)
    return total / cnt.astype(F32) - x


def _pool_kernel(u_ref, z_ref, w_ref, ls_ref, o_ref, r_scr, *, n_ctx):
    j = pl.program_id(1)
    T = u_ref.shape[1]
    for jj, win in enumerate(POOL_WINDOWS):
        @pl.when(j == jj)
        def _(win=win):
            for lo, hi in ((0, n_ctx), (n_ctx, T)):
                r_scr[lo:hi, :] = _pool_minus_identity(u_ref[0, lo:hi, :].astype(F32), win).astype(BF16)

    sub = POOL_SUB if T % POOL_SUB == 0 else T
    project = lambda r0: jnp.dot(r_scr[r0:r0 + sub, :], w_ref[0], preferred_element_type=F32)
    y_next = project(0)
    for r0 in range(0, T, sub):
        y = y_next
        if r0 + sub < T:
            y_next = project(r0 + sub)
        z = z_ref[0, r0:r0 + sub, :].astype(F32)
        o_ref[0, r0:r0 + sub, :] = (y * ls_ref[...] * (z * _sigmoid(z))).astype(o_ref.dtype)


def _pool_mixer(p, w_pool, ls, n_ctx):
    B, T, two_e = p.shape
    G, gw, _ = w_pool.shape
    assert G == len(POOL_WINDOWS) and two_e == 2 * G * gw
    return pl.pallas_call(
        functools.partial(_pool_kernel, n_ctx=n_ctx),
        grid=(B, G),
        in_specs=[pl.BlockSpec((1, T, gw), lambda b, j: (b, 0, j)),
                  pl.BlockSpec((1, T, gw), lambda b, j: (b, 0, G + j)),
                  pl.BlockSpec((1, gw, gw), lambda b, j: (j, 0, 0)),
                  pl.BlockSpec((1, gw), lambda b, j: (0, j))],
        out_specs=pl.BlockSpec((1, T, gw), lambda b, j: (b, 0, j)),
        out_shape=jax.ShapeDtypeStruct((B, T, G * gw), BF16),
        scratch_shapes=[pltpu.VMEM((T, gw), BF16)],
        compiler_params=_cparams(("parallel", "parallel")),
        name="pool_mixer",
    )(p, p, w_pool, ls.reshape(1, G * gw))


def _out_proj_kernel(a_ref, b_ref, w_ref, x_ref, mod_ref, g_ref, o_ref, *, tm, n_ctx, row_off):
    ka = a_ref.shape[2]
    sub = OUT_SUB if tm % OUT_SUB == 0 else tm

    def project(r0):
        y = jnp.dot(a_ref[0, r0:r0 + sub, :], w_ref[:ka, :], preferred_element_type=F32)
        return y + jnp.dot(b_ref[0, r0:r0 + sub, :], w_ref[ka:, :], preferred_element_type=F32)

    y_next = project(0)
    for r0 in range(0, tm, sub):
        y = y_next
        if r0 + sub < tm:
            y_next = project(r0 + sub)
        yn = y * lax.rsqrt(jnp.mean(y * y, axis=-1, keepdims=True) + EPS) * g_ref[...]
        gt = _row_mod(mod_ref, 2, 5, row_off + pl.program_id(1) * tm + r0, sub, n_ctx)
        o_ref[0, r0:r0 + sub, :] = x_ref[0, r0:r0 + sub, :] + gt * yn


def _out_proj(a, b, a_blk, b_blk, w, x, modrows, g, n_ctx, tm, latents_only):
    B, T, D = x.shape
    half_k = w.shape[0] // 2
    off = n_ctx // tm if latents_only else 0
    rows_out = T - n_ctx if latents_only else T
    return pl.pallas_call(
        functools.partial(_out_proj_kernel, tm=tm, n_ctx=n_ctx, row_off=off * tm),
        grid=(B, rows_out // tm),
        in_specs=[pl.BlockSpec((1, tm, half_k), lambda bb, r: (bb, r + off, a_blk)),
                  pl.BlockSpec((1, tm, half_k), lambda bb, r: (bb, r + off, b_blk)),
                  pl.BlockSpec(w.shape, lambda bb, r: (0, 0), pipeline_mode=pl.Buffered(1)),
                  pl.BlockSpec((1, tm, D), lambda bb, r: (bb, r + off, 0)),
                  pl.BlockSpec((1, SUBLANES, D), lambda bb, r: (bb, 0, 0)),
                  pl.BlockSpec((1, D), lambda bb, r: (0, 0))],
        out_specs=pl.BlockSpec((1, tm, D), lambda bb, r: (bb, r, 0)),
        out_shape=jax.ShapeDtypeStruct((B, rows_out, D), F32),
        compiler_params=_cparams(("parallel", "parallel")),
        name="out_proj",
    )(a, b, w, x, modrows, g.reshape(1, D))


def _row_tile(T, cap):
    best = NORM_SLAB
    for t in range(NORM_SLAB, min(T, cap) + 1, NORM_SLAB):
        if T % t == 0:
            best = t
    return best


def _col_tile(N, cap):
    best = 128
    for t in range(128, min(N, cap) + 1, 128):
        if N % t == 0:
            best = t
    return best


def kernel(x, c, ctx, c_ctx, w_mod, b_mod, g_pre, g_post, ev_w_in, ev_w_out, ev_lambda, ev_subln_g,
           ev_hg_lb_logits, ev_hg_norm_g, od_w_in, od_w_pool, od_scale, od_w_out):
    B, S, D = x.shape
    n_ctx = ctx.shape[1]
    depth = w_mod.shape[0]
    T = n_ctx + S
    W = DA_HEADS * HEAD_W

    xc = jnp.concatenate([ctx, x], axis=1)

    n_rows = -(-(B + 1) // SUBLANES) * SUBLANES
    cvec = jnp.zeros((n_rows, D), F32).at[:B].set(c).at[B].set(c_ctx)
    mod = _modulation(cvec, w_mod, b_mod)

    lb_cum = jnp.cumsum(jax.nn.softmax(ev_hg_lb_logits.astype(F32), axis=0), axis=0)
    lb_all = lb_cum - lb_cum[0]
    cos, sin = _rope_tables(n_ctx, S)

    tm_in = _row_tile(T, 1152)
    tm_out = _row_tile(T, 768)

    for l in range(depth):
        last = l == depth - 1
        ml = mod[l]
        lat = ml[:B].reshape(B, 3, D)
        cx = jnp.broadcast_to(ml[B].reshape(1, 3, D), (B, 3, D))
        modrows = jnp.concatenate([lat, cx, jnp.zeros((B, SUBLANES - 6, D), F32)], axis=1)

        if l % 2 == 0:
            e = l // 2
            w_in = ev_w_in[e]
            w_perm = jnp.concatenate([w_in[:, :5 * W], w_in[:, 7 * W:], w_in[:, 5 * W:7 * W]], axis=1).astype(BF16)
            p, pf = _norm_proj(xc, modrows, g_pre[l], w_perm, 2 * W, n_ctx, tm_in, _col_tile(W, 1024))

            lam_init = 0.8 - 0.6 * math.exp(-0.3 * l)
            lv = ev_lambda[e].astype(F32)
            lam = jnp.exp(jnp.sum(lv[0] * lv[1])) - jnp.exp(jnp.sum(lv[2] * lv[3])) + lam_init
            a = _diff_attention(p, lam.reshape(1), cos, sin, ev_subln_g[e], 1.0 - lam_init, n_ctx)
            lb = lb_all[e].reshape(2, HG_HEADS, HG_K).transpose(1, 0, 2)
            bh = _hgrn(p, pf, lb, ev_hg_norm_g[e], n_ctx, 4)
            w_out = ev_w_out[e].astype(BF16)
            ya, yb, ia, ib = a, bh, 0, 0
        else:
            o = l // 2
            p = _norm_proj(xc, modrows, g_pre[l], od_w_in[o].astype(BF16), 0, n_ctx, tm_in,
                           _col_tile(od_w_in.shape[2], 1024))
            y = _pool_mixer(p, od_w_pool[o].astype(BF16), od_scale[o], n_ctx)
            w_out = od_w_out[o].astype(BF16)
            ya, yb, ia, ib = y, y, 0, 1

        xc = _out_proj(ya, yb, ia, ib, w_out, xc, modrows, g_post[l], n_ctx,
                       n_ctx if last else tm_out, latents_only=last)
    return xc
```

```python
import functools
import math

import jax
import jax.numpy as jnp
import numpy as np
from jax import lax
from jax.experimental import pallas as pl
from jax.experimental.pallas import tpu as pltpu

DA_HEADS = 8
DA_D = 64
HEAD_W = 2 * DA_D
HG_HEADS = 8
HG_K = 128
GRID_W = 64
ROPE_BASE = 10000.0
EPS = 1e-6
POOL_WINDOWS = (2, 4, 8, 16)
HG_CHUNK = 32
LOG2E = math.log2(math.e)
ATTN_Q_BLOCK = 2304
ATTN_Q_SUB = 128
ATTN_AHEAD = 8
HG_GROUP = 8
HG_PREP_UNROLL = 8
SUBLANES = 8
POOL_SUB = 768
OUT_SUB = 256
NORM_SLAB = 16
NORM_UNROLL = 4

F32 = jnp.float32
BF16 = jnp.bfloat16
VMEM_LIMIT = 56 * 1024 * 1024


def _cparams(sem):
    return pltpu.CompilerParams(dimension_semantics=sem, vmem_limit_bytes=VMEM_LIMIT)


def _sigmoid(x):
    return 1.0 / (1.0 + jnp.exp(-x))


def _mod_kernel(c_ref, w_ref, b_ref, o_ref):
    c = c_ref[...]
    s = c * _sigmoid(c)
    o_ref[0] = jnp.dot(s, w_ref[0], preferred_element_type=F32,
                       precision=lax.Precision.HIGHEST) + b_ref[0]


def _modulation(cvec, w_mod, b_mod):
    L, D, N = w_mod.shape
    R = cvec.shape[0]
    tn = 1024 if N % 1024 == 0 else N
    return pl.pallas_call(
        _mod_kernel,
        grid=(L, N // tn),
        in_specs=[pl.BlockSpec((R, D), lambda l, j: (0, 0)),
                  pl.BlockSpec((1, D, tn), lambda l, j: (l, 0, j)),
                  pl.BlockSpec((1, 1, tn), lambda l, j: (l, 0, j))],
        out_specs=pl.BlockSpec((1, R, tn), lambda l, j: (l, 0, j)),
        out_shape=jax.ShapeDtypeStruct((L, R, N), F32),
        compiler_params=_cparams(("parallel", "parallel")),
        name="modulation",
    )(cvec, w_mod, b_mod.reshape(L, 1, N))


def _row_mod(mod_ref, lat_row, ctx_row, row0, tm, n_ctx):
    rows = row0 + lax.broadcasted_iota(jnp.int32, (tm, 1), 0)
    return jnp.where(rows < n_ctx, mod_ref[0, ctx_row:ctx_row + 1, :], mod_ref[0, lat_row:lat_row + 1, :])


def _norm_proj_kernel(x_ref, mod_ref, g_ref, w_ref, o_ref, *rest, tm, n_ctx, n_lo_tiles):
    h_scr, gs_scr = rest[-2:]
    r = pl.program_id(1)
    j = pl.program_id(2)

    @pl.when(j == 0)
    def _():
        gs_scr[0:1, :] = g_ref[...] * (1.0 + mod_ref[0, 1:2, :])
        gs_scr[1:2, :] = mod_ref[0, 0:1, :]
        gs_scr[2:3, :] = g_ref[...] * (1.0 + mod_ref[0, 4:5, :])
        gs_scr[3:4, :] = mod_ref[0, 3:4, :]

        def slab(k, carry):
            r0 = pl.multiple_of(k * NORM_SLAB, NORM_SLAB)
            base = jnp.where(r * tm + r0 < n_ctx, 2, 0)
            x = x_ref[0, pl.ds(r0, NORM_SLAB), :]
            y = x * lax.rsqrt(jnp.mean(x * x, axis=-1, keepdims=True) + EPS)
            h_scr[pl.ds(r0, NORM_SLAB), :] = (y * gs_scr[pl.ds(base, 1), :] + gs_scr[pl.ds(base + 1, 1), :]).astype(BF16)
            return carry

        lax.fori_loop(0, tm // NORM_SLAB, slab, 0, unroll=NORM_UNROLL)

    if len(rest) == 2:
        o_ref[0] = jnp.dot(h_scr[...], w_ref[...], preferred_element_type=F32).astype(o_ref.dtype)
    else:
        @pl.when(j < n_lo_tiles)
        def _():
            o_ref[0] = jnp.dot(h_scr[...], w_ref[...], preferred_element_type=F32).astype(o_ref.dtype)

        @pl.when(j >= n_lo_tiles)
        def _():
            rest[0][0] = jnp.dot(h_scr[...], w_ref[...], preferred_element_type=F32)


def _norm_proj(x, modrows, g, w, n_f32, n_ctx, tm, tn, col_order=None):
    B, T, D = x.shape
    n_lo = w.shape[1] - n_f32
    assert n_lo % tn == 0 and n_f32 % tn == 0
    lo_tiles, hi_tiles = n_lo // tn, n_f32 // tn
    if col_order is None:
        w_tile = lambda j: j
    else:
        assert sorted(col_order) == list(range(lo_tiles + hi_tiles))
        w_tile = lambda j: sum(jnp.where(j == k, src, 0) for k, src in enumerate(col_order))
    out_specs = [pl.BlockSpec((1, tm, tn), lambda b, r, j: (b, r, jnp.minimum(j, lo_tiles - 1)))]
    out_shape = [jax.ShapeDtypeStruct((B, T, n_lo), BF16)]
    if hi_tiles:
        out_specs.append(pl.BlockSpec((1, tm, tn), lambda b, r, j: (b, r, jnp.maximum(j - lo_tiles, 0))))
        out_shape.append(jax.ShapeDtypeStruct((B, T, n_f32), F32))
    outs = pl.pallas_call(
        functools.partial(_norm_proj_kernel, tm=tm, n_ctx=n_ctx, n_lo_tiles=lo_tiles),
        grid=(B, T // tm, lo_tiles + hi_tiles),
        in_specs=[pl.BlockSpec((1, tm, D), lambda b, r, j: (b, r, 0)),
                  pl.BlockSpec((1, SUBLANES, D), lambda b, r, j: (b, 0, 0)),
                  pl.BlockSpec((1, D), lambda b, r, j: (0, 0)),
                  pl.BlockSpec((D, tn), lambda b, r, j: (0, w_tile(j)))],
        out_specs=out_specs,
        out_shape=out_shape,
        scratch_shapes=[pltpu.VMEM((tm, D), BF16), pltpu.VMEM((4, D), F32)],
        compiler_params=_cparams(("parallel", "parallel", "arbitrary")),
        name="norm_proj",
    )(x, modrows, g.reshape(1, D), w)
    return outs if hi_tiles else outs[0]


def _rope(x, cos, sin_signed):
    lane = lax.broadcasted_iota(jnp.int32, x.shape, 1)
    n = x.shape[1]
    partner = jnp.where(lane % 32 < 16, pltpu.roll(x, n - 16, axis=1), pltpu.roll(x, 16, axis=1))
    return x * cos + partner * sin_signed


def _attn_kernel(lam_ref, q_ref, k_ref, v_ref, gate_ref, cq_ref, sq_ref, ck_ref, sk_ref, g_ref,
                 o_ref, k_scr, vt_scr, *, tq, n_ctx, out_scale):
    i = pl.program_id(2)
    n_keys = k_scr.shape[0]

    @pl.when(i == 0)
    def _():
        k_scr[...] = _rope(k_ref[0].astype(F32), ck_ref[...], sk_ref[...]).astype(BF16)
        vt_scr[...] = v_ref[0].astype(F32).T.astype(BF16)

    qr = _rope(q_ref[0].astype(F32), cq_ref[...], sq_ref[...]) * (DA_D ** -0.5 * LOG2E)
    first_map = lax.broadcasted_iota(jnp.int32, qr.shape, 1) < DA_D
    q_maps = (jnp.where(first_map, qr, 0.0).astype(BF16), jnp.where(first_map, 0.0, qr).astype(BF16))
    lam = lam_ref[0]
    qs = ATTN_Q_SUB

    def attend(n_ctx_rows):
        def scores(r0):
            nk = n_ctx if r0 < n_ctx_rows else n_keys
            q2 = jnp.concatenate([q_maps[0][r0:r0 + qs], q_maps[1][r0:r0 + qs]], axis=0)
            return lax.dot_general(k_scr[:nk, :], q2, (((1,), (1,)), ((), ())),
                                   preferred_element_type=F32)

        starts = list(range(0, tq, qs))
        pending = [scores(r0) for r0 in starts[:ATTN_AHEAD]]
        for n, r0 in enumerate(starts):
            rows = slice(r0, r0 + qs)
            s = pending.pop(0)
            nk = s.shape[0]
            if n + ATTN_AHEAD < len(starts):
                pending.append(scores(starts[n + ATTN_AHEAD]))
            e = jnp.exp2(s - jnp.max(s, axis=0, keepdims=True))
            rinv = 1.0 / jnp.sum(e, axis=0, keepdims=True)
            ov = jnp.dot(vt_scr[:, :nk], e.astype(BF16), preferred_element_type=F32) * rinv
            o = (ov[:, :qs] - lam * ov[:, qs:]).T
            y = o * lax.rsqrt(jnp.mean(o * o, axis=-1, keepdims=True) + EPS) * g_ref[...] * out_scale
            gate = gate_ref[0, rows, :].astype(F32)
            o_ref[0, rows, :] = (y * (gate * _sigmoid(gate))).astype(o_ref.dtype)

    @pl.when(i == 0)
    def _():
        attend(n_ctx)

    @pl.when(i != 0)
    def _():
        attend(0)


def _diff_attention(p, lam, cos, sin, subln_g, out_scale, n_ctx):
    B, T, _ = p.shape
    tq = max(t for t in range(n_ctx, ATTN_Q_BLOCK + 1, n_ctx) if T % t == 0)
    assert n_ctx % ATTN_Q_SUB == 0
    H = DA_HEADS
    head = lambda part: (lambda b, h, i: (b, i, part * H + h))
    head_all = lambda part: (lambda b, h, i: (b, 0, part * H + h))
    return pl.pallas_call(
        functools.partial(_attn_kernel, tq=tq, n_ctx=n_ctx, out_scale=out_scale),
        grid=(B, H, T // tq),
        in_specs=[pl.BlockSpec(memory_space=pltpu.SMEM),
                  pl.BlockSpec((1, tq, HEAD_W), head(0)),
                  pl.BlockSpec((1, T, HEAD_W), head_all(1)),
                  pl.BlockSpec((1, T, HEAD_W), head_all(2)),
                  pl.BlockSpec((1, tq, HEAD_W), head(3)),
                  pl.BlockSpec((tq, HEAD_W), lambda b, h, i: (i, 0)),
                  pl.BlockSpec((tq, HEAD_W), lambda b, h, i: (i, 0)),
                  pl.BlockSpec((T, HEAD_W), lambda b, h, i: (0, 0)),
                  pl.BlockSpec((T, HEAD_W), lambda b, h, i: (0, 0)),
                  pl.BlockSpec((1, HEAD_W), lambda b, h, i: (0, 0))],
        out_specs=pl.BlockSpec((1, tq, HEAD_W), lambda b, h, i: (b, i, h)),
        out_shape=jax.ShapeDtypeStruct((B, T, H * HEAD_W), BF16),
        scratch_shapes=[pltpu.VMEM((T, HEAD_W), BF16), pltpu.VMEM((HEAD_W, T), BF16)],
        compiler_params=_cparams(("parallel", "parallel", "arbitrary")),
        name="diff_attention",
    )(lam, p, p, p, p, cos, sin, cos, sin, subln_g.reshape(1, HEAD_W))


def _rope_tables(n_ctx, seq):
    n_freq = DA_D // 4
    inv = np.power(np.float32(ROPE_BASE), -np.arange(n_freq, dtype=np.float32) / np.float32(n_freq))
    t = np.arange(seq)
    lane = np.arange(HEAD_W)
    pos = np.where((lane % DA_D < DA_D // 2)[None, :], (t // GRID_W)[:, None], (t % GRID_W)[:, None]).astype(np.float32)
    ang = (pos * inv[lane % n_freq][None, :]).astype(np.float64)
    sign = np.where(lane % 32 < 16, -1.0, 1.0)[None, :]
    cos = np.concatenate([np.ones((n_ctx, HEAD_W)), np.cos(ang)], axis=0).astype(np.float32)
    sin = np.concatenate([np.zeros((n_ctx, HEAD_W)), np.sin(ang) * sign], axis=0).astype(np.float32)
    return jnp.asarray(cos), jnp.asarray(sin)


def _chunk_cumsum(g, reverse):
    n = g.shape[0]
    row = lax.broadcasted_iota(jnp.int32, g.shape, 0)
    b = g
    sh = 1
    while sh < n:
        if reverse:
            b = b + jnp.where(row < n - sh, pltpu.roll(b, n - sh, axis=0), 0.0)
        else:
            b = b + jnp.where(row >= sh, pltpu.roll(b, sh, axis=0), 0.0)
        sh *= 2
    return b


def _diag_tiles(qq, b2, c2, reverse):
    half = SUBLANES
    sub = lax.broadcasted_iota(jnp.int32, (half, qq.shape[1]), 0)
    tiles = []
    for s in range(qq.shape[0]):
        sb = s // half
        rows = slice(sb * half, (sb + 1) * half)
        w = qq[rows] * jnp.exp2(b2[rows] - c2[s:s + 1])
        keep = (sub <= s - sb * half) if reverse else (sub >= s - sb * half)
        tiles.append((s, sb, jnp.where(keep, w, 0.0)))
    return tiles


def _off_diag_pieces(qq, b2, c2, reverse):
    C = qq.shape[0]
    pieces = []
    span = SUBLANES
    while span < C:
        for u in range(0, C, 2 * span):
            lo, hi = slice(u, u + span), slice(u + span, u + 2 * span)
            if reverse:
                q_rows, k_rows, ref = lo, hi, b2[u + span:u + span + 1]
            else:
                q_rows, k_rows, ref = hi, lo, b2[u + span - 1:u + span]
            pieces.append((q_rows, k_rows, qq[q_rows] * jnp.exp2(b2[q_rows] - ref), jnp.exp2(ref - c2[k_rows])))
        span *= 2
    return pieces


def _off_diag_rows(chunk):
    levels = int(math.log2(chunk // SUBLANES))
    return chunk * levels


def _lane_sums(tiles, ones_ref):
    n = len(tiles)
    assert n % 2 == 0
    lhs = jnp.concatenate([jnp.concatenate([tiles[2 * i], tiles[2 * i + 1]], axis=1) for i in range(n // 2)], axis=0)
    r = jnp.dot(lhs.astype(BF16), ones_ref[...], preferred_element_type=F32)
    w = tiles[0].shape[1]
    out = []
    for i in range(n // 2):
        rows = slice(i * SUBLANES, (i + 1) * SUBLANES)
        out += [r[rows, :w], r[rows, w:]]
    return out


def _hgrn_kernel(q_ref, v_ref, gate_ref, zf_ref, zb_ref, lb_ref, g_ref, o_ref,
                 qt_scr, oi_scr, ox_scr, dl_scr, u_scr, st_scr, ones_scr, blk_scr, *, n_ctx):
    T = q_ref.shape[1]
    C = HG_CHUNK
    K = HG_K
    nc = T // C
    nc_ctx = n_ctx // C
    same_half = (lax.broadcasted_iota(jnp.int32, (2 * K, 2 * K), 0) < K) == (lax.broadcasted_iota(jnp.int32, (2 * K, 2 * K), 1) < K)
    ones_scr[...] = jnp.where(same_half, 1.0, 0.0).astype(BF16)
    sizes = []
    span = SUBLANES
    while span < C:
        sizes += [span] * (C // (2 * span))
        span *= 2
    bounds = [sum((sizes * 2)[:i]) for i in range(1, 2 * len(sizes))]
    n_off = blk_scr.shape[0]
    row_i = lax.broadcasted_iota(jnp.int32, (n_off, n_off), 0)
    col_i = lax.broadcasted_iota(jnp.int32, (n_off, n_off), 1)
    piece_of_row = sum((row_i >= b).astype(jnp.int32) for b in bounds)
    piece_of_col = sum((col_i >= b).astype(jnp.int32) for b in bounds)
    blk_scr[...] = jnp.where(piece_of_row == piece_of_col, 1.0, 0.0)

    def stage1(c):
        r0 = pl.multiple_of(c * C, C)
        hq = q_ref[0, pl.ds(r0, C), :].astype(F32)
        qq = hq * _sigmoid(hq)
        vb = v_ref[0, pl.ds(r0, C), :]
        v = vb.astype(F32)
        tiles, keys, pieces = [], [], []
        for d, z_ref in enumerate((zf_ref, zb_ref)):
            reverse = d == 1
            z = z_ref[0, pl.ds(r0, C), :]
            lb = lb_ref[0, d:d + 1, :]
            e = jnp.exp(-jnp.abs(z))
            inv = 1.0 / (1.0 + e)
            sig = jnp.where(z >= 0, inv, e * inv)
            nsig = jnp.where(z >= 0, e * inv, inv)
            lk2 = jnp.log2(1.0 - lb) + jnp.log2(nsig)
            b2 = _chunk_cumsum(jnp.log2(lb + (1.0 - lb) * sig), reverse)
            b2_last = b2[0:1] if reverse else b2[C - 1:C]
            qt_scr[d, pl.ds(r0, C), :] = (qq * jnp.exp2(b2)).astype(BF16)
            keys.append(jnp.exp2(lk2 + (b2_last - b2)).astype(BF16))
            dl_scr[d, pl.ds(c, 1), :] = jnp.exp2(b2_last)
            tiles.append(_diag_tiles(qq, b2, b2 - lk2, reverse))
            pieces.append(_off_diag_pieces(qq, b2, b2 - lk2, reverse))
        sums = _lane_sums([w for per_dir in tiles for _, _, w in per_dir], ones_scr)
        q_off = jnp.concatenate([p[2] for per_dir in pieces for p in per_dir], axis=0).astype(BF16)
        k_off = jnp.concatenate([p[3] for per_dir in pieces for p in per_dir], axis=0).astype(BF16)
        cross = lax.dot_general(q_off, k_off, (((1,), (1,)), ((), ())), preferred_element_type=F32)
        upd = lax.dot_general(vb, jnp.concatenate(keys, axis=1), (((0,), (0,)), ((), ())),
                              preferred_element_type=F32)
        u_scr[0, c] = upd[:, :K]
        u_scr[1, c] = upd[:, K:]
        return r0, v, tiles, sums, pieces, cross

    def stage2(r0, v, tiles, sums, pieces, cross):
        v_off = jnp.concatenate([v[p[1]] for per_dir in pieces for p in per_dir], axis=0).astype(BF16)
        contrib = jnp.dot((cross * blk_scr[...]).astype(BF16), v_off, preferred_element_type=F32)
        k = 0
        off = 0
        for d in range(2):
            parts = [jnp.zeros((SUBLANES, v.shape[1]), F32) for _ in range(C // SUBLANES)]
            for s, blk, _ in tiles[d]:
                parts[blk] = parts[blk] + sums[k] * v[s:s + 1]
                k += 1
            for q_rows, _, qp, _ in pieces[d]:
                for i in range(qp.shape[0] // SUBLANES):
                    blk = q_rows.start // SUBLANES + i
                    parts[blk] = parts[blk] + contrib[off:off + SUBLANES]
                    off += SUBLANES
            oi_scr[d, pl.ds(r0, C), :] = jnp.concatenate(parts, axis=0)

    def prep(t, carry):
        staged = [stage1(t * HG_PREP_UNROLL + j) for j in range(HG_PREP_UNROLL)]
        for args in staged:
            stage2(*args)
        return carry

    assert nc % HG_PREP_UNROLL == 0
    lax.fori_loop(0, nc // HG_PREP_UNROLL, prep, 0)

    st_scr[...] = jnp.zeros_like(st_scr)

    def scan(m, carry):
        i0 = m * HG_GROUP
        cb0 = jnp.where(i0 < nc_ctx, nc_ctx - 1 - i0, nc + nc_ctx - 1 - i0)
        for j in range(HG_GROUP):
            for d, c in enumerate((i0 + j, cb0 - j)):
                r0 = pl.multiple_of(c * C, C)
                st = st_scr[d]
                ox_scr[d, pl.ds(r0, C), :] = jnp.dot(qt_scr[d, pl.ds(r0, C), :], st.T.astype(BF16),
                                                     preferred_element_type=F32)
                st_scr[d] = dl_scr[d, pl.ds(c, 1), :] * st + u_scr[d, c]
        return carry

    assert nc % HG_GROUP == 0 and nc_ctx % HG_GROUP == 0
    lax.fori_loop(0, nc // HG_GROUP, scan, 0)

    o = (oi_scr[0] + ox_scr[0]) + (oi_scr[1] + ox_scr[1])
    y = o * lax.rsqrt(jnp.mean(o * o, axis=-1, keepdims=True) + EPS) * g_ref[...]
    gate = gate_ref[0].astype(F32)
    o_ref[0] = (y * (gate * _sigmoid(gate))).astype(o_ref.dtype)


def _hgrn(p, pf, lb, norm_g, n_ctx, part0):
    B, T, _ = p.shape
    H = HG_HEADS
    nc = T // HG_CHUNK
    part = lambda k: (lambda b, h: (b, 0, k * H + h))
    blk = (1, T, HEAD_W)
    return pl.pallas_call(
        functools.partial(_hgrn_kernel, n_ctx=n_ctx),
        grid=(B, H),
        in_specs=[pl.BlockSpec(blk, part(part0)), pl.BlockSpec(blk, part(part0 + 1)),
                  pl.BlockSpec(blk, part(part0 + 2)),
                  pl.BlockSpec(blk, part(0)), pl.BlockSpec(blk, part(1)),
                  pl.BlockSpec((1, 2, HG_K), lambda b, h: (h, 0, 0)),
                  pl.BlockSpec((1, HEAD_W), lambda b, h: (0, 0))],
        out_specs=pl.BlockSpec(blk, lambda b, h: (b, 0, h)),
        out_shape=jax.ShapeDtypeStruct((B, T, H * HEAD_W), BF16),
        scratch_shapes=[pltpu.VMEM((2, T, HG_K), BF16),
                        pltpu.VMEM((2, T, HEAD_W), F32), pltpu.VMEM((2, T, HEAD_W), F32),
                        pltpu.VMEM((2, nc, HG_K), F32),
                        pltpu.VMEM((2, nc, HEAD_W, HG_K), F32),
                        pltpu.VMEM((2, HEAD_W, HG_K), F32),
                        pltpu.VMEM((2 * HG_K, 2 * HG_K), BF16),
                        pltpu.VMEM((_off_diag_rows(HG_CHUNK), _off_diag_rows(HG_CHUNK)), F32)],
        compiler_params=_cparams(("parallel", "parallel")),
        name="hgrn2",
    )(p, p, p, pf, pf, lb, norm_g.reshape(1, HEAD_W))


def _shift_rows(x, d):
    n = x.shape[0]
    row = lax.broadcasted_iota(jnp.int32, (n, 1), 0)
    rolled = pltpu.roll(x, (-d) % n, axis=0)
    keep = (row + d >= 0) & (row + d < n)
    return jnp.where(keep, rolled, 0.0)


def _pool_minus_identity(x, w):
    n = x.shape[0]
    ahead = w - w // 2
    behind = w // 2
    fwd = x
    span = 1
    while span < ahead:
        fwd = fwd + _shift_rows(fwd, span)
        span *= 2
    bwd = x
    span = 1
    while span < behind:
        bwd = bwd + _shift_rows(bwd, -span)
        span *= 2
    total = fwd + _shift_rows(bwd, -1)
    row = lax.broadcasted_iota(jnp.int32, (n, 1), 0)
    cnt = jnp.minimum(row + ahead, n) - jnp.maximum(row - behind, 0)
    return total / cnt.astype(F32) - x


def _pool_kernel(u_ref, z_ref, w_ref, ls_ref, o_ref, r_scr, *, n_ctx):
    j = pl.program_id(1)
    T = u_ref.shape[1]
    for jj, win in enumerate(POOL_WINDOWS):
        @pl.when(j == jj)
        def _(win=win):
            for lo, hi in ((0, n_ctx), (n_ctx, T)):
                r_scr[lo:hi, :] = _pool_minus_identity(u_ref[0, lo:hi, :].astype(F32), win).astype(BF16)

    sub = POOL_SUB if T % POOL_SUB == 0 else T
    project = lambda r0: jnp.dot(r_scr[r0:r0 + sub, :], w_ref[0], preferred_element_type=F32)
    y_next = project(0)
    for r0 in range(0, T, sub):
        y = y_next
        if r0 + sub < T:
            y_next = project(r0 + sub)
        z = z_ref[0, r0:r0 + sub, :].astype(F32)
        o_ref[0, r0:r0 + sub, :] = (y * ls_ref[...] * (z * _sigmoid(z))).astype(o_ref.dtype)


def _pool_mixer(p, w_pool, ls, n_ctx):
    B, T, two_e = p.shape
    G, gw, _ = w_pool.shape
    assert G == len(POOL_WINDOWS) and two_e == 2 * G * gw
    return pl.pallas_call(
        functools.partial(_pool_kernel, n_ctx=n_ctx),
        grid=(B, G),
        in_specs=[pl.BlockSpec((1, T, gw), lambda b, j: (b, 0, j)),
                  pl.BlockSpec((1, T, gw), lambda b, j: (b, 0, G + j)),
                  pl.BlockSpec((1, gw, gw), lambda b, j: (j, 0, 0)),
                  pl.BlockSpec((1, gw), lambda b, j: (0, j))],
        out_specs=pl.BlockSpec((1, T, gw), lambda b, j: (b, 0, j)),
        out_shape=jax.ShapeDtypeStruct((B, T, G * gw), BF16),
        scratch_shapes=[pltpu.VMEM((T, gw), BF16)],
        compiler_params=_cparams(("parallel", "parallel")),
        name="pool_mixer",
    )(p, p, w_pool, ls.reshape(1, G * gw))


def _out_proj_kernel(a_ref, b_ref, w_ref, x_ref, mod_ref, g_ref, o_ref, *, tm, n_ctx, row_off):
    ka = a_ref.shape[2]
    sub = OUT_SUB if tm % OUT_SUB == 0 else tm

    def project(r0):
        y = jnp.dot(a_ref[0, r0:r0 + sub, :], w_ref[:ka, :], preferred_element_type=F32)
        return y + jnp.dot(b_ref[0, r0:r0 + sub, :], w_ref[ka:, :], preferred_element_type=F32)

    y_next = project(0)
    for r0 in range(0, tm, sub):
        y = y_next
        if r0 + sub < tm:
            y_next = project(r0 + sub)
        yn = y * lax.rsqrt(jnp.mean(y * y, axis=-1, keepdims=True) + EPS) * g_ref[...]
        gt = _row_mod(mod_ref, 2, 5, row_off + pl.program_id(1) * tm + r0, sub, n_ctx)
        o_ref[0, r0:r0 + sub, :] = x_ref[0, r0:r0 + sub, :] + gt * yn


def _out_proj(a, b, a_blk, b_blk, w, x, modrows, g, n_ctx, tm, latents_only):
    B, T, D = x.shape
    half_k = w.shape[0] // 2
    off = n_ctx // tm if latents_only else 0
    rows_out = T - n_ctx if latents_only else T
    return pl.pallas_call(
        functools.partial(_out_proj_kernel, tm=tm, n_ctx=n_ctx, row_off=off * tm),
        grid=(B, rows_out // tm),
        in_specs=[pl.BlockSpec((1, tm, half_k), lambda bb, r: (bb, r + off, a_blk)),
                  pl.BlockSpec((1, tm, half_k), lambda bb, r: (bb, r + off, b_blk)),
                  pl.BlockSpec(w.shape, lambda bb, r: (0, 0), pipeline_mode=pl.Buffered(1)),
                  pl.BlockSpec((1, tm, D), lambda bb, r: (bb, r + off, 0)),
                  pl.BlockSpec((1, SUBLANES, D), lambda bb, r: (bb, 0, 0)),
                  pl.BlockSpec((1, D), lambda bb, r: (0, 0))],
        out_specs=pl.BlockSpec((1, tm, D), lambda bb, r: (bb, r, 0)),
        out_shape=jax.ShapeDtypeStruct((B, rows_out, D), F32),
        compiler_params=_cparams(("parallel", "parallel")),
        name="out_proj",
    )(a, b, w, x, modrows, g.reshape(1, D))


def _row_tile(T, cap):
    best = NORM_SLAB
    for t in range(NORM_SLAB, min(T, cap) + 1, NORM_SLAB):
        if T % t == 0:
            best = t
    return best


def _col_tile(N, cap):
    best = 128
    for t in range(128, min(N, cap) + 1, 128):
        if N % t == 0:
            best = t
    return best


def kernel(x, c, ctx, c_ctx, w_mod, b_mod, g_pre, g_post, ev_w_in, ev_w_out, ev_lambda, ev_subln_g,
           ev_hg_lb_logits, ev_hg_norm_g, od_w_in, od_w_pool, od_scale, od_w_out):
    B, S, D = x.shape
    n_ctx = ctx.shape[1]
    depth = w_mod.shape[0]
    T = n_ctx + S
    W = DA_HEADS * HEAD_W

    xc = jnp.concatenate([ctx, x], axis=1)

    n_rows = -(-(B + 1) // SUBLANES) * SUBLANES
    cvec = jnp.zeros((n_rows, D), F32).at[:B].set(c).at[B].set(c_ctx)
    mod = _modulation(cvec, w_mod, b_mod)

    lb_cum = jnp.cumsum(jax.nn.softmax(ev_hg_lb_logits.astype(F32), axis=0), axis=0)
    lb_all = lb_cum - lb_cum[0]
    cos, sin = _rope_tables(n_ctx, S)

    tm_in = _row_tile(T, 1152)
    tm_out = _row_tile(T, 768)

    for l in range(depth):
        last = l == depth - 1
        ml = mod[l]
        lat = ml[:B].reshape(B, 3, D)
        cx = jnp.broadcast_to(ml[B].reshape(1, 3, D), (B, 3, D))
        modrows = jnp.concatenate([lat, cx, jnp.zeros((B, SUBLANES - 6, D), F32)], axis=1)

        if l % 2 == 0:
            e = l // 2
            assert _col_tile(W, 1024) == W
            p, pf = _norm_proj(xc, modrows, g_pre[l], ev_w_in[e].astype(BF16), 2 * W, n_ctx, tm_in, W,
                               col_order=(0, 1, 2, 3, 4, 7, 8, 5, 6))

            lam_init = 0.8 - 0.6 * math.exp(-0.3 * l)
            lv = ev_lambda[e].astype(F32)
            lam = jnp.exp(jnp.sum(lv[0] * lv[1])) - jnp.exp(jnp.sum(lv[2] * lv[3])) + lam_init
            a = _diff_attention(p, lam.reshape(1), cos, sin, ev_subln_g[e], 1.0 - lam_init, n_ctx)
            lb = lb_all[e].reshape(2, HG_HEADS, HG_K).transpose(1, 0, 2)
            bh = _hgrn(p, pf, lb, ev_hg_norm_g[e], n_ctx, 4)
            w_out = ev_w_out[e].astype(BF16)
            ya, yb, ia, ib = a, bh, 0, 0
        else:
            o = l // 2
            p = _norm_proj(xc, modrows, g_pre[l], od_w_in[o].astype(BF16), 0, n_ctx, tm_in,
                           _col_tile(od_w_in.shape[2], 1024))
            y = _pool_mixer(p, od_w_pool[o].astype(BF16), od_scale[o], n_ctx)
            w_out = od_w_out[o].astype(BF16)
            ya, yb, ia, ib = y, y, 0, 1

        xc = _out_proj(ya, yb, ia, ib, w_out, xc, modrows, g_post[l], n_ctx,
                       n_ctx if last else tm_out, latents_only=last)
    return xc
```

```python
import functools
import math

import jax
import jax.numpy as jnp
import numpy as np
from jax import lax
from jax.experimental import pallas as pl
from jax.experimental.pallas import tpu as pltpu

DA_HEADS = 8
DA_D = 64
HEAD_W = 2 * DA_D
HG_HEADS = 8
HG_K = 128
GRID_W = 64
ROPE_BASE = 10000.0
EPS = 1e-6
POOL_WINDOWS = (2, 4, 8, 16)
HG_CHUNK = 32
LOG2E = math.log2(math.e)
ATTN_Q_BLOCK = 2304
ATTN_Q_SUB = 128
ATTN_AHEAD = 8
HG_GROUP = 8
HG_PREP_UNROLL = 8
SUBLANES = 8
POOL_SUB = 768
OUT_SUB = 256
NORM_SLAB = 16
NORM_UNROLL = 4

F32 = jnp.float32
BF16 = jnp.bfloat16
VMEM_LIMIT = 56 * 1024 * 1024


def _cparams(sem):
    return pltpu.CompilerParams(dimension_semantics=sem, vmem_limit_bytes=VMEM_LIMIT)


def _sigmoid(x):
    return 1.0 / (1.0 + jnp.exp(-x))


def _mod_kernel(c_ref, w_ref, b_ref, o_ref):
    c = c_ref[...]
    s = c * _sigmoid(c)
    o_ref[0] = jnp.dot(s, w_ref[0], preferred_element_type=F32,
                       precision=lax.Precision.HIGHEST) + b_ref[0]


def _modulation(cvec, w_mod, b_mod):
    L, D, N = w_mod.shape
    R = cvec.shape[0]
    tn = 1024 if N % 1024 == 0 else N
    return pl.pallas_call(
        _mod_kernel,
        grid=(L, N // tn),
        in_specs=[pl.BlockSpec((R, D), lambda l, j: (0, 0)),
                  pl.BlockSpec((1, D, tn), lambda l, j: (l, 0, j)),
                  pl.BlockSpec((1, 1, tn), lambda l, j: (l, 0, j))],
        out_specs=pl.BlockSpec((1, R, tn), lambda l, j: (l, 0, j)),
        out_shape=jax.ShapeDtypeStruct((L, R, N), F32),
        compiler_params=_cparams(("parallel", "parallel")),
        name="modulation",
    )(cvec, w_mod, b_mod.reshape(L, 1, N))


def _row_mod(mod_ref, lat_row, ctx_row, row0, tm, n_ctx):
    rows = row0 + lax.broadcasted_iota(jnp.int32, (tm, 1), 0)
    return jnp.where(rows < n_ctx, mod_ref[0, ctx_row:ctx_row + 1, :], mod_ref[0, lat_row:lat_row + 1, :])


def _tile_sources(x_refs, tm, n_ctx):
    if len(x_refs) == 1:
        return [(x_refs[0], None, tm)]
    return [(ref, x_refs[0] if i == 0 else None, n_ctx) for i, ref in enumerate(x_refs[1:])]


def _norm_proj_kernel(*refs, n_x, tm, n_ctx, n_lo_tiles):
    x_refs = refs[:n_x]
    mod_ref, g_ref, w_ref, o_ref = refs[n_x:n_x + 4]
    rest = refs[n_x + 4:]
    h_scr, gs_scr = rest[-2:]
    r = pl.program_id(1)
    j = pl.program_id(2)

    @pl.when(j == 0)
    def _():
        gs_scr[0:1, :] = g_ref[...] * (1.0 + mod_ref[0, 1:2, :])
        gs_scr[1:2, :] = mod_ref[0, 0:1, :]
        gs_scr[2:3, :] = g_ref[...] * (1.0 + mod_ref[0, 4:5, :])
        gs_scr[3:4, :] = mod_ref[0, 3:4, :]

        row0 = 0
        for ref, first_tile_ref, n_rows in _tile_sources(x_refs, tm, n_ctx):
            def slab(k, carry, ref=ref, first_tile_ref=first_tile_ref, row0=row0):
                r0 = pl.multiple_of(k * NORM_SLAB, NORM_SLAB)
                x = ref[0, pl.ds(r0, NORM_SLAB), :]
                if first_tile_ref is not None:
                    x = jnp.where(r == 0, first_tile_ref[0, pl.ds(r0, NORM_SLAB), :], x)
                base = jnp.where(r * tm + row0 + r0 < n_ctx, 2, 0)
                y = x * lax.rsqrt(jnp.mean(x * x, axis=-1, keepdims=True) + EPS)
                h_scr[pl.ds(row0 + r0, NORM_SLAB), :] = (y * gs_scr[pl.ds(base, 1), :]
                                                         + gs_scr[pl.ds(base + 1, 1), :]).astype(BF16)
                return carry

            lax.fori_loop(0, n_rows // NORM_SLAB, slab, 0, unroll=NORM_UNROLL)
            row0 += n_rows

    if len(rest) == 2:
        o_ref[0] = jnp.dot(h_scr[...], w_ref[...], preferred_element_type=F32).astype(o_ref.dtype)
    else:
        @pl.when(j < n_lo_tiles)
        def _():
            o_ref[0] = jnp.dot(h_scr[...], w_ref[...], preferred_element_type=F32).astype(o_ref.dtype)

        @pl.when(j >= n_lo_tiles)
        def _():
            rest[0][0] = jnp.dot(h_scr[...], w_ref[...], preferred_element_type=F32)


def _row_operands(x, tm, tile_of):
    if not isinstance(x, tuple):
        B, T, D = x.shape
        return [x], [pl.BlockSpec((1, tm, D), lambda *g: (*tile_of(*g), 0))], (B, T, D)
    ctx, lat = x
    B, n_ctx, D = ctx.shape
    assert tm % n_ctx == 0 and lat.shape[1] % n_ctx == 0
    q = tm // n_ctx

    def lat_block(i):
        def index(*g):
            b, r = tile_of(*g)
            return b, jnp.maximum(q * r - 1 + i, 0), 0
        return pl.BlockSpec((1, n_ctx, D), index)

    specs = [pl.BlockSpec((1, n_ctx, D), lambda *g: (tile_of(*g)[0], 0, 0))] + [lat_block(i) for i in range(q)]
    return [ctx] + [lat] * q, specs, (B, n_ctx + lat.shape[1], D)


def _norm_proj(x, modrows, g, w, n_f32, n_ctx, tm, tn, col_order=None):
    x_ops, x_specs, (B, T, D) = _row_operands(x, tm, lambda b, r, j: (b, r))
    n_lo = w.shape[1] - n_f32
    assert n_lo % tn == 0 and n_f32 % tn == 0
    lo_tiles, hi_tiles = n_lo // tn, n_f32 // tn
    if col_order is None:
        w_tile = lambda j: j
    else:
        assert sorted(col_order) == list(range(lo_tiles + hi_tiles))
        w_tile = lambda j: sum(jnp.where(j == k, src, 0) for k, src in enumerate(col_order))
    out_specs = [pl.BlockSpec((1, tm, tn), lambda b, r, j: (b, r, jnp.minimum(j, lo_tiles - 1)))]
    out_shape = [jax.ShapeDtypeStruct((B, T, n_lo), BF16)]
    if hi_tiles:
        out_specs.append(pl.BlockSpec((1, tm, tn), lambda b, r, j: (b, r, jnp.maximum(j - lo_tiles, 0))))
        out_shape.append(jax.ShapeDtypeStruct((B, T, n_f32), F32))
    outs = pl.pallas_call(
        functools.partial(_norm_proj_kernel, n_x=len(x_ops), tm=tm, n_ctx=n_ctx, n_lo_tiles=lo_tiles),
        grid=(B, T // tm, lo_tiles + hi_tiles),
        in_specs=x_specs + [
                  pl.BlockSpec((1, SUBLANES, D), lambda b, r, j: (b, 0, 0)),
                  pl.BlockSpec((1, D), lambda b, r, j: (0, 0)),
                  pl.BlockSpec((D, tn), lambda b, r, j: (0, w_tile(j)))],
        out_specs=out_specs,
        out_shape=out_shape,
        scratch_shapes=[pltpu.VMEM((tm, D), BF16), pltpu.VMEM((4, D), F32)],
        compiler_params=_cparams(("parallel", "parallel", "arbitrary")),
        name="norm_proj",
    )(*x_ops, modrows, g.reshape(1, D), w)
    return outs if hi_tiles else outs[0]


def _rope(x, cos, sin_signed):
    lane = lax.broadcasted_iota(jnp.int32, x.shape, 1)
    n = x.shape[1]
    partner = jnp.where(lane % 32 < 16, pltpu.roll(x, n - 16, axis=1), pltpu.roll(x, 16, axis=1))
    return x * cos + partner * sin_signed


def _attn_kernel(lam_ref, q_ref, k_ref, v_ref, gate_ref, cq_ref, sq_ref, ck_ref, sk_ref, g_ref,
                 o_ref, k_scr, vt_scr, *, tq, n_ctx, out_scale):
    i = pl.program_id(2)
    n_keys = k_scr.shape[0]

    @pl.when(i == 0)
    def _():
        k_scr[...] = _rope(k_ref[0].astype(F32), ck_ref[...], sk_ref[...]).astype(BF16)
        vt_scr[...] = v_ref[0].astype(F32).T.astype(BF16)

    qr = _rope(q_ref[0].astype(F32), cq_ref[...], sq_ref[...]) * (DA_D ** -0.5 * LOG2E)
    first_map = lax.broadcasted_iota(jnp.int32, qr.shape, 1) < DA_D
    q_maps = (jnp.where(first_map, qr, 0.0).astype(BF16), jnp.where(first_map, 0.0, qr).astype(BF16))
    lam = lam_ref[0]
    qs = ATTN_Q_SUB

    def attend(n_ctx_rows):
        def scores(r0):
            nk = n_ctx if r0 < n_ctx_rows else n_keys
            q2 = jnp.concatenate([q_maps[0][r0:r0 + qs], q_maps[1][r0:r0 + qs]], axis=0)
            return lax.dot_general(k_scr[:nk, :], q2, (((1,), (1,)), ((), ())),
                                   preferred_element_type=F32)

        starts = list(range(0, tq, qs))
        pending = [scores(r0) for r0 in starts[:ATTN_AHEAD]]
        for n, r0 in enumerate(starts):
            rows = slice(r0, r0 + qs)
            s = pending.pop(0)
            nk = s.shape[0]
            if n + ATTN_AHEAD < len(starts):
                pending.append(scores(starts[n + ATTN_AHEAD]))
            e = jnp.exp2(s - jnp.max(s, axis=0, keepdims=True))
            rinv = 1.0 / jnp.sum(e, axis=0, keepdims=True)
            ov = jnp.dot(vt_scr[:, :nk], e.astype(BF16), preferred_element_type=F32) * rinv
            o = (ov[:, :qs] - lam * ov[:, qs:]).T
            y = o * lax.rsqrt(jnp.mean(o * o, axis=-1, keepdims=True) + EPS) * g_ref[...] * out_scale
            gate = gate_ref[0, rows, :].astype(F32)
            o_ref[0, rows, :] = (y * (gate * _sigmoid(gate))).astype(o_ref.dtype)

    @pl.when(i == 0)
    def _():
        attend(n_ctx)

    @pl.when(i != 0)
    def _():
        attend(0)


def _diff_attention(p, lam, cos, sin, subln_g, out_scale, n_ctx):
    B, T, _ = p.shape
    tq = max(t for t in range(n_ctx, ATTN_Q_BLOCK + 1, n_ctx) if T % t == 0)
    assert n_ctx % ATTN_Q_SUB == 0
    H = DA_HEADS
    head = lambda part: (lambda b, h, i: (b, i, part * H + h))
    head_all = lambda part: (lambda b, h, i: (b, 0, part * H + h))
    return pl.pallas_call(
        functools.partial(_attn_kernel, tq=tq, n_ctx=n_ctx, out_scale=out_scale),
        grid=(B, H, T // tq),
        in_specs=[pl.BlockSpec(memory_space=pltpu.SMEM),
                  pl.BlockSpec((1, tq, HEAD_W), head(0)),
                  pl.BlockSpec((1, T, HEAD_W), head_all(1)),
                  pl.BlockSpec((1, T, HEAD_W), head_all(2)),
                  pl.BlockSpec((1, tq, HEAD_W), head(3)),
                  pl.BlockSpec((tq, HEAD_W), lambda b, h, i: (i, 0)),
                  pl.BlockSpec((tq, HEAD_W), lambda b, h, i: (i, 0)),
                  pl.BlockSpec((T, HEAD_W), lambda b, h, i: (0, 0)),
                  pl.BlockSpec((T, HEAD_W), lambda b, h, i: (0, 0)),
                  pl.BlockSpec((1, HEAD_W), lambda b, h, i: (0, 0))],
        out_specs=pl.BlockSpec((1, tq, HEAD_W), lambda b, h, i: (b, i, h)),
        out_shape=jax.ShapeDtypeStruct((B, T, H * HEAD_W), BF16),
        scratch_shapes=[pltpu.VMEM((T, HEAD_W), BF16), pltpu.VMEM((HEAD_W, T), BF16)],
        compiler_params=_cparams(("parallel", "parallel", "arbitrary")),
        name="diff_attention",
    )(lam, p, p, p, p, cos, sin, cos, sin, subln_g.reshape(1, HEAD_W))


def _rope_tables(n_ctx, seq):
    n_freq = DA_D // 4
    inv = np.power(np.float32(ROPE_BASE), -np.arange(n_freq, dtype=np.float32) / np.float32(n_freq))
    t = np.arange(seq)
    lane = np.arange(HEAD_W)
    pos = np.where((lane % DA_D < DA_D // 2)[None, :], (t // GRID_W)[:, None], (t % GRID_W)[:, None]).astype(np.float32)
    ang = (pos * inv[lane % n_freq][None, :]).astype(np.float64)
    sign = np.where(lane % 32 < 16, -1.0, 1.0)[None, :]
    cos = np.concatenate([np.ones((n_ctx, HEAD_W)), np.cos(ang)], axis=0).astype(np.float32)
    sin = np.concatenate([np.zeros((n_ctx, HEAD_W)), np.sin(ang) * sign], axis=0).astype(np.float32)
    return jnp.asarray(cos), jnp.asarray(sin)


def _chunk_cumsum(g, reverse):
    n = g.shape[0]
    row = lax.broadcasted_iota(jnp.int32, g.shape, 0)
    b = g
    sh = 1
    while sh < n:
        if reverse:
            b = b + jnp.where(row < n - sh, pltpu.roll(b, n - sh, axis=0), 0.0)
        else:
            b = b + jnp.where(row >= sh, pltpu.roll(b, sh, axis=0), 0.0)
        sh *= 2
    return b


def _diag_tiles(qq, b2, c2, reverse):
    half = SUBLANES
    sub = lax.broadcasted_iota(jnp.int32, (half, qq.shape[1]), 0)
    tiles = []
    for s in range(qq.shape[0]):
        sb = s // half
        rows = slice(sb * half, (sb + 1) * half)
        w = qq[rows] * jnp.exp2(b2[rows] - c2[s:s + 1])
        keep = (sub <= s - sb * half) if reverse else (sub >= s - sb * half)
        tiles.append((s, sb, jnp.where(keep, w, 0.0)))
    return tiles


def _off_diag_pieces(qq, b2, c2, reverse):
    C = qq.shape[0]
    pieces = []
    span = SUBLANES
    while span < C:
        for u in range(0, C, 2 * span):
            lo, hi = slice(u, u + span), slice(u + span, u + 2 * span)
            if reverse:
                q_rows, k_rows, ref = lo, hi, b2[u + span:u + span + 1]
            else:
                q_rows, k_rows, ref = hi, lo, b2[u + span - 1:u + span]
            pieces.append((q_rows, k_rows, qq[q_rows] * jnp.exp2(b2[q_rows] - ref), jnp.exp2(ref - c2[k_rows])))
        span *= 2
    return pieces


def _off_diag_rows(chunk):
    levels = int(math.log2(chunk // SUBLANES))
    return chunk * levels


def _lane_sums(tiles, ones_ref):
    n = len(tiles)
    assert n % 2 == 0
    lhs = jnp.concatenate([jnp.concatenate([tiles[2 * i], tiles[2 * i + 1]], axis=1) for i in range(n // 2)], axis=0)
    r = jnp.dot(lhs.astype(BF16), ones_ref[...], preferred_element_type=F32)
    w = tiles[0].shape[1]
    out = []
    for i in range(n // 2):
        rows = slice(i * SUBLANES, (i + 1) * SUBLANES)
        out += [r[rows, :w], r[rows, w:]]
    return out


def _hgrn_kernel(q_ref, v_ref, gate_ref, zf_ref, zb_ref, lb_ref, g_ref, o_ref,
                 qt_scr, oi_scr, ox_scr, dl_scr, u_scr, st_scr, ones_scr, blk_scr, *, n_ctx):
    T = q_ref.shape[1]
    C = HG_CHUNK
    K = HG_K
    nc = T // C
    nc_ctx = n_ctx // C
    same_half = (lax.broadcasted_iota(jnp.int32, (2 * K, 2 * K), 0) < K) == (lax.broadcasted_iota(jnp.int32, (2 * K, 2 * K), 1) < K)
    ones_scr[...] = jnp.where(same_half, 1.0, 0.0).astype(BF16)
    sizes = []
    span = SUBLANES
    while span < C:
        sizes += [span] * (C // (2 * span))
        span *= 2
    bounds = [sum((sizes * 2)[:i]) for i in range(1, 2 * len(sizes))]
    n_off = blk_scr.shape[0]
    row_i = lax.broadcasted_iota(jnp.int32, (n_off, n_off), 0)
    col_i = lax.broadcasted_iota(jnp.int32, (n_off, n_off), 1)
    piece_of_row = sum((row_i >= b).astype(jnp.int32) for b in bounds)
    piece_of_col = sum((col_i >= b).astype(jnp.int32) for b in bounds)
    blk_scr[...] = jnp.where(piece_of_row == piece_of_col, 1.0, 0.0)

    def stage1(c):
        r0 = pl.multiple_of(c * C, C)
        hq = q_ref[0, pl.ds(r0, C), :].astype(F32)
        qq = hq * _sigmoid(hq)
        vb = v_ref[0, pl.ds(r0, C), :]
        v = vb.astype(F32)
        tiles, keys, pieces = [], [], []
        for d, z_ref in enumerate((zf_ref, zb_ref)):
            reverse = d == 1
            z = z_ref[0, pl.ds(r0, C), :]
            lb = lb_ref[0, d:d + 1, :]
            e = jnp.exp(-jnp.abs(z))
            inv = 1.0 / (1.0 + e)
            sig = jnp.where(z >= 0, inv, e * inv)
            nsig = jnp.where(z >= 0, e * inv, inv)
            lk2 = jnp.log2(1.0 - lb) + jnp.log2(nsig)
            b2 = _chunk_cumsum(jnp.log2(lb + (1.0 - lb) * sig), reverse)
            b2_last = b2[0:1] if reverse else b2[C - 1:C]
            qt_scr[d, pl.ds(r0, C), :] = (qq * jnp.exp2(b2)).astype(BF16)
            keys.append(jnp.exp2(lk2 + (b2_last - b2)).astype(BF16))
            dl_scr[d, pl.ds(c, 1), :] = jnp.exp2(b2_last)
            tiles.append(_diag_tiles(qq, b2, b2 - lk2, reverse))
            pieces.append(_off_diag_pieces(qq, b2, b2 - lk2, reverse))
        sums = _lane_sums([w for per_dir in tiles for _, _, w in per_dir], ones_scr)
        q_off = jnp.concatenate([p[2] for per_dir in pieces for p in per_dir], axis=0).astype(BF16)
        k_off = jnp.concatenate([p[3] for per_dir in pieces for p in per_dir], axis=0).astype(BF16)
        cross = lax.dot_general(q_off, k_off, (((1,), (1,)), ((), ())), preferred_element_type=F32)
        upd = lax.dot_general(vb, jnp.concatenate(keys, axis=1), (((0,), (0,)), ((), ())),
                              preferred_element_type=F32)
        u_scr[0, c] = upd[:, :K]
        u_scr[1, c] = upd[:, K:]
        return r0, v, tiles, sums, pieces, cross

    def stage2(r0, v, tiles, sums, pieces, cross):
        v_off = jnp.concatenate([v[p[1]] for per_dir in pieces for p in per_dir], axis=0).astype(BF16)
        contrib = jnp.dot((cross * blk_scr[...]).astype(BF16), v_off, preferred_element_type=F32)
        k = 0
        off = 0
        for d in range(2):
            parts = [jnp.zeros((SUBLANES, v.shape[1]), F32) for _ in range(C // SUBLANES)]
            for s, blk, _ in tiles[d]:
                parts[blk] = parts[blk] + sums[k] * v[s:s + 1]
                k += 1
            for q_rows, _, qp, _ in pieces[d]:
                for i in range(qp.shape[0] // SUBLANES):
                    blk = q_rows.start // SUBLANES + i
                    parts[blk] = parts[blk] + contrib[off:off + SUBLANES]
                    off += SUBLANES
            oi_scr[d, pl.ds(r0, C), :] = jnp.concatenate(parts, axis=0)

    def prep(t, carry):
        staged = [stage1(t * HG_PREP_UNROLL + j) for j in range(HG_PREP_UNROLL)]
        for args in staged:
            stage2(*args)
        return carry

    assert nc % HG_PREP_UNROLL == 0
    lax.fori_loop(0, nc // HG_PREP_UNROLL, prep, 0)

    st_scr[...] = jnp.zeros_like(st_scr)

    def scan(m, carry):
        i0 = m * HG_GROUP
        cb0 = jnp.where(i0 < nc_ctx, nc_ctx - 1 - i0, nc + nc_ctx - 1 - i0)
        for j in range(HG_GROUP):
            for d, c in enumerate((i0 + j, cb0 - j)):
                r0 = pl.multiple_of(c * C, C)
                st = st_scr[d]
                ox_scr[d, pl.ds(r0, C), :] = jnp.dot(qt_scr[d, pl.ds(r0, C), :], st.T.astype(BF16),
                                                     preferred_element_type=F32)
                st_scr[d] = dl_scr[d, pl.ds(c, 1), :] * st + u_scr[d, c]
        return carry

    assert nc % HG_GROUP == 0 and nc_ctx % HG_GROUP == 0
    lax.fori_loop(0, nc // HG_GROUP, scan, 0)

    o = (oi_scr[0] + ox_scr[0]) + (oi_scr[1] + ox_scr[1])
    y = o * lax.rsqrt(jnp.mean(o * o, axis=-1, keepdims=True) + EPS) * g_ref[...]
    gate = gate_ref[0].astype(F32)
    o_ref[0] = (y * (gate * _sigmoid(gate))).astype(o_ref.dtype)


def _hgrn(p, pf, lb, norm_g, n_ctx, part0):
    B, T, _ = p.shape
    H = HG_HEADS
    nc = T // HG_CHUNK
    part = lambda k: (lambda b, h: (b, 0, k * H + h))
    blk = (1, T, HEAD_W)
    return pl.pallas_call(
        functools.partial(_hgrn_kernel, n_ctx=n_ctx),
        grid=(B, H),
        in_specs=[pl.BlockSpec(blk, part(part0)), pl.BlockSpec(blk, part(part0 + 1)),
                  pl.BlockSpec(blk, part(part0 + 2)),
                  pl.BlockSpec(blk, part(0)), pl.BlockSpec(blk, part(1)),
                  pl.BlockSpec((1, 2, HG_K), lambda b, h: (h, 0, 0)),
                  pl.BlockSpec((1, HEAD_W), lambda b, h: (0, 0))],
        out_specs=pl.BlockSpec(blk, lambda b, h: (b, 0, h)),
        out_shape=jax.ShapeDtypeStruct((B, T, H * HEAD_W), BF16),
        scratch_shapes=[pltpu.VMEM((2, T, HG_K), BF16),
                        pltpu.VMEM((2, T, HEAD_W), F32), pltpu.VMEM((2, T, HEAD_W), F32),
                        pltpu.VMEM((2, nc, HG_K), F32),
                        pltpu.VMEM((2, nc, HEAD_W, HG_K), F32),
                        pltpu.VMEM((2, HEAD_W, HG_K), F32),
                        pltpu.VMEM((2 * HG_K, 2 * HG_K), BF16),
                        pltpu.VMEM((_off_diag_rows(HG_CHUNK), _off_diag_rows(HG_CHUNK)), F32)],
        compiler_params=_cparams(("parallel", "parallel")),
        name="hgrn2",
    )(p, p, p, pf, pf, lb, norm_g.reshape(1, HEAD_W))


def _shift_rows(x, d):
    n = x.shape[0]
    row = lax.broadcasted_iota(jnp.int32, (n, 1), 0)
    rolled = pltpu.roll(x, (-d) % n, axis=0)
    keep = (row + d >= 0) & (row + d < n)
    return jnp.where(keep, rolled, 0.0)


def _pool_minus_identity(x, w):
    n = x.shape[0]
    ahead = w - w // 2
    behind = w // 2
    fwd = x
    span = 1
    while span < ahead:
        fwd = fwd + _shift_rows(fwd, span)
        span *= 2
    bwd = x
    span = 1
    while span < behind:
        bwd = bwd + _shift_rows(bwd, -span)
        span *= 2
    total = fwd + _shift_rows(bwd, -1)
    row = lax.broadcasted_iota(jnp.int32, (n, 1), 0)
    cnt = jnp.minimum(row + ahead, n) - jnp.maximum(row - behind, 0)
    return total / cnt.astype(F32) - x


def _pool_kernel(u_ref, z_ref, w_ref, ls_ref, o_ref, r_scr, *, n_ctx):
    j = pl.program_id(1)
    T = u_ref.shape[1]
    for jj, win in enumerate(POOL_WINDOWS):
        @pl.when(j == jj)
        def _(win=win):
            for lo, hi in ((0, n_ctx), (n_ctx, T)):
                r_scr[lo:hi, :] = _pool_minus_identity(u_ref[0, lo:hi, :].astype(F32), win).astype(BF16)

    sub = POOL_SUB if T % POOL_SUB == 0 else T
    project = lambda r0: jnp.dot(r_scr[r0:r0 + sub, :], w_ref[0], preferred_element_type=F32)
    y_next = project(0)
    for r0 in range(0, T, sub):
        y = y_next
        if r0 + sub < T:
            y_next = project(r0 + sub)
        z = z_ref[0, r0:r0 + sub, :].astype(F32)
        o_ref[0, r0:r0 + sub, :] = (y * ls_ref[...] * (z * _sigmoid(z))).astype(o_ref.dtype)


def _pool_mixer(p, w_pool, ls, n_ctx):
    B, T, two_e = p.shape
    G, gw, _ = w_pool.shape
    assert G == len(POOL_WINDOWS) and two_e == 2 * G * gw
    return pl.pallas_call(
        functools.partial(_pool_kernel, n_ctx=n_ctx),
        grid=(B, G),
        in_specs=[pl.BlockSpec((1, T, gw), lambda b, j: (b, 0, j)),
                  pl.BlockSpec((1, T, gw), lambda b, j: (b, 0, G + j)),
                  pl.BlockSpec((1, gw, gw), lambda b, j: (j, 0, 0)),
                  pl.BlockSpec((1, gw), lambda b, j: (0, j))],
        out_specs=pl.BlockSpec((1, T, gw), lambda b, j: (b, 0, j)),
        out_shape=jax.ShapeDtypeStruct((B, T, G * gw), BF16),
        scratch_shapes=[pltpu.VMEM((T, gw), BF16)],
        compiler_params=_cparams(("parallel", "parallel")),
        name="pool_mixer",
    )(p, p, w_pool, ls.reshape(1, G * gw))


def _out_proj_kernel(a_ref, b_ref, w_ref, *refs, n_x, tm, n_ctx, row_off):
    x_refs = refs[:n_x]
    mod_ref, g_ref, o_ref = refs[n_x:]
    ka = a_ref.shape[2]
    sub = OUT_SUB if tm % OUT_SUB == 0 else tm
    sources = _tile_sources(x_refs, tm, n_ctx)
    assert all(n_rows % sub == 0 for _, _, n_rows in sources)

    def residual(r0):
        for ref, first_tile_ref, n_rows in sources:
            if r0 < n_rows:
                x = ref[0, r0:r0 + sub, :]
                if first_tile_ref is not None:
                    x = jnp.where(pl.program_id(1) == 0, first_tile_ref[0, r0:r0 + sub, :], x)
                return x
            r0 -= n_rows

    def project(r0):
        y = jnp.dot(a_ref[0, r0:r0 + sub, :], w_ref[:ka, :], preferred_element_type=F32)
        return y + jnp.dot(b_ref[0, r0:r0 + sub, :], w_ref[ka:, :], preferred_element_type=F32)

    y_next = project(0)
    for r0 in range(0, tm, sub):
        y = y_next
        if r0 + sub < tm:
            y_next = project(r0 + sub)
        yn = y * lax.rsqrt(jnp.mean(y * y, axis=-1, keepdims=True) + EPS) * g_ref[...]
        gt = _row_mod(mod_ref, 2, 5, row_off + pl.program_id(1) * tm + r0, sub, n_ctx)
        o_ref[0, r0:r0 + sub, :] = residual(r0) + gt * yn


def _out_proj(a, b, a_blk, b_blk, w, x, modrows, g, n_ctx, tm, latents_only):
    off = n_ctx // tm if latents_only else 0
    x_ops, x_specs, (B, T, D) = _row_operands(x, tm, lambda bb, r: (bb, r + off))
    half_k = w.shape[0] // 2
    rows_out = T - n_ctx if latents_only else T
    return pl.pallas_call(
        functools.partial(_out_proj_kernel, n_x=len(x_ops), tm=tm, n_ctx=n_ctx, row_off=off * tm),
        grid=(B, rows_out // tm),
        in_specs=[pl.BlockSpec((1, tm, half_k), lambda bb, r: (bb, r + off, a_blk)),
                  pl.BlockSpec((1, tm, half_k), lambda bb, r: (bb, r + off, b_blk)),
                  pl.BlockSpec(w.shape, lambda bb, r: (0, 0), pipeline_mode=pl.Buffered(1))]
        + x_specs + [
                  pl.BlockSpec((1, SUBLANES, D), lambda bb, r: (bb, 0, 0)),
                  pl.BlockSpec((1, D), lambda bb, r: (0, 0))],
        out_specs=pl.BlockSpec((1, tm, D), lambda bb, r: (bb, r, 0)),
        out_shape=jax.ShapeDtypeStruct((B, rows_out, D), F32),
        compiler_params=_cparams(("parallel", "parallel")),
        name="out_proj",
    )(a, b, w, *x_ops, modrows, g.reshape(1, D))


def _row_tile(T, cap):
    best = NORM_SLAB
    for t in range(NORM_SLAB, min(T, cap) + 1, NORM_SLAB):
        if T % t == 0:
            best = t
    return best


def _col_tile(N, cap):
    best = 128
    for t in range(128, min(N, cap) + 1, 128):
        if N % t == 0:
            best = t
    return best


def kernel(x, c, ctx, c_ctx, w_mod, b_mod, g_pre, g_post, ev_w_in, ev_w_out, ev_lambda, ev_subln_g,
           ev_hg_lb_logits, ev_hg_norm_g, od_w_in, od_w_pool, od_scale, od_w_out):
    B, S, D = x.shape
    n_ctx = ctx.shape[1]
    depth = w_mod.shape[0]
    T = n_ctx + S
    W = DA_HEADS * HEAD_W

    xc = (ctx, x)
    tm_pair = max(t for t in range(n_ctx, 768 + 1, n_ctx) if T % t == 0)

    n_rows = -(-(B + 1) // SUBLANES) * SUBLANES
    cvec = jnp.zeros((n_rows, D), F32).at[:B].set(c).at[B].set(c_ctx)
    mod = _modulation(cvec, w_mod, b_mod)

    lb_cum = jnp.cumsum(jax.nn.softmax(ev_hg_lb_logits.astype(F32), axis=0), axis=0)
    lb_all = lb_cum - lb_cum[0]
    cos, sin = _rope_tables(n_ctx, S)

    tm_in = _row_tile(T, 1152)
    tm_out = _row_tile(T, 768)

    for l in range(depth):
        last = l == depth - 1
        tm_l, tm_o = (tm_pair, tm_pair) if isinstance(xc, tuple) else (tm_in, n_ctx if last else tm_out)
        ml = mod[l]
        lat = ml[:B].reshape(B, 3, D)
        cx = jnp.broadcast_to(ml[B].reshape(1, 3, D), (B, 3, D))
        modrows = jnp.concatenate([lat, cx, jnp.zeros((B, SUBLANES - 6, D), F32)], axis=1)

        if l % 2 == 0:
            e = l // 2
            assert _col_tile(W, 1024) == W
            p, pf = _norm_proj(xc, modrows, g_pre[l], ev_w_in[e].astype(BF16), 2 * W, n_ctx, tm_l, W,
                               col_order=(0, 1, 2, 3, 4, 7, 8, 5, 6))

            lam_init = 0.8 - 0.6 * math.exp(-0.3 * l)
            lv = ev_lambda[e].astype(F32)
            lam = jnp.exp(jnp.sum(lv[0] * lv[1])) - jnp.exp(jnp.sum(lv[2] * lv[3])) + lam_init
            a = _diff_attention(p, lam.reshape(1), cos, sin, ev_subln_g[e], 1.0 - lam_init, n_ctx)
            lb = lb_all[e].reshape(2, HG_HEADS, HG_K).transpose(1, 0, 2)
            bh = _hgrn(p, pf, lb, ev_hg_norm_g[e], n_ctx, 4)
            w_out = ev_w_out[e].astype(BF16)
            ya, yb, ia, ib = a, bh, 0, 0
        else:
            o = l // 2
            p = _norm_proj(xc, modrows, g_pre[l], od_w_in[o].astype(BF16), 0, n_ctx, tm_l,
                           _col_tile(od_w_in.shape[2], 1024))
            y = _pool_mixer(p, od_w_pool[o].astype(BF16), od_scale[o], n_ctx)
            w_out = od_w_out[o].astype(BF16)
            ya, yb, ia, ib = y, y, 0, 1

        xc = _out_proj(ya, yb, ia, ib, w_out, xc, modrows, g_post[l], n_ctx, tm_o, latents_only=last)
    return xc
```

```python
import functools
import math

import jax
import jax.numpy as jnp
import numpy as np
from jax import lax
from jax.experimental import pallas as pl
from jax.experimental.pallas import tpu as pltpu

DA_HEADS = 8
DA_D = 64
HEAD_W = 2 * DA_D
HG_HEADS = 8
HG_K = 128
GRID_W = 64
ROPE_BASE = 10000.0
EPS = 1e-6
POOL_WINDOWS = (2, 4, 8, 16)
HG_CHUNK = 32
LOG2E = math.log2(math.e)
ATTN_Q_BLOCK = 2304
ATTN_Q_SUB = 128
ATTN_AHEAD = 8
HG_GROUP = 8
HG_PREP_UNROLL = 8
SUBLANES = 8
POOL_SUB = 768
OUT_SUB = 256
NORM_SLAB = 16
NORM_UNROLL = 4

F32 = jnp.float32
BF16 = jnp.bfloat16
VMEM_LIMIT = 56 * 1024 * 1024


def _cparams(sem):
    return pltpu.CompilerParams(dimension_semantics=sem, vmem_limit_bytes=VMEM_LIMIT)


def _sigmoid(x):
    return 1.0 / (1.0 + jnp.exp(-x))


def _mod_kernel(c_ref, w_ref, b_ref, o_ref):
    c = c_ref[...]
    s = c * _sigmoid(c)
    o_ref[0] = jnp.dot(s, w_ref[0], preferred_element_type=F32,
                       precision=lax.Precision.HIGHEST) + b_ref[0]


def _modulation(cvec, w_mod, b_mod):
    L, D, N = w_mod.shape
    R = cvec.shape[0]
    tn = 1024 if N % 1024 == 0 else N
    return pl.pallas_call(
        _mod_kernel,
        grid=(L, N // tn),
        in_specs=[pl.BlockSpec((R, D), lambda l, j: (0, 0)),
                  pl.BlockSpec((1, D, tn), lambda l, j: (l, 0, j)),
                  pl.BlockSpec((1, 1, tn), lambda l, j: (l, 0, j))],
        out_specs=pl.BlockSpec((1, R, tn), lambda l, j: (l, 0, j)),
        out_shape=jax.ShapeDtypeStruct((L, R, N), F32),
        compiler_params=_cparams(("parallel", "parallel")),
        name="modulation",
    )(cvec, w_mod, b_mod.reshape(L, 1, N))


def _row_mod(mod_ref, lat_row, ctx_row, row0, tm, n_ctx):
    rows = row0 + lax.broadcasted_iota(jnp.int32, (tm, 1), 0)
    return jnp.where(rows < n_ctx, mod_ref[0, ctx_row:ctx_row + 1, :], mod_ref[0, lat_row:lat_row + 1, :])


def _tile_sources(x_refs, tm, n_ctx):
    if len(x_refs) == 1:
        return [(x_refs[0], None, tm)]
    return [(ref, x_refs[0] if i == 0 else None, n_ctx) for i, ref in enumerate(x_refs[1:])]


def _norm_proj_kernel(*refs, n_x, tm, n_ctx, n_lo_tiles):
    x_refs = refs[:n_x]
    mod_ref, g_ref, w_ref, o_ref = refs[n_x:n_x + 4]
    rest = refs[n_x + 4:]
    h_scr, gs_scr = rest[-2:]
    r = pl.program_id(1)
    j = pl.program_id(2)

    @pl.when(j == 0)
    def _():
        gs_scr[0:1, :] = g_ref[...] * (1.0 + mod_ref[0, 1:2, :])
        gs_scr[1:2, :] = mod_ref[0, 0:1, :]
        gs_scr[2:3, :] = g_ref[...] * (1.0 + mod_ref[0, 4:5, :])
        gs_scr[3:4, :] = mod_ref[0, 3:4, :]

        row0 = 0
        for ref, first_tile_ref, n_rows in _tile_sources(x_refs, tm, n_ctx):
            def slab(k, carry, ref=ref, first_tile_ref=first_tile_ref, row0=row0):
                r0 = pl.multiple_of(k * NORM_SLAB, NORM_SLAB)
                x = ref[0, pl.ds(r0, NORM_SLAB), :]
                if first_tile_ref is not None:
                    x = jnp.where(r == 0, first_tile_ref[0, pl.ds(r0, NORM_SLAB), :], x)
                base = jnp.where(r * tm + row0 + r0 < n_ctx, 2, 0)
                y = x * lax.rsqrt(jnp.mean(x * x, axis=-1, keepdims=True) + EPS)
                h_scr[pl.ds(row0 + r0, NORM_SLAB), :] = (y * gs_scr[pl.ds(base, 1), :]
                                                         + gs_scr[pl.ds(base + 1, 1), :]).astype(BF16)
                return carry

            lax.fori_loop(0, n_rows // NORM_SLAB, slab, 0, unroll=NORM_UNROLL)
            row0 += n_rows

    if len(rest) == 2:
        o_ref[0] = jnp.dot(h_scr[...], w_ref[...], preferred_element_type=F32).astype(o_ref.dtype)
    else:
        @pl.when(j < n_lo_tiles)
        def _():
            o_ref[0] = jnp.dot(h_scr[...], w_ref[...], preferred_element_type=F32).astype(o_ref.dtype)

        @pl.when(j >= n_lo_tiles)
        def _():
            rest[0][0] = jnp.dot(h_scr[...], w_ref[...], preferred_element_type=F32)


def _row_operands(x, tm, tile_of):
    if not isinstance(x, tuple):
        B, T, D = x.shape
        return [x], [pl.BlockSpec((1, tm, D), lambda *g: (*tile_of(*g), 0))], (B, T, D)
    ctx, lat = x
    B, n_ctx, D = ctx.shape
    assert tm % n_ctx == 0 and lat.shape[1] % n_ctx == 0
    q = tm // n_ctx

    def lat_block(i):
        def index(*g):
            b, r = tile_of(*g)
            return b, jnp.maximum(q * r - 1 + i, 0), 0
        return pl.BlockSpec((1, n_ctx, D), index)

    specs = [pl.BlockSpec((1, n_ctx, D), lambda *g: (tile_of(*g)[0], 0, 0))] + [lat_block(i) for i in range(q)]
    return [ctx] + [lat] * q, specs, (B, n_ctx + lat.shape[1], D)


def _norm_proj(x, modrows, g, w, n_f32, n_ctx, tm, tn, col_order=None):
    x_ops, x_specs, (B, T, D) = _row_operands(x, tm, lambda b, r, j: (b, r))
    n_lo = w.shape[1] - n_f32
    assert n_lo % tn == 0 and n_f32 % tn == 0
    lo_tiles, hi_tiles = n_lo // tn, n_f32 // tn
    if col_order is None:
        w_tile = lambda j: j
    else:
        assert sorted(col_order) == list(range(lo_tiles + hi_tiles))
        w_tile = lambda j: sum(jnp.where(j == k, src, 0) for k, src in enumerate(col_order))
    out_specs = [pl.BlockSpec((1, tm, tn), lambda b, r, j: (b, r, jnp.minimum(j, lo_tiles - 1)))]
    out_shape = [jax.ShapeDtypeStruct((B, T, n_lo), BF16)]
    if hi_tiles:
        out_specs.append(pl.BlockSpec((1, tm, tn), lambda b, r, j: (b, r, jnp.maximum(j - lo_tiles, 0))))
        out_shape.append(jax.ShapeDtypeStruct((B, T, n_f32), F32))
    outs = pl.pallas_call(
        functools.partial(_norm_proj_kernel, n_x=len(x_ops), tm=tm, n_ctx=n_ctx, n_lo_tiles=lo_tiles),
        grid=(B, T // tm, lo_tiles + hi_tiles),
        in_specs=x_specs + [
                  pl.BlockSpec((1, SUBLANES, D), lambda b, r, j: (b, 0, 0)),
                  pl.BlockSpec((1, D), lambda b, r, j: (0, 0)),
                  pl.BlockSpec((D, tn), lambda b, r, j: (0, w_tile(j)))],
        out_specs=out_specs,
        out_shape=out_shape,
        scratch_shapes=[pltpu.VMEM((tm, D), BF16), pltpu.VMEM((4, D), F32)],
        compiler_params=_cparams(("parallel", "parallel", "arbitrary")),
        name="norm_proj",
    )(*x_ops, modrows, g.reshape(1, D), w)
    return outs if hi_tiles else outs[0]


def _rope(x, cos, sin_signed):
    lane = lax.broadcasted_iota(jnp.int32, x.shape, 1)
    n = x.shape[1]
    partner = jnp.where(lane % 32 < 16, pltpu.roll(x, n - 16, axis=1), pltpu.roll(x, 16, axis=1))
    return x * cos + partner * sin_signed


def _attn_kernel(lam_ref, q_ref, k_ref, v_ref, gate_ref, cq_ref, sq_ref, ck_ref, sk_ref, g_ref,
                 o_ref, k_scr, vt_scr, *, tq, n_ctx, out_scale):
    i = pl.program_id(2)
    n_keys = k_scr.shape[0]

    @pl.when(i == 0)
    def _():
        k_scr[...] = _rope(k_ref[0].astype(F32), ck_ref[...], sk_ref[...]).astype(BF16)
        vt_scr[...] = v_ref[0].astype(F32).T.astype(BF16)

    qr = _rope(q_ref[0].astype(F32), cq_ref[...], sq_ref[...]) * (DA_D ** -0.5 * LOG2E)
    first_map = lax.broadcasted_iota(jnp.int32, qr.shape, 1) < DA_D
    q_maps = (jnp.where(first_map, qr, 0.0).astype(BF16), jnp.where(first_map, 0.0, qr).astype(BF16))
    lam = lam_ref[0]
    qs = ATTN_Q_SUB

    def attend(n_ctx_rows):
        def scores(r0):
            nk = n_ctx if r0 < n_ctx_rows else n_keys
            q2 = jnp.concatenate([q_maps[0][r0:r0 + qs], q_maps[1][r0:r0 + qs]], axis=0)
            return lax.dot_general(k_scr[:nk, :], q2, (((1,), (1,)), ((), ())),
                                   preferred_element_type=F32)

        starts = list(range(0, tq, qs))
        pending = [scores(r0) for r0 in starts[:ATTN_AHEAD]]
        for n, r0 in enumerate(starts):
            rows = slice(r0, r0 + qs)
            s = pending.pop(0)
            nk = s.shape[0]
            if n + ATTN_AHEAD < len(starts):
                pending.append(scores(starts[n + ATTN_AHEAD]))
            e = jnp.exp2(s - jnp.max(s, axis=0, keepdims=True))
            rinv = 1.0 / jnp.sum(e, axis=0, keepdims=True)
            ov = jnp.dot(vt_scr[:, :nk], e.astype(BF16), preferred_element_type=F32) * rinv
            o = (ov[:, :qs] - lam * ov[:, qs:]).T
            y = o * lax.rsqrt(jnp.mean(o * o, axis=-1, keepdims=True) + EPS) * g_ref[...] * out_scale
            gate = gate_ref[0, rows, :].astype(F32)
            o_ref[0, rows, :] = (y * (gate * _sigmoid(gate))).astype(o_ref.dtype)

    @pl.when(i == 0)
    def _():
        attend(n_ctx)

    @pl.when(i != 0)
    def _():
        attend(0)


def _diff_attention(p, lam, cos, sin, subln_g, out_scale, n_ctx):
    B, T, _ = p.shape
    tq = max(t for t in range(n_ctx, ATTN_Q_BLOCK + 1, n_ctx) if T % t == 0)
    assert n_ctx % ATTN_Q_SUB == 0
    H = DA_HEADS
    head = lambda part: (lambda b, h, i: (b, i, part * H + h))
    head_all = lambda part: (lambda b, h, i: (b, 0, part * H + h))
    return pl.pallas_call(
        functools.partial(_attn_kernel, tq=tq, n_ctx=n_ctx, out_scale=out_scale),
        grid=(B, H, T // tq),
        in_specs=[pl.BlockSpec(memory_space=pltpu.SMEM),
                  pl.BlockSpec((1, tq, HEAD_W), head(0)),
                  pl.BlockSpec((1, T, HEAD_W), head_all(1)),
                  pl.BlockSpec((1, T, HEAD_W), head_all(2)),
                  pl.BlockSpec((1, tq, HEAD_W), head(3)),
                  pl.BlockSpec((tq, HEAD_W), lambda b, h, i: (i, 0)),
                  pl.BlockSpec((tq, HEAD_W), lambda b, h, i: (i, 0)),
                  pl.BlockSpec((T, HEAD_W), lambda b, h, i: (0, 0)),
                  pl.BlockSpec((T, HEAD_W), lambda b, h, i: (0, 0)),
                  pl.BlockSpec((1, HEAD_W), lambda b, h, i: (0, 0))],
        out_specs=pl.BlockSpec((1, tq, HEAD_W), lambda b, h, i: (b, i, h)),
        out_shape=jax.ShapeDtypeStruct((B, T, H * HEAD_W), BF16),
        scratch_shapes=[pltpu.VMEM((T, HEAD_W), BF16), pltpu.VMEM((HEAD_W, T), BF16)],
        compiler_params=_cparams(("parallel", "parallel", "arbitrary")),
        name="diff_attention",
    )(lam, p, p, p, p, cos, sin, cos, sin, subln_g.reshape(1, HEAD_W))


def _rope_tables(n_ctx, seq):
    n_freq = DA_D // 4
    inv = np.power(np.float32(ROPE_BASE), -np.arange(n_freq, dtype=np.float32) / np.float32(n_freq))
    t = np.arange(seq)
    lane = np.arange(HEAD_W)
    pos = np.where((lane % DA_D < DA_D // 2)[None, :], (t // GRID_W)[:, None], (t % GRID_W)[:, None]).astype(np.float32)
    ang = (pos * inv[lane % n_freq][None, :]).astype(np.float64)
    sign = np.where(lane % 32 < 16, -1.0, 1.0)[None, :]
    cos = np.concatenate([np.ones((n_ctx, HEAD_W)), np.cos(ang)], axis=0).astype(np.float32)
    sin = np.concatenate([np.zeros((n_ctx, HEAD_W)), np.sin(ang) * sign], axis=0).astype(np.float32)
    return jnp.asarray(cos), jnp.asarray(sin)


def _chunk_cumsum(g, reverse):
    n = g.shape[0]
    row = lax.broadcasted_iota(jnp.int32, g.shape, 0)
    b = g
    sh = 1
    while sh < n:
        if reverse:
            b = b + jnp.where(row < n - sh, pltpu.roll(b, n - sh, axis=0), 0.0)
        else:
            b = b + jnp.where(row >= sh, pltpu.roll(b, sh, axis=0), 0.0)
        sh *= 2
    return b


def _diag_tiles(qq, b2, c2, reverse):
    half = SUBLANES
    sub = lax.broadcasted_iota(jnp.int32, (half, qq.shape[1]), 0)
    tiles = []
    for s in range(qq.shape[0]):
        sb = s // half
        rows = slice(sb * half, (sb + 1) * half)
        w = qq[rows] * jnp.exp2(b2[rows] - c2[s:s + 1])
        keep = (sub <= s - sb * half) if reverse else (sub >= s - sb * half)
        tiles.append((s, sb, jnp.where(keep, w, 0.0)))
    return tiles


def _off_diag_pieces(qq, b2, c2, reverse):
    C = qq.shape[0]
    pieces = []
    span = SUBLANES
    while span < C:
        for u in range(0, C, 2 * span):
            lo, hi = slice(u, u + span), slice(u + span, u + 2 * span)
            if reverse:
                q_rows, k_rows, ref = lo, hi, b2[u + span:u + span + 1]
            else:
                q_rows, k_rows, ref = hi, lo, b2[u + span - 1:u + span]
            pieces.append((q_rows, k_rows, qq[q_rows] * jnp.exp2(b2[q_rows] - ref), jnp.exp2(ref - c2[k_rows])))
        span *= 2
    return pieces


def _off_diag_rows(chunk):
    levels = int(math.log2(chunk // SUBLANES))
    return chunk * levels


def _lane_sums(tiles, ones_ref):
    n = len(tiles)
    assert n % 2 == 0
    lhs = jnp.concatenate([jnp.concatenate([tiles[2 * i], tiles[2 * i + 1]], axis=1) for i in range(n // 2)], axis=0)
    r = jnp.dot(lhs.astype(BF16), ones_ref[...], preferred_element_type=F32)
    w = tiles[0].shape[1]
    out = []
    for i in range(n // 2):
        rows = slice(i * SUBLANES, (i + 1) * SUBLANES)
        out += [r[rows, :w], r[rows, w:]]
    return out


def _hgrn_kernel(q_ref, v_ref, gate_ref, zf_ref, zb_ref, lb_ref, g_ref, o_ref,
                 qt_scr, oi_scr, ox_scr, dl_scr, u_scr, st_scr, ones_scr, blk_scr, *, n_ctx):
    T = q_ref.shape[1]
    C = HG_CHUNK
    K = HG_K
    nc = T // C
    nc_ctx = n_ctx // C
    same_half = (lax.broadcasted_iota(jnp.int32, (2 * K, 2 * K), 0) < K) == (lax.broadcasted_iota(jnp.int32, (2 * K, 2 * K), 1) < K)
    ones_scr[...] = jnp.where(same_half, 1.0, 0.0).astype(BF16)
    sizes = []
    span = SUBLANES
    while span < C:
        sizes += [span] * (C // (2 * span))
        span *= 2
    bounds = [sum((sizes * 2)[:i]) for i in range(1, 2 * len(sizes))]
    n_off = blk_scr.shape[0]
    row_i = lax.broadcasted_iota(jnp.int32, (n_off, n_off), 0)
    col_i = lax.broadcasted_iota(jnp.int32, (n_off, n_off), 1)
    piece_of_row = sum((row_i >= b).astype(jnp.int32) for b in bounds)
    piece_of_col = sum((col_i >= b).astype(jnp.int32) for b in bounds)
    blk_scr[...] = jnp.where(piece_of_row == piece_of_col, 1.0, 0.0)

    def stage1(c):
        r0 = pl.multiple_of(c * C, C)
        hq = q_ref[0, pl.ds(r0, C), :].astype(F32)
        qq = hq * _sigmoid(hq)
        vb = v_ref[0, pl.ds(r0, C), :]
        v = vb.astype(F32)
        tiles, keys, pieces = [], [], []
        for d, z_ref in enumerate((zf_ref, zb_ref)):
            reverse = d == 1
            z = z_ref[0, pl.ds(r0, C), :]
            lb = lb_ref[0, d:d + 1, :]
            e = jnp.exp(-jnp.abs(z))
            inv = 1.0 / (1.0 + e)
            sig = jnp.where(z >= 0, inv, e * inv)
            nsig = jnp.where(z >= 0, e * inv, inv)
            lk2 = jnp.log2(1.0 - lb) + jnp.log2(nsig)
            b2 = _chunk_cumsum(jnp.log2(lb + (1.0 - lb) * sig), reverse)
            b2_last = b2[0:1] if reverse else b2[C - 1:C]
            qt_scr[d, pl.ds(r0, C), :] = (qq * jnp.exp2(b2)).astype(BF16)
            keys.append(jnp.exp2(lk2 + (b2_last - b2)).astype(BF16))
            dl_scr[d, pl.ds(c, 1), :] = jnp.exp2(b2_last)
            tiles.append(_diag_tiles(qq, b2, b2 - lk2, reverse))
            pieces.append(_off_diag_pieces(qq, b2, b2 - lk2, reverse))
        sums = _lane_sums([w for per_dir in tiles for _, _, w in per_dir], ones_scr)
        q_off = jnp.concatenate([p[2] for per_dir in pieces for p in per_dir], axis=0).astype(BF16)
        k_off = jnp.concatenate([p[3] for per_dir in pieces for p in per_dir], axis=0).astype(BF16)
        cross = lax.dot_general(q_off, k_off, (((1,), (1,)), ((), ())), preferred_element_type=F32)
        upd = lax.dot_general(vb, jnp.concatenate(keys, axis=1), (((0,), (0,)), ((), ())),
                              preferred_element_type=F32)
        u_scr[0, c] = upd[:, :K]
        u_scr[1, c] = upd[:, K:]
        return r0, v, tiles, sums, pieces, cross

    def stage2(r0, v, tiles, sums, pieces, cross):
        v_off = jnp.concatenate([v[p[1]] for per_dir in pieces for p in per_dir], axis=0).astype(BF16)
        contrib = jnp.dot((cross * blk_scr[...]).astype(BF16), v_off, preferred_element_type=F32)
        k = 0
        off = 0
        for d in range(2):
            parts = [jnp.zeros((SUBLANES, v.shape[1]), F32) for _ in range(C // SUBLANES)]
            for s, blk, _ in tiles[d]:
                parts[blk] = parts[blk] + sums[k] * v[s:s + 1]
                k += 1
            for q_rows, _, qp, _ in pieces[d]:
                for i in range(qp.shape[0] // SUBLANES):
                    blk = q_rows.start // SUBLANES + i
                    parts[blk] = parts[blk] + contrib[off:off + SUBLANES]
                    off += SUBLANES
            oi_scr[d, pl.ds(r0, C), :] = jnp.concatenate(parts, axis=0)

    def prep(t, carry):
        staged = [stage1(t * HG_PREP_UNROLL + j) for j in range(HG_PREP_UNROLL)]
        for args in staged:
            stage2(*args)
        return carry

    assert nc % HG_PREP_UNROLL == 0
    lax.fori_loop(0, nc // HG_PREP_UNROLL, prep, 0)

    st_scr[...] = jnp.zeros_like(st_scr)

    def scan(m, carry):
        i0 = m * HG_GROUP
        cb0 = jnp.where(i0 < nc_ctx, nc_ctx - 1 - i0, nc + nc_ctx - 1 - i0)
        for j in range(HG_GROUP):
            for d, c in enumerate((i0 + j, cb0 - j)):
                r0 = pl.multiple_of(c * C, C)
                st = st_scr[d]
                ox_scr[d, pl.ds(r0, C), :] = jnp.dot(qt_scr[d, pl.ds(r0, C), :], st.T.astype(BF16),
                                                     preferred_element_type=F32)
                st_scr[d] = dl_scr[d, pl.ds(c, 1), :] * st + u_scr[d, c]
        return carry

    assert nc % HG_GROUP == 0 and nc_ctx % HG_GROUP == 0
    lax.fori_loop(0, nc // HG_GROUP, scan, 0)

    o = (oi_scr[0] + ox_scr[0]) + (oi_scr[1] + ox_scr[1])
    y = o * lax.rsqrt(jnp.mean(o * o, axis=-1, keepdims=True) + EPS) * g_ref[...]
    gate = gate_ref[0].astype(F32)
    o_ref[0] = (y * (gate * _sigmoid(gate))).astype(o_ref.dtype)


def _hgrn(p, pf, lb, norm_g, n_ctx, part0):
    B, T, _ = p.shape
    H = HG_HEADS
    nc = T // HG_CHUNK
    part = lambda k: (lambda b, h: (b, 0, k * H + h))
    blk = (1, T, HEAD_W)
    return pl.pallas_call(
        functools.partial(_hgrn_kernel, n_ctx=n_ctx),
        grid=(B, H),
        in_specs=[pl.BlockSpec(blk, part(part0)), pl.BlockSpec(blk, part(part0 + 1)),
                  pl.BlockSpec(blk, part(part0 + 2)),
                  pl.BlockSpec(blk, part(0)), pl.BlockSpec(blk, part(1)),
                  pl.BlockSpec((1, 2, HG_K), lambda b, h: (h, 0, 0)),
                  pl.BlockSpec((1, HEAD_W), lambda b, h: (0, 0))],
        out_specs=pl.BlockSpec(blk, lambda b, h: (b, 0, h)),
        out_shape=jax.ShapeDtypeStruct((B, T, H * HEAD_W), BF16),
        scratch_shapes=[pltpu.VMEM((2, T, HG_K), BF16),
                        pltpu.VMEM((2, T, HEAD_W), F32), pltpu.VMEM((2, T, HEAD_W), F32),
                        pltpu.VMEM((2, nc, HG_K), F32),
                        pltpu.VMEM((2, nc, HEAD_W, HG_K), F32),
                        pltpu.VMEM((2, HEAD_W, HG_K), F32),
                        pltpu.VMEM((2 * HG_K, 2 * HG_K), BF16),
                        pltpu.VMEM((_off_diag_rows(HG_CHUNK), _off_diag_rows(HG_CHUNK)), F32)],
        compiler_params=_cparams(("parallel", "parallel")),
        name="hgrn2",
    )(p, p, p, pf, pf, lb, norm_g.reshape(1, HEAD_W))


def _shift_rows(x, d):
    n = x.shape[0]
    row = lax.broadcasted_iota(jnp.int32, (n, 1), 0)
    rolled = pltpu.roll(x, (-d) % n, axis=0)
    keep = (row + d >= 0) & (row + d < n)
    return jnp.where(keep, rolled, 0.0)


def _pool_minus_identity(x, w):
    n = x.shape[0]
    ahead = w - w // 2
    behind = w // 2
    fwd = x
    span = 1
    while span < ahead:
        fwd = fwd + _shift_rows(fwd, span)
        span *= 2
    bwd = x
    span = 1
    while span < behind:
        bwd = bwd + _shift_rows(bwd, -span)
        span *= 2
    total = fwd + _shift_rows(bwd, -1)
    row = lax.broadcasted_iota(jnp.int32, (n, 1), 0)
    cnt = jnp.minimum(row + ahead, n) - jnp.maximum(row - behind, 0)
    return total / cnt.astype(F32) - x


def _pool_kernel(u_ref, z_ref, w_ref, ls_ref, o_ref, r_scr, *, n_ctx):
    j = pl.program_id(1)
    T = u_ref.shape[1]
    for jj, win in enumerate(POOL_WINDOWS):
        @pl.when(j == jj)
        def _(win=win):
            for lo, hi in ((0, n_ctx), (n_ctx, T)):
                r_scr[lo:hi, :] = _pool_minus_identity(u_ref[0, lo:hi, :].astype(F32), win).astype(BF16)

    sub = POOL_SUB if T % POOL_SUB == 0 else T
    project = lambda r0: jnp.dot(r_scr[r0:r0 + sub, :], w_ref[0], preferred_element_type=F32)
    y_next = project(0)
    for r0 in range(0, T, sub):
        y = y_next
        if r0 + sub < T:
            y_next = project(r0 + sub)
        z = z_ref[0, r0:r0 + sub, :].astype(F32)
        o_ref[0, r0:r0 + sub, :] = (y * ls_ref[...] * (z * _sigmoid(z))).astype(o_ref.dtype)


def _pool_mixer(p, w_pool, ls, n_ctx):
    B, T, two_e = p.shape
    G, gw, _ = w_pool.shape
    assert G == len(POOL_WINDOWS) and two_e == 2 * G * gw
    return pl.pallas_call(
        functools.partial(_pool_kernel, n_ctx=n_ctx),
        grid=(B, G),
        in_specs=[pl.BlockSpec((1, T, gw), lambda b, j: (b, 0, j)),
                  pl.BlockSpec((1, T, gw), lambda b, j: (b, 0, G + j)),
                  pl.BlockSpec((1, gw, gw), lambda b, j: (j, 0, 0)),
                  pl.BlockSpec((1, gw), lambda b, j: (0, j))],
        out_specs=pl.BlockSpec((1, T, gw), lambda b, j: (b, 0, j)),
        out_shape=jax.ShapeDtypeStruct((B, T, G * gw), BF16),
        scratch_shapes=[pltpu.VMEM((T, gw), BF16)],
        compiler_params=_cparams(("parallel", "parallel")),
        name="pool_mixer",
    )(p, p, w_pool, ls.reshape(1, G * gw))


def _out_proj_kernel(a_ref, b_ref, w_ref, *refs, n_x, tm, n_ctx, row_off):
    x_refs = refs[:n_x]
    mod_ref, g_ref, o_ref = refs[n_x:]
    ka = a_ref.shape[2]
    sub = OUT_SUB if tm % OUT_SUB == 0 else tm
    sources = _tile_sources(x_refs, tm, n_ctx)
    assert all(n_rows % sub == 0 for _, _, n_rows in sources)

    def residual(r0):
        for ref, first_tile_ref, n_rows in sources:
            if r0 < n_rows:
                x = ref[0, r0:r0 + sub, :]
                if first_tile_ref is not None:
                    x = jnp.where(pl.program_id(1) == 0, first_tile_ref[0, r0:r0 + sub, :], x)
                return x
            r0 -= n_rows

    def project(r0):
        y = jnp.dot(a_ref[0, r0:r0 + sub, :], w_ref[:ka, :], preferred_element_type=F32)
        return y + jnp.dot(b_ref[0, r0:r0 + sub, :], w_ref[ka:, :], preferred_element_type=F32)

    y_next = project(0)
    for r0 in range(0, tm, sub):
        y = y_next
        if r0 + sub < tm:
            y_next = project(r0 + sub)
        yn = y * lax.rsqrt(jnp.mean(y * y, axis=-1, keepdims=True) + EPS) * g_ref[...]
        gt = _row_mod(mod_ref, 2, 5, row_off + pl.program_id(1) * tm + r0, sub, n_ctx)
        o_ref[0, r0:r0 + sub, :] = residual(r0) + gt * yn


def _out_proj(a, b, a_blk, b_blk, w, x, modrows, g, n_ctx, tm, latents_only):
    off = n_ctx // tm if latents_only else 0
    x_ops, x_specs, (B, T, D) = _row_operands(x, tm, lambda bb, r: (bb, r + off))
    half_k = w.shape[0] // 2
    rows_out = T - n_ctx if latents_only else T
    return pl.pallas_call(
        functools.partial(_out_proj_kernel, n_x=len(x_ops), tm=tm, n_ctx=n_ctx, row_off=off * tm),
        grid=(B, rows_out // tm),
        in_specs=[pl.BlockSpec((1, tm, half_k), lambda bb, r: (bb, r + off, a_blk)),
                  pl.BlockSpec((1, tm, half_k), lambda bb, r: (bb, r + off, b_blk)),
                  pl.BlockSpec(w.shape, lambda bb, r: (0, 0), pipeline_mode=pl.Buffered(1))]
        + x_specs + [
                  pl.BlockSpec((1, SUBLANES, D), lambda bb, r: (bb, 0, 0)),
                  pl.BlockSpec((1, D), lambda bb, r: (0, 0))],
        out_specs=pl.BlockSpec((1, tm, D), lambda bb, r: (bb, r, 0)),
        out_shape=jax.ShapeDtypeStruct((B, rows_out, D), F32),
        compiler_params=_cparams(("parallel", "parallel")),
        name="out_proj",
    )(a, b, w, *x_ops, modrows, g.reshape(1, D))


def _row_tile(T, cap):
    best = NORM_SLAB
    for t in range(NORM_SLAB, min(T, cap) + 1, NORM_SLAB):
        if T % t == 0:
            best = t
    return best


def _col_tile(N, cap):
    best = 128
    for t in range(128, min(N, cap) + 1, 128):
        if N % t == 0:
            best = t
    return best


def kernel(x, c, ctx, c_ctx, w_mod, b_mod, g_pre, g_post, ev_w_in, ev_w_out, ev_lambda, ev_subln_g,
           ev_hg_lb_logits, ev_hg_norm_g, od_w_in, od_w_pool, od_scale, od_w_out):
    B, S, D = x.shape
    n_ctx = ctx.shape[1]
    depth = w_mod.shape[0]
    T = n_ctx + S
    W = DA_HEADS * HEAD_W

    xc = (ctx, x)
    tm_pair = max(t for t in range(n_ctx, 768 + 1, n_ctx) if T % t == 0)

    n_rows = -(-(B + 1) // SUBLANES) * SUBLANES
    cvec = jnp.zeros((n_rows, D), F32).at[:B].set(c).at[B].set(c_ctx)
    mod = _modulation(cvec, w_mod, b_mod)

    lb_cum = jnp.cumsum(jax.nn.softmax(ev_hg_lb_logits.astype(F32), axis=0), axis=0)
    lb_all = lb_cum - lb_cum[0]
    cos, sin = _rope_tables(n_ctx, S)

    tm_in = _row_tile(T, 1152)
    tm_out = _row_tile(T, 768)

    for l in range(depth):
        last = l == depth - 1
        tm_l, tm_o = (tm_pair, tm_pair) if isinstance(xc, tuple) else (tm_in, n_ctx if last else tm_out)
        ml = mod[l]
        lat = ml[:B].reshape(B, 3, D)
        cx = jnp.broadcast_to(ml[B].reshape(1, 3, D), (B, 3, D))
        modrows = jnp.concatenate([lat, cx, jnp.zeros((B, SUBLANES - 6, D), F32)], axis=1)

        if l % 2 == 0:
            e = l // 2
            assert _col_tile(W, 1024) == W
            p, pf = _norm_proj(xc, modrows, g_pre[l], ev_w_in[e].astype(BF16), 2 * W, n_ctx, tm_l, W,
                               col_order=(0, 1, 2, 3, 4, 7, 8, 5, 6))

            lam_init = 0.8 - 0.6 * math.exp(-0.3 * l)
            lv = ev_lambda[e].astype(F32)
            lam = jnp.exp(jnp.sum(lv[0] * lv[1])) - jnp.exp(jnp.sum(lv[2] * lv[3])) + lam_init
            a = _diff_attention(p, lam.reshape(1), cos, sin, ev_subln_g[e], 1.0 - lam_init, n_ctx)
            lb = lb_all[e].reshape(2, HG_HEADS, HG_K).transpose(1, 0, 2)
            bh = _hgrn(p, pf, lb, ev_hg_norm_g[e], n_ctx, 4)
            w_out = ev_w_out[e].astype(BF16)
            ya, yb, ia, ib = a, bh, 0, 0
        else:
            o = l // 2
            p = _norm_proj(xc, modrows, g_pre[l], od_w_in[o].astype(BF16), 0, n_ctx, tm_l,
                           _col_tile(od_w_in.shape[2], 2048))
            y = _pool_mixer(p, od_w_pool[o].astype(BF16), od_scale[o], n_ctx)
            w_out = od_w_out[o].astype(BF16)
            ya, yb, ia, ib = y, y, 0, 1

        xc = _out_proj(ya, yb, ia, ib, w_out, xc, modrows, g_post[l], n_ctx, tm_o, latents_only=last)
    return xc
```

```python
import functools
import math

import jax
import jax.numpy as jnp
import numpy as np
from jax import lax
from jax.experimental import pallas as pl
from jax.experimental.pallas import tpu as pltpu

DA_HEADS = 8
DA_D = 64
HEAD_W = 2 * DA_D
HG_HEADS = 8
HG_K = 128
GRID_W = 64
ROPE_BASE = 10000.0
EPS = 1e-6
POOL_WINDOWS = (2, 4, 8, 16)
HG_CHUNK = 32
LOG2E = math.log2(math.e)
ATTN_Q_BLOCK = 2304
ATTN_Q_SUB = 128
ATTN_AHEAD = 8
HG_GROUP = 8
HG_PREP_UNROLL = 8
SUBLANES = 8
POOL_SUB = 768
OUT_SUB = 256
NORM_SLAB = 16
NORM_UNROLL = 4

F32 = jnp.float32
BF16 = jnp.bfloat16
VMEM_LIMIT = 56 * 1024 * 1024


def _cparams(sem):
    return pltpu.CompilerParams(dimension_semantics=sem, vmem_limit_bytes=VMEM_LIMIT)


def _sigmoid(x):
    return 1.0 / (1.0 + jnp.exp(-x))


def _mod_kernel(c_ref, w_ref, b_ref, o_ref):
    c = c_ref[...]
    s = c * _sigmoid(c)
    o_ref[0] = jnp.dot(s, w_ref[0], preferred_element_type=F32,
                       precision=lax.Precision.HIGHEST) + b_ref[0]


def _modulation(cvec, w_mod, b_mod):
    L, D, N = w_mod.shape
    R = cvec.shape[0]
    tn = 1024 if N % 1024 == 0 else N
    return pl.pallas_call(
        _mod_kernel,
        grid=(L, N // tn),
        in_specs=[pl.BlockSpec((R, D), lambda l, j: (0, 0)),
                  pl.BlockSpec((1, D, tn), lambda l, j: (l, 0, j)),
                  pl.BlockSpec((1, 1, tn), lambda l, j: (l, 0, j))],
        out_specs=pl.BlockSpec((1, R, tn), lambda l, j: (l, 0, j)),
        out_shape=jax.ShapeDtypeStruct((L, R, N), F32),
        compiler_params=_cparams(("parallel", "parallel")),
        name="modulation",
    )(cvec, w_mod, b_mod.reshape(L, 1, N))


def _row_mod(mod_ref, lat_row, ctx_row, row0, tm, n_ctx):
    rows = row0 + lax.broadcasted_iota(jnp.int32, (tm, 1), 0)
    return jnp.where(rows < n_ctx, mod_ref[0, ctx_row:ctx_row + 1, :], mod_ref[0, lat_row:lat_row + 1, :])


def _tile_sources(x_refs, tm, n_ctx):
    if len(x_refs) == 1:
        return [(x_refs[0], None, tm)]
    src_rows = x_refs[0].shape[1]
    n_ctx_src = n_ctx // src_rows
    return [(ref, x_refs[i] if i < n_ctx_src else None, src_rows) for i, ref in enumerate(x_refs[n_ctx_src:])]


def _norm_proj_kernel(*refs, n_x, tm, n_ctx, n_lo_tiles):
    x_refs = refs[:n_x]
    mod_ref, g_ref, w_ref, o_ref = refs[n_x:n_x + 4]
    rest = refs[n_x + 4:]
    h_scr, gs_scr = rest[-2:]
    r = pl.program_id(1)
    j = pl.program_id(2)

    @pl.when(j == 0)
    def _():
        gs_scr[0:1, :] = g_ref[...] * (1.0 + mod_ref[0, 1:2, :])
        gs_scr[1:2, :] = mod_ref[0, 0:1, :]
        gs_scr[2:3, :] = g_ref[...] * (1.0 + mod_ref[0, 4:5, :])
        gs_scr[3:4, :] = mod_ref[0, 3:4, :]

        row0 = 0
        for ref, first_tile_ref, n_rows in _tile_sources(x_refs, tm, n_ctx):
            def slab(k, carry, ref=ref, first_tile_ref=first_tile_ref, row0=row0):
                r0 = pl.multiple_of(k * NORM_SLAB, NORM_SLAB)
                x = ref[0, pl.ds(r0, NORM_SLAB), :]
                if first_tile_ref is not None:
                    x = jnp.where(r == 0, first_tile_ref[0, pl.ds(r0, NORM_SLAB), :], x)
                base = jnp.where(r * tm + row0 + r0 < n_ctx, 2, 0)
                y = x * lax.rsqrt(jnp.mean(x * x, axis=-1, keepdims=True) + EPS)
                h_scr[pl.ds(row0 + r0, NORM_SLAB), :] = (y * gs_scr[pl.ds(base, 1), :]
                                                         + gs_scr[pl.ds(base + 1, 1), :]).astype(BF16)
                return carry

            lax.fori_loop(0, n_rows // NORM_SLAB, slab, 0, unroll=NORM_UNROLL)
            row0 += n_rows

    if len(rest) == 2:
        o_ref[0] = jnp.dot(h_scr[...], w_ref[...], preferred_element_type=F32).astype(o_ref.dtype)
    else:
        @pl.when(j < n_lo_tiles)
        def _():
            o_ref[0] = jnp.dot(h_scr[...], w_ref[...], preferred_element_type=F32).astype(o_ref.dtype)

        @pl.when(j >= n_lo_tiles)
        def _():
            rest[0][0] = jnp.dot(h_scr[...], w_ref[...], preferred_element_type=F32)


def _row_operands(x, tm, tile_of, src_rows):
    if not isinstance(x, tuple):
        B, T, D = x.shape
        return [x], [pl.BlockSpec((1, tm, D), lambda *g: (*tile_of(*g), 0))], (B, T, D)
    ctx, lat = x
    B, n_ctx, D = ctx.shape
    assert tm % src_rows == 0 and n_ctx % src_rows == 0 and lat.shape[1] % src_rows == 0 and tm >= n_ctx
    q = tm // src_rows
    n_ctx_src = n_ctx // src_rows

    def ctx_block(i):
        return pl.BlockSpec((1, src_rows, D), lambda *g: (tile_of(*g)[0], i, 0))

    def lat_block(i):
        def index(*g):
            b, r = tile_of(*g)
            return b, jnp.maximum(q * r + i - n_ctx_src, 0), 0
        return pl.BlockSpec((1, src_rows, D), index)

    specs = [ctx_block(i) for i in range(n_ctx_src)] + [lat_block(i) for i in range(q)]
    return [ctx] * n_ctx_src + [lat] * q, specs, (B, n_ctx + lat.shape[1], D)


def _norm_proj(x, modrows, g, w, n_f32, n_ctx, tm, tn, col_order=None):
    x_ops, x_specs, (B, T, D) = _row_operands(x, tm, lambda b, r, j: (b, r), math.gcd(tm, n_ctx))
    n_lo = w.shape[1] - n_f32
    assert n_lo % tn == 0 and n_f32 % tn == 0
    lo_tiles, hi_tiles = n_lo // tn, n_f32 // tn
    if col_order is None:
        w_tile = lambda j: j
    else:
        assert sorted(col_order) == list(range(lo_tiles + hi_tiles))
        w_tile = lambda j: sum(jnp.where(j == k, src, 0) for k, src in enumerate(col_order))
    out_specs = [pl.BlockSpec((1, tm, tn), lambda b, r, j: (b, r, jnp.minimum(j, lo_tiles - 1)))]
    out_shape = [jax.ShapeDtypeStruct((B, T, n_lo), BF16)]
    if hi_tiles:
        out_specs.append(pl.BlockSpec((1, tm, tn), lambda b, r, j: (b, r, jnp.maximum(j - lo_tiles, 0))))
        out_shape.append(jax.ShapeDtypeStruct((B, T, n_f32), F32))
    outs = pl.pallas_call(
        functools.partial(_norm_proj_kernel, n_x=len(x_ops), tm=tm, n_ctx=n_ctx, n_lo_tiles=lo_tiles),
        grid=(B, T // tm, lo_tiles + hi_tiles),
        in_specs=x_specs + [
                  pl.BlockSpec((1, SUBLANES, D), lambda b, r, j: (b, 0, 0)),
                  pl.BlockSpec((1, D), lambda b, r, j: (0, 0)),
                  pl.BlockSpec((D, tn), lambda b, r, j: (0, w_tile(j)))],
        out_specs=out_specs,
        out_shape=out_shape,
        scratch_shapes=[pltpu.VMEM((tm, D), BF16), pltpu.VMEM((4, D), F32)],
        compiler_params=_cparams(("parallel", "parallel", "arbitrary")),
        name="norm_proj",
    )(*x_ops, modrows, g.reshape(1, D), w)
    return outs if hi_tiles else outs[0]


def _rope(x, cos, sin_signed):
    lane = lax.broadcasted_iota(jnp.int32, x.shape, 1)
    n = x.shape[1]
    partner = jnp.where(lane % 32 < 16, pltpu.roll(x, n - 16, axis=1), pltpu.roll(x, 16, axis=1))
    return x * cos + partner * sin_signed


def _attn_kernel(lam_ref, q_ref, k_ref, v_ref, gate_ref, cq_ref, sq_ref, ck_ref, sk_ref, g_ref,
                 o_ref, k_scr, vt_scr, *, tq, n_ctx, out_scale):
    i = pl.program_id(2)
    n_keys = k_scr.shape[0]

    @pl.when(i == 0)
    def _():
        k_scr[...] = _rope(k_ref[0].astype(F32), ck_ref[...], sk_ref[...]).astype(BF16)
        vt_scr[...] = v_ref[0].astype(F32).T.astype(BF16)

    qr = _rope(q_ref[0].astype(F32), cq_ref[...], sq_ref[...]) * (DA_D ** -0.5 * LOG2E)
    first_map = lax.broadcasted_iota(jnp.int32, qr.shape, 1) < DA_D
    q_maps = (jnp.where(first_map, qr, 0.0).astype(BF16), jnp.where(first_map, 0.0, qr).astype(BF16))
    lam = lam_ref[0]
    qs = ATTN_Q_SUB

    def attend(n_ctx_rows):
        def scores(r0):
            nk = n_ctx if r0 < n_ctx_rows else n_keys
            q2 = jnp.concatenate([q_maps[0][r0:r0 + qs], q_maps[1][r0:r0 + qs]], axis=0)
            return lax.dot_general(k_scr[:nk, :], q2, (((1,), (1,)), ((), ())),
                                   preferred_element_type=F32)

        starts = list(range(0, tq, qs))
        pending = [scores(r0) for r0 in starts[:ATTN_AHEAD]]
        for n, r0 in enumerate(starts):
            rows = slice(r0, r0 + qs)
            s = pending.pop(0)
            nk = s.shape[0]
            if n + ATTN_AHEAD < len(starts):
                pending.append(scores(starts[n + ATTN_AHEAD]))
            e = jnp.exp2(s - jnp.max(s, axis=0, keepdims=True))
            rinv = 1.0 / jnp.sum(e, axis=0, keepdims=True)
            ov = jnp.dot(vt_scr[:, :nk], e.astype(BF16), preferred_element_type=F32) * rinv
            o = (ov[:, :qs] - lam * ov[:, qs:]).T
            y = o * lax.rsqrt(jnp.mean(o * o, axis=-1, keepdims=True) + EPS) * g_ref[...] * out_scale
            gate = gate_ref[0, rows, :].astype(F32)
            o_ref[0, rows, :] = (y * (gate * _sigmoid(gate))).astype(o_ref.dtype)

    @pl.when(i == 0)
    def _():
        attend(n_ctx)

    @pl.when(i != 0)
    def _():
        attend(0)


def _diff_attention(p, lam, cos, sin, subln_g, out_scale, n_ctx):
    B, T, _ = p.shape
    tq = max(t for t in range(n_ctx, ATTN_Q_BLOCK + 1, n_ctx) if T % t == 0)
    assert n_ctx % ATTN_Q_SUB == 0
    H = DA_HEADS
    head = lambda part: (lambda b, h, i: (b, i, part * H + h))
    head_all = lambda part: (lambda b, h, i: (b, 0, part * H + h))
    return pl.pallas_call(
        functools.partial(_attn_kernel, tq=tq, n_ctx=n_ctx, out_scale=out_scale),
        grid=(B, H, T // tq),
        in_specs=[pl.BlockSpec(memory_space=pltpu.SMEM),
                  pl.BlockSpec((1, tq, HEAD_W), head(0)),
                  pl.BlockSpec((1, T, HEAD_W), head_all(1)),
                  pl.BlockSpec((1, T, HEAD_W), head_all(2)),
                  pl.BlockSpec((1, tq, HEAD_W), head(3)),
                  pl.BlockSpec((tq, HEAD_W), lambda b, h, i: (i, 0)),
                  pl.BlockSpec((tq, HEAD_W), lambda b, h, i: (i, 0)),
                  pl.BlockSpec((T, HEAD_W), lambda b, h, i: (0, 0)),
                  pl.BlockSpec((T, HEAD_W), lambda b, h, i: (0, 0)),
                  pl.BlockSpec((1, HEAD_W), lambda b, h, i: (0, 0))],
        out_specs=pl.BlockSpec((1, tq, HEAD_W), lambda b, h, i: (b, i, h)),
        out_shape=jax.ShapeDtypeStruct((B, T, H * HEAD_W), BF16),
        scratch_shapes=[pltpu.VMEM((T, HEAD_W), BF16), pltpu.VMEM((HEAD_W, T), BF16)],
        compiler_params=_cparams(("parallel", "parallel", "arbitrary")),
        name="diff_attention",
    )(lam, p, p, p, p, cos, sin, cos, sin, subln_g.reshape(1, HEAD_W))


def _rope_tables(n_ctx, seq):
    n_freq = DA_D // 4
    inv = np.power(np.float32(ROPE_BASE), -np.arange(n_freq, dtype=np.float32) / np.float32(n_freq))
    t = np.arange(seq)
    lane = np.arange(HEAD_W)
    pos = np.where((lane % DA_D < DA_D // 2)[None, :], (t // GRID_W)[:, None], (t % GRID_W)[:, None]).astype(np.float32)
    ang = (pos * inv[lane % n_freq][None, :]).astype(np.float64)
    sign = np.where(lane % 32 < 16, -1.0, 1.0)[None, :]
    cos = np.concatenate([np.ones((n_ctx, HEAD_W)), np.cos(ang)], axis=0).astype(np.float32)
    sin = np.concatenate([np.zeros((n_ctx, HEAD_W)), np.sin(ang) * sign], axis=0).astype(np.float32)
    return jnp.asarray(cos), jnp.asarray(sin)


def _chunk_cumsum(g, reverse):
    n = g.shape[0]
    row = lax.broadcasted_iota(jnp.int32, g.shape, 0)
    b = g
    sh = 1
    while sh < n:
        if reverse:
            b = b + jnp.where(row < n - sh, pltpu.roll(b, n - sh, axis=0), 0.0)
        else:
            b = b + jnp.where(row >= sh, pltpu.roll(b, sh, axis=0), 0.0)
        sh *= 2
    return b


def _diag_tiles(qq, b2, c2, reverse):
    half = SUBLANES
    sub = lax.broadcasted_iota(jnp.int32, (half, qq.shape[1]), 0)
    tiles = []
    for s in range(qq.shape[0]):
        sb = s // half
        rows = slice(sb * half, (sb + 1) * half)
        w = qq[rows] * jnp.exp2(b2[rows] - c2[s:s + 1])
        keep = (sub <= s - sb * half) if reverse else (sub >= s - sb * half)
        tiles.append((s, sb, jnp.where(keep, w, 0.0)))
    return tiles


def _off_diag_pieces(qq, b2, c2, reverse):
    C = qq.shape[0]
    pieces = []
    span = SUBLANES
    while span < C:
        for u in range(0, C, 2 * span):
            lo, hi = slice(u, u + span), slice(u + span, u + 2 * span)
            if reverse:
                q_rows, k_rows, ref = lo, hi, b2[u + span:u + span + 1]
            else:
                q_rows, k_rows, ref = hi, lo, b2[u + span - 1:u + span]
            pieces.append((q_rows, k_rows, qq[q_rows] * jnp.exp2(b2[q_rows] - ref), jnp.exp2(ref - c2[k_rows])))
        span *= 2
    return pieces


def _off_diag_rows(chunk):
    levels = int(math.log2(chunk // SUBLANES))
    return chunk * levels


def _lane_sums(tiles, ones_ref):
    n = len(tiles)
    assert n % 2 == 0
    lhs = jnp.concatenate([jnp.concatenate([tiles[2 * i], tiles[2 * i + 1]], axis=1) for i in range(n // 2)], axis=0)
    r = jnp.dot(lhs.astype(BF16), ones_ref[...], preferred_element_type=F32)
    w = tiles[0].shape[1]
    out = []
    for i in range(n // 2):
        rows = slice(i * SUBLANES, (i + 1) * SUBLANES)
        out += [r[rows, :w], r[rows, w:]]
    return out


def _hgrn_kernel(q_ref, v_ref, gate_ref, zf_ref, zb_ref, lb_ref, g_ref, o_ref,
                 qt_scr, oi_scr, ox_scr, dl_scr, u_scr, st_scr, ones_scr, blk_scr, *, n_ctx):
    T = q_ref.shape[1]
    C = HG_CHUNK
    K = HG_K
    nc = T // C
    nc_ctx = n_ctx // C
    same_half = (lax.broadcasted_iota(jnp.int32, (2 * K, 2 * K), 0) < K) == (lax.broadcasted_iota(jnp.int32, (2 * K, 2 * K), 1) < K)
    ones_scr[...] = jnp.where(same_half, 1.0, 0.0).astype(BF16)
    sizes = []
    span = SUBLANES
    while span < C:
        sizes += [span] * (C // (2 * span))
        span *= 2
    bounds = [sum((sizes * 2)[:i]) for i in range(1, 2 * len(sizes))]
    n_off = blk_scr.shape[0]
    row_i = lax.broadcasted_iota(jnp.int32, (n_off, n_off), 0)
    col_i = lax.broadcasted_iota(jnp.int32, (n_off, n_off), 1)
    piece_of_row = sum((row_i >= b).astype(jnp.int32) for b in bounds)
    piece_of_col = sum((col_i >= b).astype(jnp.int32) for b in bounds)
    blk_scr[...] = jnp.where(piece_of_row == piece_of_col, 1.0, 0.0)

    def stage1(c):
        r0 = pl.multiple_of(c * C, C)
        hq = q_ref[0, pl.ds(r0, C), :].astype(F32)
        qq = hq * _sigmoid(hq)
        vb = v_ref[0, pl.ds(r0, C), :]
        v = vb.astype(F32)
        tiles, keys, pieces = [], [], []
        for d, z_ref in enumerate((zf_ref, zb_ref)):
            reverse = d == 1
            z = z_ref[0, pl.ds(r0, C), :]
            lb = lb_ref[0, d:d + 1, :]
            e = jnp.exp(-jnp.abs(z))
            inv = 1.0 / (1.0 + e)
            sig = jnp.where(z >= 0, inv, e * inv)
            nsig = jnp.where(z >= 0, e * inv, inv)
            lk2 = jnp.log2(1.0 - lb) + jnp.log2(nsig)
            b2 = _chunk_cumsum(jnp.log2(lb + (1.0 - lb) * sig), reverse)
            b2_last = b2[0:1] if reverse else b2[C - 1:C]
            qt_scr[d, pl.ds(r0, C), :] = (qq * jnp.exp2(b2)).astype(BF16)
            keys.append(jnp.exp2(lk2 + (b2_last - b2)).astype(BF16))
            dl_scr[d, pl.ds(c, 1), :] = jnp.exp2(b2_last)
            tiles.append(_diag_tiles(qq, b2, b2 - lk2, reverse))
            pieces.append(_off_diag_pieces(qq, b2, b2 - lk2, reverse))
        sums = _lane_sums([w for per_dir in tiles for _, _, w in per_dir], ones_scr)
        q_off = jnp.concatenate([p[2] for per_dir in pieces for p in per_dir], axis=0).astype(BF16)
        k_off = jnp.concatenate([p[3] for per_dir in pieces for p in per_dir], axis=0).astype(BF16)
        cross = lax.dot_general(q_off, k_off, (((1,), (1,)), ((), ())), preferred_element_type=F32)
        upd = lax.dot_general(vb, jnp.concatenate(keys, axis=1), (((0,), (0,)), ((), ())),
                              preferred_element_type=F32)
        u_scr[0, c] = upd[:, :K]
        u_scr[1, c] = upd[:, K:]
        return r0, v, tiles, sums, pieces, cross

    def stage2(r0, v, tiles, sums, pieces, cross):
        v_off = jnp.concatenate([v[p[1]] for per_dir in pieces for p in per_dir], axis=0).astype(BF16)
        contrib = jnp.dot((cross * blk_scr[...]).astype(BF16), v_off, preferred_element_type=F32)
        k = 0
        off = 0
        for d in range(2):
            parts = [jnp.zeros((SUBLANES, v.shape[1]), F32) for _ in range(C // SUBLANES)]
            for s, blk, _ in tiles[d]:
                parts[blk] = parts[blk] + sums[k] * v[s:s + 1]
                k += 1
            for q_rows, _, qp, _ in pieces[d]:
                for i in range(qp.shape[0] // SUBLANES):
                    blk = q_rows.start // SUBLANES + i
                    parts[blk] = parts[blk] + contrib[off:off + SUBLANES]
                    off += SUBLANES
            oi_scr[d, pl.ds(r0, C), :] = jnp.concatenate(parts, axis=0)

    def prep(t, carry):
        staged = [stage1(t * HG_PREP_UNROLL + j) for j in range(HG_PREP_UNROLL)]
        for args in staged:
            stage2(*args)
        return carry

    assert nc % HG_PREP_UNROLL == 0
    lax.fori_loop(0, nc // HG_PREP_UNROLL, prep, 0)

    st_scr[...] = jnp.zeros_like(st_scr)

    def scan(m, carry):
        i0 = m * HG_GROUP
        cb0 = jnp.where(i0 < nc_ctx, nc_ctx - 1 - i0, nc + nc_ctx - 1 - i0)
        for j in range(HG_GROUP):
            for d, c in enumerate((i0 + j, cb0 - j)):
                r0 = pl.multiple_of(c * C, C)
                st = st_scr[d]
                ox_scr[d, pl.ds(r0, C), :] = jnp.dot(qt_scr[d, pl.ds(r0, C), :], st.T.astype(BF16),
                                                     preferred_element_type=F32)
                st_scr[d] = dl_scr[d, pl.ds(c, 1), :] * st + u_scr[d, c]
        return carry

    assert nc % HG_GROUP == 0 and nc_ctx % HG_GROUP == 0
    lax.fori_loop(0, nc // HG_GROUP, scan, 0)

    o = (oi_scr[0] + ox_scr[0]) + (oi_scr[1] + ox_scr[1])
    y = o * lax.rsqrt(jnp.mean(o * o, axis=-1, keepdims=True) + EPS) * g_ref[...]
    gate = gate_ref[0].astype(F32)
    o_ref[0] = (y * (gate * _sigmoid(gate))).astype(o_ref.dtype)


def _hgrn(p, pf, lb, norm_g, n_ctx, part0):
    B, T, _ = p.shape
    H = HG_HEADS
    nc = T // HG_CHUNK
    part = lambda k: (lambda b, h: (b, 0, k * H + h))
    blk = (1, T, HEAD_W)
    return pl.pallas_call(
        functools.partial(_hgrn_kernel, n_ctx=n_ctx),
        grid=(B, H),
        in_specs=[pl.BlockSpec(blk, part(part0)), pl.BlockSpec(blk, part(part0 + 1)),
                  pl.BlockSpec(blk, part(part0 + 2)),
                  pl.BlockSpec(blk, part(0)), pl.BlockSpec(blk, part(1)),
                  pl.BlockSpec((1, 2, HG_K), lambda b, h: (h, 0, 0)),
                  pl.BlockSpec((1, HEAD_W), lambda b, h: (0, 0))],
        out_specs=pl.BlockSpec(blk, lambda b, h: (b, 0, h)),
        out_shape=jax.ShapeDtypeStruct((B, T, H * HEAD_W), BF16),
        scratch_shapes=[pltpu.VMEM((2, T, HG_K), BF16),
                        pltpu.VMEM((2, T, HEAD_W), F32), pltpu.VMEM((2, T, HEAD_W), F32),
                        pltpu.VMEM((2, nc, HG_K), F32),
                        pltpu.VMEM((2, nc, HEAD_W, HG_K), F32),
                        pltpu.VMEM((2, HEAD_W, HG_K), F32),
                        pltpu.VMEM((2 * HG_K, 2 * HG_K), BF16),
                        pltpu.VMEM((_off_diag_rows(HG_CHUNK), _off_diag_rows(HG_CHUNK)), F32)],
        compiler_params=_cparams(("parallel", "parallel")),
        name="hgrn2",
    )(p, p, p, pf, pf, lb, norm_g.reshape(1, HEAD_W))


def _shift_rows(x, d):
    n = x.shape[0]
    row = lax.broadcasted_iota(jnp.int32, (n, 1), 0)
    rolled = pltpu.roll(x, (-d) % n, axis=0)
    keep = (row + d >= 0) & (row + d < n)
    return jnp.where(keep, rolled, 0.0)


def _pool_minus_identity(x, w):
    n = x.shape[0]
    ahead = w - w // 2
    behind = w // 2
    fwd = x
    span = 1
    while span < ahead:
        fwd = fwd + _shift_rows(fwd, span)
        span *= 2
    bwd = x
    span = 1
    while span < behind:
        bwd = bwd + _shift_rows(bwd, -span)
        span *= 2
    total = fwd + _shift_rows(bwd, -1)
    row = lax.broadcasted_iota(jnp.int32, (n, 1), 0)
    cnt = jnp.minimum(row + ahead, n) - jnp.maximum(row - behind, 0)
    return total / cnt.astype(F32) - x


def _pool_kernel(u_ref, z_ref, w_ref, ls_ref, o_ref, r_scr, *, n_ctx):
    j = pl.program_id(1)
    T = u_ref.shape[1]
    for jj, win in enumerate(POOL_WINDOWS):
        @pl.when(j == jj)
        def _(win=win):
            for lo, hi in ((0, n_ctx), (n_ctx, T)):
                r_scr[lo:hi, :] = _pool_minus_identity(u_ref[0, lo:hi, :].astype(F32), win).astype(BF16)

    sub = POOL_SUB if T % POOL_SUB == 0 else T
    project = lambda r0: jnp.dot(r_scr[r0:r0 + sub, :], w_ref[0], preferred_element_type=F32)
    y_next = project(0)
    for r0 in range(0, T, sub):
        y = y_next
        if r0 + sub < T:
            y_next = project(r0 + sub)
        z = z_ref[0, r0:r0 + sub, :].astype(F32)
        o_ref[0, r0:r0 + sub, :] = (y * ls_ref[...] * (z * _sigmoid(z))).astype(o_ref.dtype)


def _pool_mixer(p, w_pool, ls, n_ctx):
    B, T, two_e = p.shape
    G, gw, _ = w_pool.shape
    assert G == len(POOL_WINDOWS) and two_e == 2 * G * gw
    return pl.pallas_call(
        functools.partial(_pool_kernel, n_ctx=n_ctx),
        grid=(B, G),
        in_specs=[pl.BlockSpec((1, T, gw), lambda b, j: (b, 0, j)),
                  pl.BlockSpec((1, T, gw), lambda b, j: (b, 0, G + j)),
                  pl.BlockSpec((1, gw, gw), lambda b, j: (j, 0, 0)),
                  pl.BlockSpec((1, gw), lambda b, j: (0, j))],
        out_specs=pl.BlockSpec((1, T, gw), lambda b, j: (b, 0, j)),
        out_shape=jax.ShapeDtypeStruct((B, T, G * gw), BF16),
        scratch_shapes=[pltpu.VMEM((T, gw), BF16)],
        compiler_params=_cparams(("parallel", "parallel")),
        name="pool_mixer",
    )(p, p, w_pool, ls.reshape(1, G * gw))


def _out_proj_kernel(a_ref, b_ref, w_ref, *refs, n_x, tm, n_ctx, row_off):
    x_refs = refs[:n_x]
    mod_ref, g_ref, o_ref = refs[n_x:]
    ka = a_ref.shape[2]
    sub = OUT_SUB if tm % OUT_SUB == 0 else tm
    sources = _tile_sources(x_refs, tm, n_ctx)
    assert all(n_rows % sub == 0 for _, _, n_rows in sources)

    def residual(r0):
        for ref, first_tile_ref, n_rows in sources:
            if r0 < n_rows:
                x = ref[0, r0:r0 + sub, :]
                if first_tile_ref is not None:
                    x = jnp.where(pl.program_id(1) == 0, first_tile_ref[0, r0:r0 + sub, :], x)
                return x
            r0 -= n_rows

    def project(r0):
        y = jnp.dot(a_ref[0, r0:r0 + sub, :], w_ref[:ka, :], preferred_element_type=F32)
        return y + jnp.dot(b_ref[0, r0:r0 + sub, :], w_ref[ka:, :], preferred_element_type=F32)

    y_next = project(0)
    for r0 in range(0, tm, sub):
        y = y_next
        if r0 + sub < tm:
            y_next = project(r0 + sub)
        yn = y * lax.rsqrt(jnp.mean(y * y, axis=-1, keepdims=True) + EPS) * g_ref[...]
        gt = _row_mod(mod_ref, 2, 5, row_off + pl.program_id(1) * tm + r0, sub, n_ctx)
        o_ref[0, r0:r0 + sub, :] = residual(r0) + gt * yn


def _out_proj(a, b, a_blk, b_blk, w, x, modrows, g, n_ctx, tm, latents_only):
    off = n_ctx // tm if latents_only else 0
    x_ops, x_specs, (B, T, D) = _row_operands(x, tm, lambda bb, r: (bb, r + off), n_ctx)
    half_k = w.shape[0] // 2
    rows_out = T - n_ctx if latents_only else T
    return pl.pallas_call(
        functools.partial(_out_proj_kernel, n_x=len(x_ops), tm=tm, n_ctx=n_ctx, row_off=off * tm),
        grid=(B, rows_out // tm),
        in_specs=[pl.BlockSpec((1, tm, half_k), lambda bb, r: (bb, r + off, a_blk)),
                  pl.BlockSpec((1, tm, half_k), lambda bb, r: (bb, r + off, b_blk)),
                  pl.BlockSpec(w.shape, lambda bb, r: (0, 0), pipeline_mode=pl.Buffered(1))]
        + x_specs + [
                  pl.BlockSpec((1, SUBLANES, D), lambda bb, r: (bb, 0, 0)),
                  pl.BlockSpec((1, D), lambda bb, r: (0, 0))],
        out_specs=pl.BlockSpec((1, tm, D), lambda bb, r: (bb, r, 0)),
        out_shape=jax.ShapeDtypeStruct((B, rows_out, D), F32),
        compiler_params=_cparams(("parallel", "parallel")),
        name="out_proj",
    )(a, b, w, *x_ops, modrows, g.reshape(1, D))


def _row_tile(T, cap):
    best = NORM_SLAB
    for t in range(NORM_SLAB, min(T, cap) + 1, NORM_SLAB):
        if T % t == 0:
            best = t
    return best


def _col_tile(N, cap):
    best = 128
    for t in range(128, min(N, cap) + 1, 128):
        if N % t == 0:
            best = t
    return best


def kernel(x, c, ctx, c_ctx, w_mod, b_mod, g_pre, g_post, ev_w_in, ev_w_out, ev_lambda, ev_subln_g,
           ev_hg_lb_logits, ev_hg_norm_g, od_w_in, od_w_pool, od_scale, od_w_out):
    B, S, D = x.shape
    n_ctx = ctx.shape[1]
    depth = w_mod.shape[0]
    T = n_ctx + S
    W = DA_HEADS * HEAD_W

    xc = (ctx, x)
    tm_pair = max(t for t in range(n_ctx, 768 + 1, n_ctx) if T % t == 0)

    n_rows = -(-(B + 1) // SUBLANES) * SUBLANES
    cvec = jnp.zeros((n_rows, D), F32).at[:B].set(c).at[B].set(c_ctx)
    mod = _modulation(cvec, w_mod, b_mod)

    lb_cum = jnp.cumsum(jax.nn.softmax(ev_hg_lb_logits.astype(F32), axis=0), axis=0)
    lb_all = lb_cum - lb_cum[0]
    cos, sin = _rope_tables(n_ctx, S)

    tm_in = _row_tile(T, 1152)
    tm_out = _row_tile(T, 768)

    for l in range(depth):
        last = l == depth - 1
        tm_l, tm_o = (tm_in, tm_pair) if isinstance(xc, tuple) else (tm_in, n_ctx if last else tm_out)
        ml = mod[l]
        lat = ml[:B].reshape(B, 3, D)
        cx = jnp.broadcast_to(ml[B].reshape(1, 3, D), (B, 3, D))
        modrows = jnp.concatenate([lat, cx, jnp.zeros((B, SUBLANES - 6, D), F32)], axis=1)

        if l % 2 == 0:
            e = l // 2
            assert _col_tile(W, 1024) == W
            p, pf = _norm_proj(xc, modrows, g_pre[l], ev_w_in[e].astype(BF16), 2 * W, n_ctx, tm_l, W,
                               col_order=(0, 1, 2, 3, 4, 7, 8, 5, 6))

            lam_init = 0.8 - 0.6 * math.exp(-0.3 * l)
            lv = ev_lambda[e].astype(F32)
            lam = jnp.exp(jnp.sum(lv[0] * lv[1])) - jnp.exp(jnp.sum(lv[2] * lv[3])) + lam_init
            a = _diff_attention(p, lam.reshape(1), cos, sin, ev_subln_g[e], 1.0 - lam_init, n_ctx)
            lb = lb_all[e].reshape(2, HG_HEADS, HG_K).transpose(1, 0, 2)
            bh = _hgrn(p, pf, lb, ev_hg_norm_g[e], n_ctx, 4)
            w_out = ev_w_out[e].astype(BF16)
            ya, yb, ia, ib = a, bh, 0, 0
        else:
            o = l // 2
            p = _norm_proj(xc, modrows, g_pre[l], od_w_in[o].astype(BF16), 0, n_ctx, tm_l,
                           _col_tile(od_w_in.shape[2], 2048))
            y = _pool_mixer(p, od_w_pool[o].astype(BF16), od_scale[o], n_ctx)
            w_out = od_w_out[o].astype(BF16)
            ya, yb, ia, ib = y, y, 0, 1

        xc = _out_proj(ya, yb, ia, ib, w_out, xc, modrows, g_post[l], n_ctx, tm_o, latents_only=last)
    return xc
```

```python
import functools
import math

import jax
import jax.numpy as jnp
import numpy as np
from jax import lax
from jax.experimental import pallas as pl
from jax.experimental.pallas import tpu as pltpu

DA_HEADS = 8
DA_D = 64
HEAD_W = 2 * DA_D
HG_HEADS = 8
HG_K = 128
GRID_W = 64
ROPE_BASE = 10000.0
EPS = 1e-6
POOL_WINDOWS = (2, 4, 8, 16)
HG_CHUNK = 32
LOG2E = math.log2(math.e)
ATTN_Q_BLOCK = 2304
ATTN_HEADS_PER_STEP = 2
ATTN_Q_SUB = 128
ATTN_AHEAD = 8
HG_GROUP = 8
HG_PREP_UNROLL = 8
SUBLANES = 8
POOL_SUB = 768
OUT_SUB = 256
NORM_SLAB = 16
NORM_UNROLL = 4

F32 = jnp.float32
BF16 = jnp.bfloat16
VMEM_LIMIT = 56 * 1024 * 1024


def _cparams(sem):
    return pltpu.CompilerParams(dimension_semantics=sem, vmem_limit_bytes=VMEM_LIMIT)


def _sigmoid(x):
    return 1.0 / (1.0 + jnp.exp(-x))


def _mod_kernel(c_ref, w_ref, b_ref, o_ref):
    c = c_ref[...]
    s = c * _sigmoid(c)
    o_ref[0] = jnp.dot(s, w_ref[0], preferred_element_type=F32,
                       precision=lax.Precision.HIGHEST) + b_ref[0]


def _modulation(cvec, w_mod, b_mod):
    L, D, N = w_mod.shape
    R = cvec.shape[0]
    tn = 1024 if N % 1024 == 0 else N
    return pl.pallas_call(
        _mod_kernel,
        grid=(L, N // tn),
        in_specs=[pl.BlockSpec((R, D), lambda l, j: (0, 0)),
                  pl.BlockSpec((1, D, tn), lambda l, j: (l, 0, j)),
                  pl.BlockSpec((1, 1, tn), lambda l, j: (l, 0, j))],
        out_specs=pl.BlockSpec((1, R, tn), lambda l, j: (l, 0, j)),
        out_shape=jax.ShapeDtypeStruct((L, R, N), F32),
        compiler_params=_cparams(("parallel", "parallel")),
        name="modulation",
    )(cvec, w_mod, b_mod.reshape(L, 1, N))


def _row_mod(mod_ref, lat_row, ctx_row, row0, tm, n_ctx):
    rows = row0 + lax.broadcasted_iota(jnp.int32, (tm, 1), 0)
    return jnp.where(rows < n_ctx, mod_ref[0, ctx_row:ctx_row + 1, :], mod_ref[0, lat_row:lat_row + 1, :])


def _tile_sources(x_refs, tm, n_ctx):
    if len(x_refs) == 1:
        return [(x_refs[0], None, tm)]
    src_rows = x_refs[0].shape[1]
    n_ctx_src = n_ctx // src_rows
    return [(ref, x_refs[i] if i < n_ctx_src else None, src_rows) for i, ref in enumerate(x_refs[n_ctx_src:])]


def _norm_proj_kernel(*refs, n_x, tm, n_ctx, n_lo_tiles):
    x_refs = refs[:n_x]
    mod_ref, g_ref, w_ref, o_ref = refs[n_x:n_x + 4]
    rest = refs[n_x + 4:]
    h_scr, gs_scr = rest[-2:]
    r = pl.program_id(1)
    j = pl.program_id(2)

    @pl.when(j == 0)
    def _():
        gs_scr[0:1, :] = g_ref[...] * (1.0 + mod_ref[0, 1:2, :])
        gs_scr[1:2, :] = mod_ref[0, 0:1, :]
        gs_scr[2:3, :] = g_ref[...] * (1.0 + mod_ref[0, 4:5, :])
        gs_scr[3:4, :] = mod_ref[0, 3:4, :]

        row0 = 0
        for ref, first_tile_ref, n_rows in _tile_sources(x_refs, tm, n_ctx):
            def slab(k, carry, ref=ref, first_tile_ref=first_tile_ref, row0=row0):
                r0 = pl.multiple_of(k * NORM_SLAB, NORM_SLAB)
                x = ref[0, pl.ds(r0, NORM_SLAB), :]
                if first_tile_ref is not None:
                    x = jnp.where(r == 0, first_tile_ref[0, pl.ds(r0, NORM_SLAB), :], x)
                base = jnp.where(r * tm + row0 + r0 < n_ctx, 2, 0)
                y = x * lax.rsqrt(jnp.mean(x * x, axis=-1, keepdims=True) + EPS)
                h_scr[pl.ds(row0 + r0, NORM_SLAB), :] = (y * gs_scr[pl.ds(base, 1), :]
                                                         + gs_scr[pl.ds(base + 1, 1), :]).astype(BF16)
                return carry

            lax.fori_loop(0, n_rows // NORM_SLAB, slab, 0, unroll=NORM_UNROLL)
            row0 += n_rows

    if len(rest) == 2:
        o_ref[0] = jnp.dot(h_scr[...], w_ref[...], preferred_element_type=F32).astype(o_ref.dtype)
    else:
        @pl.when(j < n_lo_tiles)
        def _():
            o_ref[0] = jnp.dot(h_scr[...], w_ref[...], preferred_element_type=F32).astype(o_ref.dtype)

        @pl.when(j >= n_lo_tiles)
        def _():
            rest[0][0] = jnp.dot(h_scr[...], w_ref[...], preferred_element_type=F32)


def _row_operands(x, tm, tile_of, src_rows):
    if not isinstance(x, tuple):
        B, T, D = x.shape
        return [x], [pl.BlockSpec((1, tm, D), lambda *g: (*tile_of(*g), 0))], (B, T, D)
    ctx, lat = x
    B, n_ctx, D = ctx.shape
    assert tm % src_rows == 0 and n_ctx % src_rows == 0 and lat.shape[1] % src_rows == 0 and tm >= n_ctx
    q = tm // src_rows
    n_ctx_src = n_ctx // src_rows

    def ctx_block(i):
        return pl.BlockSpec((1, src_rows, D), lambda *g: (tile_of(*g)[0], i, 0))

    def lat_block(i):
        def index(*g):
            b, r = tile_of(*g)
            return b, jnp.maximum(q * r + i - n_ctx_src, 0), 0
        return pl.BlockSpec((1, src_rows, D), index)

    specs = [ctx_block(i) for i in range(n_ctx_src)] + [lat_block(i) for i in range(q)]
    return [ctx] * n_ctx_src + [lat] * q, specs, (B, n_ctx + lat.shape[1], D)


def _norm_proj(x, modrows, g, w, n_f32, n_ctx, tm, tn, col_order=None):
    x_ops, x_specs, (B, T, D) = _row_operands(x, tm, lambda b, r, j: (b, r), math.gcd(tm, n_ctx))
    n_lo = w.shape[1] - n_f32
    assert n_lo % tn == 0 and n_f32 % tn == 0
    lo_tiles, hi_tiles = n_lo // tn, n_f32 // tn
    if col_order is None:
        w_tile = lambda j: j
    else:
        assert sorted(col_order) == list(range(lo_tiles + hi_tiles))
        w_tile = lambda j: sum(jnp.where(j == k, src, 0) for k, src in enumerate(col_order))
    out_specs = [pl.BlockSpec((1, tm, tn), lambda b, r, j: (b, r, jnp.minimum(j, lo_tiles - 1)))]
    out_shape = [jax.ShapeDtypeStruct((B, T, n_lo), BF16)]
    if hi_tiles:
        out_specs.append(pl.BlockSpec((1, tm, tn), lambda b, r, j: (b, r, jnp.maximum(j - lo_tiles, 0))))
        out_shape.append(jax.ShapeDtypeStruct((B, T, n_f32), F32))
    outs = pl.pallas_call(
        functools.partial(_norm_proj_kernel, n_x=len(x_ops), tm=tm, n_ctx=n_ctx, n_lo_tiles=lo_tiles),
        grid=(B, T // tm, lo_tiles + hi_tiles),
        in_specs=x_specs + [
                  pl.BlockSpec((1, SUBLANES, D), lambda b, r, j: (b, 0, 0)),
                  pl.BlockSpec((1, D), lambda b, r, j: (0, 0)),
                  pl.BlockSpec((D, tn), lambda b, r, j: (0, w_tile(j)))],
        out_specs=out_specs,
        out_shape=out_shape,
        scratch_shapes=[pltpu.VMEM((tm, D), BF16), pltpu.VMEM((4, D), F32)],
        compiler_params=_cparams(("parallel", "parallel", "arbitrary")),
        name="norm_proj",
    )(*x_ops, modrows, g.reshape(1, D), w)
    return outs if hi_tiles else outs[0]


def _rope(x, cos, sin_signed):
    lane = lax.broadcasted_iota(jnp.int32, x.shape, 1)
    n = x.shape[1]
    partner = jnp.where(lane % 32 < 16, pltpu.roll(x, n - 16, axis=1), pltpu.roll(x, 16, axis=1))
    return x * cos + partner * sin_signed


def _attn_kernel(lam_ref, q_ref, k_ref, v_ref, gate_ref, cq_ref, sq_ref, ck_ref, sk_ref, g_ref,
                 o_ref, k_scr, vt_scr, *, tq, n_ctx, out_scale):
    for hh in range(q_ref.shape[2] // HEAD_W):
        _attn_one_head(lam_ref, q_ref, k_ref, v_ref, gate_ref, cq_ref, sq_ref, ck_ref, sk_ref, g_ref, o_ref,
                       k_scr, vt_scr, slice(hh * HEAD_W, (hh + 1) * HEAD_W), tq, n_ctx, out_scale)


def _attn_one_head(lam_ref, q_ref, k_ref, v_ref, gate_ref, cq_ref, sq_ref, ck_ref, sk_ref, g_ref, o_ref,
                   k_scr, vt_scr, cols, tq, n_ctx, out_scale):
    i = pl.program_id(2)
    n_keys = k_scr.shape[0]

    @pl.when(i == 0)
    def _():
        k_scr[...] = _rope(k_ref[0, :, cols].astype(F32), ck_ref[...], sk_ref[...]).astype(BF16)
        vt_scr[...] = v_ref[0, :, cols].astype(F32).T.astype(BF16)

    qr = _rope(q_ref[0, :, cols].astype(F32), cq_ref[...], sq_ref[...]) * (DA_D ** -0.5 * LOG2E)
    first_map = lax.broadcasted_iota(jnp.int32, qr.shape, 1) < DA_D
    q_maps = (jnp.where(first_map, qr, 0.0).astype(BF16), jnp.where(first_map, 0.0, qr).astype(BF16))
    lam = lam_ref[0]
    qs = ATTN_Q_SUB

    def attend(n_ctx_rows):
        def scores(r0):
            nk = n_ctx if r0 < n_ctx_rows else n_keys
            q2 = jnp.concatenate([q_maps[0][r0:r0 + qs], q_maps[1][r0:r0 + qs]], axis=0)
            return lax.dot_general(k_scr[:nk, :], q2, (((1,), (1,)), ((), ())),
                                   preferred_element_type=F32)

        starts = list(range(0, tq, qs))
        pending = [scores(r0) for r0 in starts[:ATTN_AHEAD]]
        for n, r0 in enumerate(starts):
            rows = slice(r0, r0 + qs)
            s = pending.pop(0)
            nk = s.shape[0]
            if n + ATTN_AHEAD < len(starts):
                pending.append(scores(starts[n + ATTN_AHEAD]))
            e = jnp.exp2(s - jnp.max(s, axis=0, keepdims=True))
            rinv = 1.0 / jnp.sum(e, axis=0, keepdims=True)
            ov = jnp.dot(vt_scr[:, :nk], e.astype(BF16), preferred_element_type=F32) * rinv
            o = (ov[:, :qs] - lam * ov[:, qs:]).T
            y = o * lax.rsqrt(jnp.mean(o * o, axis=-1, keepdims=True) + EPS) * g_ref[...] * out_scale
            gate = gate_ref[0, rows, cols].astype(F32)
            o_ref[0, rows, cols] = (y * (gate * _sigmoid(gate))).astype(o_ref.dtype)

    @pl.when(i == 0)
    def _():
        attend(n_ctx)

    @pl.when(i != 0)
    def _():
        attend(0)


def _diff_attention(p, lam, cos, sin, subln_g, out_scale, n_ctx):
    B, T, _ = p.shape
    tq = max(t for t in range(n_ctx, ATTN_Q_BLOCK + 1, n_ctx) if T % t == 0)
    assert n_ctx % ATTN_Q_SUB == 0 and DA_HEADS % ATTN_HEADS_PER_STEP == 0
    H = DA_HEADS // ATTN_HEADS_PER_STEP
    gw = ATTN_HEADS_PER_STEP * HEAD_W
    head = lambda part: (lambda b, h, i: (b, i, part * H + h))
    head_all = lambda part: (lambda b, h, i: (b, 0, part * H + h))
    return pl.pallas_call(
        functools.partial(_attn_kernel, tq=tq, n_ctx=n_ctx, out_scale=out_scale),
        grid=(B, H, T // tq),
        in_specs=[pl.BlockSpec(memory_space=pltpu.SMEM),
                  pl.BlockSpec((1, tq, gw), head(0)),
                  pl.BlockSpec((1, T, gw), head_all(1)),
                  pl.BlockSpec((1, T, gw), head_all(2)),
                  pl.BlockSpec((1, tq, gw), head(3)),
                  pl.BlockSpec((tq, HEAD_W), lambda b, h, i: (i, 0)),
                  pl.BlockSpec((tq, HEAD_W), lambda b, h, i: (i, 0)),
                  pl.BlockSpec((T, HEAD_W), lambda b, h, i: (0, 0)),
                  pl.BlockSpec((T, HEAD_W), lambda b, h, i: (0, 0)),
                  pl.BlockSpec((1, HEAD_W), lambda b, h, i: (0, 0))],
        out_specs=pl.BlockSpec((1, tq, gw), lambda b, h, i: (b, i, h)),
        out_shape=jax.ShapeDtypeStruct((B, T, DA_HEADS * HEAD_W), BF16),
        scratch_shapes=[pltpu.VMEM((T, HEAD_W), BF16), pltpu.VMEM((HEAD_W, T), BF16)],
        compiler_params=_cparams(("parallel", "parallel", "arbitrary")),
        name="diff_attention",
    )(lam, p, p, p, p, cos, sin, cos, sin, subln_g.reshape(1, HEAD_W))


def _rope_tables(n_ctx, seq):
    n_freq = DA_D // 4
    inv = np.power(np.float32(ROPE_BASE), -np.arange(n_freq, dtype=np.float32) / np.float32(n_freq))
    t = np.arange(seq)
    lane = np.arange(HEAD_W)
    pos = np.where((lane % DA_D < DA_D // 2)[None, :], (t // GRID_W)[:, None], (t % GRID_W)[:, None]).astype(np.float32)
    ang = (pos * inv[lane % n_freq][None, :]).astype(np.float64)
    sign = np.where(lane % 32 < 16, -1.0, 1.0)[None, :]
    cos = np.concatenate([np.ones((n_ctx, HEAD_W)), np.cos(ang)], axis=0).astype(np.float32)
    sin = np.concatenate([np.zeros((n_ctx, HEAD_W)), np.sin(ang) * sign], axis=0).astype(np.float32)
    return jnp.asarray(cos), jnp.asarray(sin)


def _chunk_cumsum(g, reverse):
    n = g.shape[0]
    row = lax.broadcasted_iota(jnp.int32, g.shape, 0)
    b = g
    sh = 1
    while sh < n:
        if reverse:
            b = b + jnp.where(row < n - sh, pltpu.roll(b, n - sh, axis=0), 0.0)
        else:
            b = b + jnp.where(row >= sh, pltpu.roll(b, sh, axis=0), 0.0)
        sh *= 2
    return b


def _diag_tiles(qq, b2, c2, reverse):
    half = SUBLANES
    sub = lax.broadcasted_iota(jnp.int32, (half, qq.shape[1]), 0)
    tiles = []
    for s in range(qq.shape[0]):
        sb = s // half
        rows = slice(sb * half, (sb + 1) * half)
        w = qq[rows] * jnp.exp2(b2[rows] - c2[s:s + 1])
        keep = (sub <= s - sb * half) if reverse else (sub >= s - sb * half)
        tiles.append((s, sb, jnp.where(keep, w, 0.0)))
    return tiles


def _off_diag_pieces(qq, b2, c2, reverse):
    C = qq.shape[0]
    pieces = []
    span = SUBLANES
    while span < C:
        for u in range(0, C, 2 * span):
            lo, hi = slice(u, u + span), slice(u + span, u + 2 * span)
            if reverse:
                q_rows, k_rows, ref = lo, hi, b2[u + span:u + span + 1]
            else:
                q_rows, k_rows, ref = hi, lo, b2[u + span - 1:u + span]
            pieces.append((q_rows, k_rows, qq[q_rows] * jnp.exp2(b2[q_rows] - ref), jnp.exp2(ref - c2[k_rows])))
        span *= 2
    return pieces


def _off_diag_rows(chunk):
    levels = int(math.log2(chunk // SUBLANES))
    return chunk * levels


def _lane_sums(tiles, ones_ref):
    n = len(tiles)
    assert n % 2 == 0
    lhs = jnp.concatenate([jnp.concatenate([tiles[2 * i], tiles[2 * i + 1]], axis=1) for i in range(n // 2)], axis=0)
    r = jnp.dot(lhs.astype(BF16), ones_ref[...], preferred_element_type=F32)
    w = tiles[0].shape[1]
    out = []
    for i in range(n // 2):
        rows = slice(i * SUBLANES, (i + 1) * SUBLANES)
        out += [r[rows, :w], r[rows, w:]]
    return out


def _hgrn_kernel(q_ref, v_ref, gate_ref, zf_ref, zb_ref, lb_ref, g_ref, o_ref,
                 qt_scr, oi_scr, ox_scr, dl_scr, u_scr, st_scr, ones_scr, blk_scr, *, n_ctx):
    T = q_ref.shape[1]
    C = HG_CHUNK
    K = HG_K
    nc = T // C
    nc_ctx = n_ctx // C
    same_half = (lax.broadcasted_iota(jnp.int32, (2 * K, 2 * K), 0) < K) == (lax.broadcasted_iota(jnp.int32, (2 * K, 2 * K), 1) < K)
    ones_scr[...] = jnp.where(same_half, 1.0, 0.0).astype(BF16)
    sizes = []
    span = SUBLANES
    while span < C:
        sizes += [span] * (C // (2 * span))
        span *= 2
    bounds = [sum((sizes * 2)[:i]) for i in range(1, 2 * len(sizes))]
    n_off = blk_scr.shape[0]
    row_i = lax.broadcasted_iota(jnp.int32, (n_off, n_off), 0)
    col_i = lax.broadcasted_iota(jnp.int32, (n_off, n_off), 1)
    piece_of_row = sum((row_i >= b).astype(jnp.int32) for b in bounds)
    piece_of_col = sum((col_i >= b).astype(jnp.int32) for b in bounds)
    blk_scr[...] = jnp.where(piece_of_row == piece_of_col, 1.0, 0.0)

    def stage1(c):
        r0 = pl.multiple_of(c * C, C)
        hq = q_ref[0, pl.ds(r0, C), :].astype(F32)
        qq = hq * _sigmoid(hq)
        vb = v_ref[0, pl.ds(r0, C), :]
        v = vb.astype(F32)
        tiles, keys, pieces = [], [], []
        for d, z_ref in enumerate((zf_ref, zb_ref)):
            reverse = d == 1
            z = z_ref[0, pl.ds(r0, C), :]
            lb = lb_ref[0, d:d + 1, :]
            e = jnp.exp(-jnp.abs(z))
            inv = 1.0 / (1.0 + e)
            sig = jnp.where(z >= 0, inv, e * inv)
            nsig = jnp.where(z >= 0, e * inv, inv)
            lk2 = jnp.log2(1.0 - lb) + jnp.log2(nsig)
            b2 = _chunk_cumsum(jnp.log2(lb + (1.0 - lb) * sig), reverse)
            b2_last = b2[0:1] if reverse else b2[C - 1:C]
            qt_scr[d, pl.ds(r0, C), :] = (qq * jnp.exp2(b2)).astype(BF16)
            keys.append(jnp.exp2(lk2 + (b2_last - b2)).astype(BF16))
            dl_scr[d, pl.ds(c, 1), :] = jnp.exp2(b2_last)
            tiles.append(_diag_tiles(qq, b2, b2 - lk2, reverse))
            pieces.append(_off_diag_pieces(qq, b2, b2 - lk2, reverse))
        sums = _lane_sums([w for per_dir in tiles for _, _, w in per_dir], ones_scr)
        q_off = jnp.concatenate([p[2] for per_dir in pieces for p in per_dir], axis=0).astype(BF16)
        k_off = jnp.concatenate([p[3] for per_dir in pieces for p in per_dir], axis=0).astype(BF16)
        cross = lax.dot_general(q_off, k_off, (((1,), (1,)), ((), ())), preferred_element_type=F32)
        upd = lax.dot_general(vb, jnp.concatenate(keys, axis=1), (((0,), (0,)), ((), ())),
                              preferred_element_type=F32)
        u_scr[0, c] = upd[:, :K]
        u_scr[1, c] = upd[:, K:]
        return r0, v, tiles, sums, pieces, cross

    def stage2(r0, v, tiles, sums, pieces, cross):
        v_off = jnp.concatenate([v[p[1]] for per_dir in pieces for p in per_dir], axis=0).astype(BF16)
        contrib = jnp.dot((cross * blk_scr[...]).astype(BF16), v_off, preferred_element_type=F32)
        k = 0
        off = 0
        for d in range(2):
            parts = [jnp.zeros((SUBLANES, v.shape[1]), F32) for _ in range(C // SUBLANES)]
            for s, blk, _ in tiles[d]:
                parts[blk] = parts[blk] + sums[k] * v[s:s + 1]
                k += 1
            for q_rows, _, qp, _ in pieces[d]:
                for i in range(qp.shape[0] // SUBLANES):
                    blk = q_rows.start // SUBLANES + i
                    parts[blk] = parts[blk] + contrib[off:off + SUBLANES]
                    off += SUBLANES
            oi_scr[d, pl.ds(r0, C), :] = jnp.concatenate(parts, axis=0)

    def prep(t, carry):
        staged = [stage1(t * HG_PREP_UNROLL + j) for j in range(HG_PREP_UNROLL)]
        for args in staged:
            stage2(*args)
        return carry

    assert nc % HG_PREP_UNROLL == 0
    lax.fori_loop(0, nc // HG_PREP_UNROLL, prep, 0)

    st_scr[...] = jnp.zeros_like(st_scr)

    def scan(m, carry):
        i0 = m * HG_GROUP
        cb0 = jnp.where(i0 < nc_ctx, nc_ctx - 1 - i0, nc + nc_ctx - 1 - i0)
        for j in range(HG_GROUP):
            for d, c in enumerate((i0 + j, cb0 - j)):
                r0 = pl.multiple_of(c * C, C)
                st = st_scr[d]
                ox_scr[d, pl.ds(r0, C), :] = jnp.dot(qt_scr[d, pl.ds(r0, C), :], st.T.astype(BF16),
                                                     preferred_element_type=F32)
                st_scr[d] = dl_scr[d, pl.ds(c, 1), :] * st + u_scr[d, c]
        return carry

    assert nc % HG_GROUP == 0 and nc_ctx % HG_GROUP == 0
    lax.fori_loop(0, nc // HG_GROUP, scan, 0)

    o = (oi_scr[0] + ox_scr[0]) + (oi_scr[1] + ox_scr[1])
    y = o * lax.rsqrt(jnp.mean(o * o, axis=-1, keepdims=True) + EPS) * g_ref[...]
    gate = gate_ref[0].astype(F32)
    o_ref[0] = (y * (gate * _sigmoid(gate))).astype(o_ref.dtype)


def _hgrn(p, pf, lb, norm_g, n_ctx, part0):
    B, T, _ = p.shape
    H = HG_HEADS
    nc = T // HG_CHUNK
    part = lambda k: (lambda b, h: (b, 0, k * H + h))
    blk = (1, T, HEAD_W)
    return pl.pallas_call(
        functools.partial(_hgrn_kernel, n_ctx=n_ctx),
        grid=(B, H),
        in_specs=[pl.BlockSpec(blk, part(part0)), pl.BlockSpec(blk, part(part0 + 1)),
                  pl.BlockSpec(blk, part(part0 + 2)),
                  pl.BlockSpec(blk, part(0)), pl.BlockSpec(blk, part(1)),
                  pl.BlockSpec((1, 2, HG_K), lambda b, h: (h, 0, 0)),
                  pl.BlockSpec((1, HEAD_W), lambda b, h: (0, 0))],
        out_specs=pl.BlockSpec(blk, lambda b, h: (b, 0, h)),
        out_shape=jax.ShapeDtypeStruct((B, T, H * HEAD_W), BF16),
        scratch_shapes=[pltpu.VMEM((2, T, HG_K), BF16),
                        pltpu.VMEM((2, T, HEAD_W), F32), pltpu.VMEM((2, T, HEAD_W), F32),
                        pltpu.VMEM((2, nc, HG_K), F32),
                        pltpu.VMEM((2, nc, HEAD_W, HG_K), F32),
                        pltpu.VMEM((2, HEAD_W, HG_K), F32),
                        pltpu.VMEM((2 * HG_K, 2 * HG_K), BF16),
                        pltpu.VMEM((_off_diag_rows(HG_CHUNK), _off_diag_rows(HG_CHUNK)), F32)],
        compiler_params=_cparams(("parallel", "parallel")),
        name="hgrn2",
    )(p, p, p, pf, pf, lb, norm_g.reshape(1, HEAD_W))


def _shift_rows(x, d):
    n = x.shape[0]
    row = lax.broadcasted_iota(jnp.int32, (n, 1), 0)
    rolled = pltpu.roll(x, (-d) % n, axis=0)
    keep = (row + d >= 0) & (row + d < n)
    return jnp.where(keep, rolled, 0.0)


def _pool_minus_identity(x, w):
    n = x.shape[0]
    ahead = w - w // 2
    behind = w // 2
    fwd = x
    span = 1
    while span < ahead:
        fwd = fwd + _shift_rows(fwd, span)
        span *= 2
    bwd = x
    span = 1
    while span < behind:
        bwd = bwd + _shift_rows(bwd, -span)
        span *= 2
    total = fwd + _shift_rows(bwd, -1)
    row = lax.broadcasted_iota(jnp.int32, (n, 1), 0)
    cnt = jnp.minimum(row + ahead, n) - jnp.maximum(row - behind, 0)
    return total / cnt.astype(F32) - x


def _pool_kernel(u_ref, z_ref, w_ref, ls_ref, o_ref, r_scr, *, n_ctx):
    j = pl.program_id(1)
    T = u_ref.shape[1]
    for jj, win in enumerate(POOL_WINDOWS):
        @pl.when(j == jj)
        def _(win=win):
            for lo, hi in ((0, n_ctx), (n_ctx, T)):
                r_scr[lo:hi, :] = _pool_minus_identity(u_ref[0, lo:hi, :].astype(F32), win).astype(BF16)

    sub = POOL_SUB if T % POOL_SUB == 0 else T
    project = lambda r0: jnp.dot(r_scr[r0:r0 + sub, :], w_ref[0], preferred_element_type=F32)
    y_next = project(0)
    for r0 in range(0, T, sub):
        y = y_next
        if r0 + sub < T:
            y_next = project(r0 + sub)
        z = z_ref[0, r0:r0 + sub, :].astype(F32)
        o_ref[0, r0:r0 + sub, :] = (y * ls_ref[...] * (z * _sigmoid(z))).astype(o_ref.dtype)


def _pool_mixer(p, w_pool, ls, n_ctx):
    B, T, two_e = p.shape
    G, gw, _ = w_pool.shape
    assert G == len(POOL_WINDOWS) and two_e == 2 * G * gw
    return pl.pallas_call(
        functools.partial(_pool_kernel, n_ctx=n_ctx),
        grid=(B, G),
        in_specs=[pl.BlockSpec((1, T, gw), lambda b, j: (b, 0, j)),
                  pl.BlockSpec((1, T, gw), lambda b, j: (b, 0, G + j)),
                  pl.BlockSpec((1, gw, gw), lambda b, j: (j, 0, 0)),
                  pl.BlockSpec((1, gw), lambda b, j: (0, j))],
        out_specs=pl.BlockSpec((1, T, gw), lambda b, j: (b, 0, j)),
        out_shape=jax.ShapeDtypeStruct((B, T, G * gw), BF16),
        scratch_shapes=[pltpu.VMEM((T, gw), BF16)],
        compiler_params=_cparams(("parallel", "parallel")),
        name="pool_mixer",
    )(p, p, w_pool, ls.reshape(1, G * gw))


def _out_proj_kernel(a_ref, b_ref, w_ref, *refs, n_x, tm, n_ctx, row_off):
    x_refs = refs[:n_x]
    mod_ref, g_ref, o_ref = refs[n_x:]
    ka = a_ref.shape[2]
    sub = OUT_SUB if tm % OUT_SUB == 0 else tm
    sources = _tile_sources(x_refs, tm, n_ctx)
    assert all(n_rows % sub == 0 for _, _, n_rows in sources)

    def residual(r0):
        for ref, first_tile_ref, n_rows in sources:
            if r0 < n_rows:
                x = ref[0, r0:r0 + sub, :]
                if first_tile_ref is not None:
                    x = jnp.where(pl.program_id(1) == 0, first_tile_ref[0, r0:r0 + sub, :], x)
                return x
            r0 -= n_rows

    def project(r0):
        y = jnp.dot(a_ref[0, r0:r0 + sub, :], w_ref[:ka, :], preferred_element_type=F32)
        return y + jnp.dot(b_ref[0, r0:r0 + sub, :], w_ref[ka:, :], preferred_element_type=F32)

    y_next = project(0)
    for r0 in range(0, tm, sub):
        y = y_next
        if r0 + sub < tm:
            y_next = project(r0 + sub)
        yn = y * lax.rsqrt(jnp.mean(y * y, axis=-1, keepdims=True) + EPS) * g_ref[...]
        gt = _row_mod(mod_ref, 2, 5, row_off + pl.program_id(1) * tm + r0, sub, n_ctx)
        o_ref[0, r0:r0 + sub, :] = residual(r0) + gt * yn


def _out_proj(a, b, a_blk, b_blk, w, x, modrows, g, n_ctx, tm, latents_only):
    off = n_ctx // tm if latents_only else 0
    x_ops, x_specs, (B, T, D) = _row_operands(x, tm, lambda bb, r: (bb, r + off), n_ctx)
    half_k = w.shape[0] // 2
    rows_out = T - n_ctx if latents_only else T
    return pl.pallas_call(
        functools.partial(_out_proj_kernel, n_x=len(x_ops), tm=tm, n_ctx=n_ctx, row_off=off * tm),
        grid=(B, rows_out // tm),
        in_specs=[pl.BlockSpec((1, tm, half_k), lambda bb, r: (bb, r + off, a_blk)),
                  pl.BlockSpec((1, tm, half_k), lambda bb, r: (bb, r + off, b_blk)),
                  pl.BlockSpec(w.shape, lambda bb, r: (0, 0), pipeline_mode=pl.Buffered(1))]
        + x_specs + [
                  pl.BlockSpec((1, SUBLANES, D), lambda bb, r: (bb, 0, 0)),
                  pl.BlockSpec((1, D), lambda bb, r: (0, 0))],
        out_specs=pl.BlockSpec((1, tm, D), lambda bb, r: (bb, r, 0)),
        out_shape=jax.ShapeDtypeStruct((B, rows_out, D), F32),
        compiler_params=_cparams(("parallel", "parallel")),
        name="out_proj",
    )(a, b, w, *x_ops, modrows, g.reshape(1, D))


def _row_tile(T, cap):
    best = NORM_SLAB
    for t in range(NORM_SLAB, min(T, cap) + 1, NORM_SLAB):
        if T % t == 0:
            best = t
    return best


def _col_tile(N, cap):
    best = 128
    for t in range(128, min(N, cap) + 1, 128):
        if N % t == 0:
            best = t
    return best


def kernel(x, c, ctx, c_ctx, w_mod, b_mod, g_pre, g_post, ev_w_in, ev_w_out, ev_lambda, ev_subln_g,
           ev_hg_lb_logits, ev_hg_norm_g, od_w_in, od_w_pool, od_scale, od_w_out):
    B, S, D = x.shape
    n_ctx = ctx.shape[1]
    depth = w_mod.shape[0]
    T = n_ctx + S
    W = DA_HEADS * HEAD_W

    xc = (ctx, x)
    tm_pair = max(t for t in range(n_ctx, 768 + 1, n_ctx) if T % t == 0)

    n_rows = -(-(B + 1) // SUBLANES) * SUBLANES
    cvec = jnp.zeros((n_rows, D), F32).at[:B].set(c).at[B].set(c_ctx)
    mod = _modulation(cvec, w_mod, b_mod)

    lb_cum = jnp.cumsum(jax.nn.softmax(ev_hg_lb_logits.astype(F32), axis=0), axis=0)
    lb_all = lb_cum - lb_cum[0]
    cos, sin = _rope_tables(n_ctx, S)

    tm_in = _row_tile(T, 1152)
    tm_out = _row_tile(T, 768)

    for l in range(depth):
        last = l == depth - 1
        tm_l, tm_o = (tm_in, tm_pair) if isinstance(xc, tuple) else (tm_in, n_ctx if last else tm_out)
        ml = mod[l]
        lat = ml[:B].reshape(B, 3, D)
        cx = jnp.broadcast_to(ml[B].reshape(1, 3, D), (B, 3, D))
        modrows = jnp.concatenate([lat, cx, jnp.zeros((B, SUBLANES - 6, D), F32)], axis=1)

        if l % 2 == 0:
            e = l // 2
            assert _col_tile(W, 1024) == W
            p, pf = _norm_proj(xc, modrows, g_pre[l], ev_w_in[e].astype(BF16), 2 * W, n_ctx, tm_l, W,
                               col_order=(0, 1, 2, 3, 4, 7, 8, 5, 6))

            lam_init = 0.8 - 0.6 * math.exp(-0.3 * l)
            lv = ev_lambda[e].astype(F32)
            lam = jnp.exp(jnp.sum(lv[0] * lv[1])) - jnp.exp(jnp.sum(lv[2] * lv[3])) + lam_init
            a = _diff_attention(p, lam.reshape(1), cos, sin, ev_subln_g[e], 1.0 - lam_init, n_ctx)
            lb = lb_all[e].reshape(2, HG_HEADS, HG_K).transpose(1, 0, 2)
            bh = _hgrn(p, pf, lb, ev_hg_norm_g[e], n_ctx, 4)
            w_out = ev_w_out[e].astype(BF16)
            ya, yb, ia, ib = a, bh, 0, 0
        else:
            o = l // 2
            p = _norm_proj(xc, modrows, g_pre[l], od_w_in[o].astype(BF16), 0, n_ctx, tm_l,
                           _col_tile(od_w_in.shape[2], 2048))
            y = _pool_mixer(p, od_w_pool[o].astype(BF16), od_scale[o], n_ctx)
            w_out = od_w_out[o].astype(BF16)
            ya, yb, ia, ib = y, y, 0, 1

        xc = _out_proj(ya, yb, ia, ib, w_out, xc, modrows, g_post[l], n_ctx, tm_o, latents_only=last)
    return xc
```

```python
import functools
import math

import jax
import jax.numpy as jnp
import numpy as np
from jax import lax
from jax.experimental import pallas as pl
from jax.experimental.pallas import tpu as pltpu

DA_HEADS = 8
DA_D = 64
HEAD_W = 2 * DA_D
HG_HEADS = 8
HG_K = 128
GRID_W = 64
ROPE_BASE = 10000.0
EPS = 1e-6
POOL_WINDOWS = (2, 4, 8, 16)
HG_CHUNK = 32
LOG2E = math.log2(math.e)
ATTN_Q_BLOCK = 2304
ATTN_Q_SUB = 128
ATTN_AHEAD = 8
HG_GROUP = 8
HG_PREP_UNROLL = 12
SUBLANES = 8
POOL_SUB = 768
OUT_SUB = 256
NORM_SLAB = 16
NORM_UNROLL = 4

F32 = jnp.float32
BF16 = jnp.bfloat16
VMEM_LIMIT = 56 * 1024 * 1024


def _cparams(sem):
    return pltpu.CompilerParams(dimension_semantics=sem, vmem_limit_bytes=VMEM_LIMIT)


def _sigmoid(x):
    return 1.0 / (1.0 + jnp.exp(-x))


def _mod_kernel(c_ref, w_ref, b_ref, o_ref):
    c = c_ref[...]
    s = c * _sigmoid(c)
    o_ref[0] = jnp.dot(s, w_ref[0], preferred_element_type=F32,
                       precision=lax.Precision.HIGHEST) + b_ref[0]


def _modulation(cvec, w_mod, b_mod):
    L, D, N = w_mod.shape
    R = cvec.shape[0]
    tn = 1024 if N % 1024 == 0 else N
    return pl.pallas_call(
        _mod_kernel,
        grid=(L, N // tn),
        in_specs=[pl.BlockSpec((R, D), lambda l, j: (0, 0)),
                  pl.BlockSpec((1, D, tn), lambda l, j: (l, 0, j)),
                  pl.BlockSpec((1, 1, tn), lambda l, j: (l, 0, j))],
        out_specs=pl.BlockSpec((1, R, tn), lambda l, j: (l, 0, j)),
        out_shape=jax.ShapeDtypeStruct((L, R, N), F32),
        compiler_params=_cparams(("parallel", "parallel")),
        name="modulation",
    )(cvec, w_mod, b_mod.reshape(L, 1, N))


def _row_mod(mod_ref, lat_row, ctx_row, row0, tm, n_ctx):
    rows = row0 + lax.broadcasted_iota(jnp.int32, (tm, 1), 0)
    return jnp.where(rows < n_ctx, mod_ref[0, ctx_row:ctx_row + 1, :], mod_ref[0, lat_row:lat_row + 1, :])


def _tile_sources(x_refs, tm, n_ctx):
    if len(x_refs) == 1:
        return [(x_refs[0], None, tm)]
    src_rows = x_refs[0].shape[1]
    n_ctx_src = n_ctx // src_rows
    return [(ref, x_refs[i] if i < n_ctx_src else None, src_rows) for i, ref in enumerate(x_refs[n_ctx_src:])]


def _norm_proj_kernel(*refs, n_x, tm, n_ctx, n_lo_tiles):
    x_refs = refs[:n_x]
    mod_ref, g_ref, w_ref, o_ref = refs[n_x:n_x + 4]
    rest = refs[n_x + 4:]
    h_scr, gs_scr = rest[-2:]
    r = pl.program_id(1)
    j = pl.program_id(2)

    @pl.when(j == 0)
    def _():
        gs_scr[0:1, :] = g_ref[...] * (1.0 + mod_ref[0, 1:2, :])
        gs_scr[1:2, :] = mod_ref[0, 0:1, :]
        gs_scr[2:3, :] = g_ref[...] * (1.0 + mod_ref[0, 4:5, :])
        gs_scr[3:4, :] = mod_ref[0, 3:4, :]

        row0 = 0
        for ref, first_tile_ref, n_rows in _tile_sources(x_refs, tm, n_ctx):
            def slab(k, carry, ref=ref, first_tile_ref=first_tile_ref, row0=row0):
                r0 = pl.multiple_of(k * NORM_SLAB, NORM_SLAB)
                x = ref[0, pl.ds(r0, NORM_SLAB), :]
                if first_tile_ref is not None:
                    x = jnp.where(r == 0, first_tile_ref[0, pl.ds(r0, NORM_SLAB), :], x)
                base = jnp.where(r * tm + row0 + r0 < n_ctx, 2, 0)
                y = x * lax.rsqrt(jnp.mean(x * x, axis=-1, keepdims=True) + EPS)
                h_scr[pl.ds(row0 + r0, NORM_SLAB), :] = (y * gs_scr[pl.ds(base, 1), :]
                                                         + gs_scr[pl.ds(base + 1, 1), :]).astype(BF16)
                return carry

            lax.fori_loop(0, n_rows // NORM_SLAB, slab, 0, unroll=NORM_UNROLL)
            row0 += n_rows

    if len(rest) == 2:
        o_ref[0] = jnp.dot(h_scr[...], w_ref[...], preferred_element_type=F32).astype(o_ref.dtype)
    else:
        @pl.when(j < n_lo_tiles)
        def _():
            o_ref[0] = jnp.dot(h_scr[...], w_ref[...], preferred_element_type=F32).astype(o_ref.dtype)

        @pl.when(j >= n_lo_tiles)
        def _():
            rest[0][0] = jnp.dot(h_scr[...], w_ref[...], preferred_element_type=F32)


def _row_operands(x, tm, tile_of, src_rows):
    if not isinstance(x, tuple):
        B, T, D = x.shape
        return [x], [pl.BlockSpec((1, tm, D), lambda *g: (*tile_of(*g), 0))], (B, T, D)
    ctx, lat = x
    B, n_ctx, D = ctx.shape
    assert tm % src_rows == 0 and n_ctx % src_rows == 0 and lat.shape[1] % src_rows == 0 and tm >= n_ctx
    q = tm // src_rows
    n_ctx_src = n_ctx // src_rows

    def ctx_block(i):
        return pl.BlockSpec((1, src_rows, D), lambda *g: (tile_of(*g)[0], i, 0))

    def lat_block(i):
        def index(*g):
            b, r = tile_of(*g)
            return b, jnp.maximum(q * r + i - n_ctx_src, 0), 0
        return pl.BlockSpec((1, src_rows, D), index)

    specs = [ctx_block(i) for i in range(n_ctx_src)] + [lat_block(i) for i in range(q)]
    return [ctx] * n_ctx_src + [lat] * q, specs, (B, n_ctx + lat.shape[1], D)


def _norm_proj(x, modrows, g, w, n_f32, n_ctx, tm, tn, col_order=None):
    x_ops, x_specs, (B, T, D) = _row_operands(x, tm, lambda b, r, j: (b, r), math.gcd(tm, n_ctx))
    n_lo = w.shape[1] - n_f32
    assert n_lo % tn == 0 and n_f32 % tn == 0
    lo_tiles, hi_tiles = n_lo // tn, n_f32 // tn
    if col_order is None:
        w_tile = lambda j: j
    else:
        assert sorted(col_order) == list(range(lo_tiles + hi_tiles))
        w_tile = lambda j: sum(jnp.where(j == k, src, 0) for k, src in enumerate(col_order))
    out_specs = [pl.BlockSpec((1, tm, tn), lambda b, r, j: (b, r, jnp.minimum(j, lo_tiles - 1)))]
    out_shape = [jax.ShapeDtypeStruct((B, T, n_lo), BF16)]
    if hi_tiles:
        out_specs.append(pl.BlockSpec((1, tm, tn), lambda b, r, j: (b, r, jnp.maximum(j - lo_tiles, 0))))
        out_shape.append(jax.ShapeDtypeStruct((B, T, n_f32), F32))
    outs = pl.pallas_call(
        functools.partial(_norm_proj_kernel, n_x=len(x_ops), tm=tm, n_ctx=n_ctx, n_lo_tiles=lo_tiles),
        grid=(B, T // tm, lo_tiles + hi_tiles),
        in_specs=x_specs + [
                  pl.BlockSpec((1, SUBLANES, D), lambda b, r, j: (b, 0, 0)),
                  pl.BlockSpec((1, D), lambda b, r, j: (0, 0)),
                  pl.BlockSpec((D, tn), lambda b, r, j: (0, w_tile(j)))],
        out_specs=out_specs,
        out_shape=out_shape,
        scratch_shapes=[pltpu.VMEM((tm, D), BF16), pltpu.VMEM((4, D), F32)],
        compiler_params=_cparams(("parallel", "parallel", "arbitrary")),
        name="norm_proj",
    )(*x_ops, modrows, g.reshape(1, D), w)
    return outs if hi_tiles else outs[0]


def _rope(x, cos, sin_signed):
    lane = lax.broadcasted_iota(jnp.int32, x.shape, 1)
    n = x.shape[1]
    partner = jnp.where(lane % 32 < 16, pltpu.roll(x, n - 16, axis=1), pltpu.roll(x, 16, axis=1))
    return x * cos + partner * sin_signed


def _attn_kernel(lam_ref, q_ref, k_ref, v_ref, gate_ref, cq_ref, sq_ref, ck_ref, sk_ref, g_ref,
                 o_ref, k_scr, vt_scr, *, tq, n_ctx, out_scale):
    i = pl.program_id(2)
    n_keys = k_scr.shape[0]

    @pl.when(i == 0)
    def _():
        k_scr[...] = _rope(k_ref[0].astype(F32), ck_ref[...], sk_ref[...]).astype(BF16)
        vt_scr[...] = v_ref[0].astype(F32).T.astype(BF16)

    qr = _rope(q_ref[0].astype(F32), cq_ref[...], sq_ref[...]) * (DA_D ** -0.5 * LOG2E)
    first_map = lax.broadcasted_iota(jnp.int32, qr.shape, 1) < DA_D
    q_maps = (jnp.where(first_map, qr, 0.0).astype(BF16), jnp.where(first_map, 0.0, qr).astype(BF16))
    lam = lam_ref[0]
    qs = ATTN_Q_SUB

    def attend(n_ctx_rows):
        def scores(r0):
            nk = n_ctx if r0 < n_ctx_rows else n_keys
            q2 = jnp.concatenate([q_maps[0][r0:r0 + qs], q_maps[1][r0:r0 + qs]], axis=0)
            return lax.dot_general(k_scr[:nk, :], q2, (((1,), (1,)), ((), ())),
                                   preferred_element_type=F32)

        starts = list(range(0, tq, qs))
        pending = [scores(r0) for r0 in starts[:ATTN_AHEAD]]
        for n, r0 in enumerate(starts):
            rows = slice(r0, r0 + qs)
            s = pending.pop(0)
            nk = s.shape[0]
            if n + ATTN_AHEAD < len(starts):
                pending.append(scores(starts[n + ATTN_AHEAD]))
            e = jnp.exp2(s - jnp.max(s, axis=0, keepdims=True))
            rinv = 1.0 / jnp.sum(e, axis=0, keepdims=True)
            ov = jnp.dot(vt_scr[:, :nk], e.astype(BF16), preferred_element_type=F32) * rinv
            o = (ov[:, :qs] - lam * ov[:, qs:]).T
            y = o * lax.rsqrt(jnp.mean(o * o, axis=-1, keepdims=True) + EPS) * g_ref[...] * out_scale
            gate = gate_ref[0, rows, :].astype(F32)
            o_ref[0, rows, :] = (y * (gate * _sigmoid(gate))).astype(o_ref.dtype)

    @pl.when(i == 0)
    def _():
        attend(n_ctx)

    @pl.when(i != 0)
    def _():
        attend(0)


def _diff_attention(p, lam, cos, sin, subln_g, out_scale, n_ctx):
    B, T, _ = p.shape
    tq = max(t for t in range(n_ctx, ATTN_Q_BLOCK + 1, n_ctx) if T % t == 0)
    assert n_ctx % ATTN_Q_SUB == 0
    H = DA_HEADS
    head = lambda part: (lambda b, h, i: (b, i, part * H + h))
    head_all = lambda part: (lambda b, h, i: (b, 0, part * H + h))
    return pl.pallas_call(
        functools.partial(_attn_kernel, tq=tq, n_ctx=n_ctx, out_scale=out_scale),
        grid=(B, H, T // tq),
        in_specs=[pl.BlockSpec(memory_space=pltpu.SMEM),
                  pl.BlockSpec((1, tq, HEAD_W), head(0)),
                  pl.BlockSpec((1, T, HEAD_W), head_all(1)),
                  pl.BlockSpec((1, T, HEAD_W), head_all(2)),
                  pl.BlockSpec((1, tq, HEAD_W), head(3)),
                  pl.BlockSpec((tq, HEAD_W), lambda b, h, i: (i, 0)),
                  pl.BlockSpec((tq, HEAD_W), lambda b, h, i: (i, 0)),
                  pl.BlockSpec((T, HEAD_W), lambda b, h, i: (0, 0)),
                  pl.BlockSpec((T, HEAD_W), lambda b, h, i: (0, 0)),
                  pl.BlockSpec((1, HEAD_W), lambda b, h, i: (0, 0))],
        out_specs=pl.BlockSpec((1, tq, HEAD_W), lambda b, h, i: (b, i, h)),
        out_shape=jax.ShapeDtypeStruct((B, T, H * HEAD_W), BF16),
        scratch_shapes=[pltpu.VMEM((T, HEAD_W), BF16), pltpu.VMEM((HEAD_W, T), BF16)],
        compiler_params=_cparams(("parallel", "parallel", "arbitrary")),
        name="diff_attention",
    )(lam, p, p, p, p, cos, sin, cos, sin, subln_g.reshape(1, HEAD_W))


def _rope_tables(n_ctx, seq):
    n_freq = DA_D // 4
    inv = np.power(np.float32(ROPE_BASE), -np.arange(n_freq, dtype=np.float32) / np.float32(n_freq))
    t = np.arange(seq)
    lane = np.arange(HEAD_W)
    pos = np.where((lane % DA_D < DA_D // 2)[None, :], (t // GRID_W)[:, None], (t % GRID_W)[:, None]).astype(np.float32)
    ang = (pos * inv[lane % n_freq][None, :]).astype(np.float64)
    sign = np.where(lane % 32 < 16, -1.0, 1.0)[None, :]
    cos = np.concatenate([np.ones((n_ctx, HEAD_W)), np.cos(ang)], axis=0).astype(np.float32)
    sin = np.concatenate([np.zeros((n_ctx, HEAD_W)), np.sin(ang) * sign], axis=0).astype(np.float32)
    return jnp.asarray(cos), jnp.asarray(sin)


def _chunk_cumsum(g, reverse):
    n = g.shape[0]
    row = lax.broadcasted_iota(jnp.int32, g.shape, 0)
    b = g
    sh = 1
    while sh < n:
        if reverse:
            b = b + jnp.where(row < n - sh, pltpu.roll(b, n - sh, axis=0), 0.0)
        else:
            b = b + jnp.where(row >= sh, pltpu.roll(b, sh, axis=0), 0.0)
        sh *= 2
    return b


def _diag_tiles(qq, b2, c2, reverse):
    half = SUBLANES
    sub = lax.broadcasted_iota(jnp.int32, (half, qq.shape[1]), 0)
    tiles = []
    for s in range(qq.shape[0]):
        sb = s // half
        rows = slice(sb * half, (sb + 1) * half)
        w = qq[rows] * jnp.exp2(b2[rows] - c2[s:s + 1])
        keep = (sub <= s - sb * half) if reverse else (sub >= s - sb * half)
        tiles.append((s, sb, jnp.where(keep, w, 0.0)))
    return tiles


def _off_diag_pieces(qq, b2, c2, reverse):
    C = qq.shape[0]
    pieces = []
    span = SUBLANES
    while span < C:
        for u in range(0, C, 2 * span):
            lo, hi = slice(u, u + span), slice(u + span, u + 2 * span)
            if reverse:
                q_rows, k_rows, ref = lo, hi, b2[u + span:u + span + 1]
            else:
                q_rows, k_rows, ref = hi, lo, b2[u + span - 1:u + span]
            pieces.append((q_rows, k_rows, qq[q_rows] * jnp.exp2(b2[q_rows] - ref), jnp.exp2(ref - c2[k_rows])))
        span *= 2
    return pieces


def _off_diag_rows(chunk):
    levels = int(math.log2(chunk // SUBLANES))
    return chunk * levels


def _lane_sums(tiles, ones_ref):
    n = len(tiles)
    assert n % 2 == 0
    lhs = jnp.concatenate([jnp.concatenate([tiles[2 * i], tiles[2 * i + 1]], axis=1) for i in range(n // 2)], axis=0)
    r = jnp.dot(lhs.astype(BF16), ones_ref[...], preferred_element_type=F32)
    w = tiles[0].shape[1]
    out = []
    for i in range(n // 2):
        rows = slice(i * SUBLANES, (i + 1) * SUBLANES)
        out += [r[rows, :w], r[rows, w:]]
    return out


def _hgrn_kernel(q_ref, v_ref, gate_ref, zf_ref, zb_ref, lb_ref, g_ref, o_ref,
                 qt_scr, oi_scr, ox_scr, dl_scr, u_scr, st_scr, ones_scr, blk_scr, *, n_ctx):
    T = q_ref.shape[1]
    C = HG_CHUNK
    K = HG_K
    nc = T // C
    nc_ctx = n_ctx // C
    same_half = (lax.broadcasted_iota(jnp.int32, (2 * K, 2 * K), 0) < K) == (lax.broadcasted_iota(jnp.int32, (2 * K, 2 * K), 1) < K)
    ones_scr[...] = jnp.where(same_half, 1.0, 0.0).astype(BF16)
    sizes = []
    span = SUBLANES
    while span < C:
        sizes += [span] * (C // (2 * span))
        span *= 2
    bounds = [sum((sizes * 2)[:i]) for i in range(1, 2 * len(sizes))]
    n_off = blk_scr.shape[0]
    row_i = lax.broadcasted_iota(jnp.int32, (n_off, n_off), 0)
    col_i = lax.broadcasted_iota(jnp.int32, (n_off, n_off), 1)
    piece_of_row = sum((row_i >= b).astype(jnp.int32) for b in bounds)
    piece_of_col = sum((col_i >= b).astype(jnp.int32) for b in bounds)
    blk_scr[...] = jnp.where(piece_of_row == piece_of_col, 1.0, 0.0)

    def stage1(c):
        r0 = pl.multiple_of(c * C, C)
        hq = q_ref[0, pl.ds(r0, C), :].astype(F32)
        qq = hq * _sigmoid(hq)
        vb = v_ref[0, pl.ds(r0, C), :]
        v = vb.astype(F32)
        tiles, keys, pieces = [], [], []
        for d, z_ref in enumerate((zf_ref, zb_ref)):
            reverse = d == 1
            z = z_ref[0, pl.ds(r0, C), :]
            lb = lb_ref[0, d:d + 1, :]
            e = jnp.exp(-jnp.abs(z))
            inv = 1.0 / (1.0 + e)
            sig = jnp.where(z >= 0, inv, e * inv)
            nsig = jnp.where(z >= 0, e * inv, inv)
            lk2 = jnp.log2(1.0 - lb) + jnp.log2(nsig)
            b2 = _chunk_cumsum(jnp.log2(lb + (1.0 - lb) * sig), reverse)
            b2_last = b2[0:1] if reverse else b2[C - 1:C]
            qt_scr[d, pl.ds(r0, C), :] = (qq * jnp.exp2(b2)).astype(BF16)
            keys.append(jnp.exp2(lk2 + (b2_last - b2)).astype(BF16))
            dl_scr[d, pl.ds(c, 1), :] = jnp.exp2(b2_last)
            tiles.append(_diag_tiles(qq, b2, b2 - lk2, reverse))
            pieces.append(_off_diag_pieces(qq, b2, b2 - lk2, reverse))
        sums = _lane_sums([w for per_dir in tiles for _, _, w in per_dir], ones_scr)
        q_off = jnp.concatenate([p[2] for per_dir in pieces for p in per_dir], axis=0).astype(BF16)
        k_off = jnp.concatenate([p[3] for per_dir in pieces for p in per_dir], axis=0).astype(BF16)
        cross = lax.dot_general(q_off, k_off, (((1,), (1,)), ((), ())), preferred_element_type=F32)
        upd = lax.dot_general(vb, jnp.concatenate(keys, axis=1), (((0,), (0,)), ((), ())),
                              preferred_element_type=F32)
        u_scr[0, c] = upd[:, :K]
        u_scr[1, c] = upd[:, K:]
        return r0, v, tiles, sums, pieces, cross

    def stage2(r0, v, tiles, sums, pieces, cross):
        v_off = jnp.concatenate([v[p[1]] for per_dir in pieces for p in per_dir], axis=0).astype(BF16)
        contrib = jnp.dot((cross * blk_scr[...]).astype(BF16), v_off, preferred_element_type=F32)
        k = 0
        off = 0
        for d in range(2):
            parts = [jnp.zeros((SUBLANES, v.shape[1]), F32) for _ in range(C // SUBLANES)]
            for s, blk, _ in tiles[d]:
                parts[blk] = parts[blk] + sums[k] * v[s:s + 1]
                k += 1
            for q_rows, _, qp, _ in pieces[d]:
                for i in range(qp.shape[0] // SUBLANES):
                    blk = q_rows.start // SUBLANES + i
                    parts[blk] = parts[blk] + contrib[off:off + SUBLANES]
                    off += SUBLANES
            oi_scr[d, pl.ds(r0, C), :] = jnp.concatenate(parts, axis=0)

    def prep(t, carry):
        staged = [stage1(t * HG_PREP_UNROLL + j) for j in range(HG_PREP_UNROLL)]
        for args in staged:
            stage2(*args)
        return carry

    assert nc % HG_PREP_UNROLL == 0
    lax.fori_loop(0, nc // HG_PREP_UNROLL, prep, 0)

    st_scr[...] = jnp.zeros_like(st_scr)

    def scan(m, carry):
        i0 = m * HG_GROUP
        cb0 = jnp.where(i0 < nc_ctx, nc_ctx - 1 - i0, nc + nc_ctx - 1 - i0)
        for j in range(HG_GROUP):
            for d, c in enumerate((i0 + j, cb0 - j)):
                r0 = pl.multiple_of(c * C, C)
                st = st_scr[d]
                ox_scr[d, pl.ds(r0, C), :] = jnp.dot(qt_scr[d, pl.ds(r0, C), :], st.T.astype(BF16),
                                                     preferred_element_type=F32)
                st_scr[d] = dl_scr[d, pl.ds(c, 1), :] * st + u_scr[d, c]
        return carry

    assert nc % HG_GROUP == 0 and nc_ctx % HG_GROUP == 0
    lax.fori_loop(0, nc // HG_GROUP, scan, 0)

    o = (oi_scr[0] + ox_scr[0]) + (oi_scr[1] + ox_scr[1])
    y = o * lax.rsqrt(jnp.mean(o * o, axis=-1, keepdims=True) + EPS) * g_ref[...]
    gate = gate_ref[0].astype(F32)
    o_ref[0] = (y * (gate * _sigmoid(gate))).astype(o_ref.dtype)


def _hgrn(p, pf, lb, norm_g, n_ctx, part0):
    B, T, _ = p.shape
    H = HG_HEADS
    nc = T // HG_CHUNK
    part = lambda k: (lambda b, h: (b, 0, k * H + h))
    blk = (1, T, HEAD_W)
    return pl.pallas_call(
        functools.partial(_hgrn_kernel, n_ctx=n_ctx),
        grid=(B, H),
        in_specs=[pl.BlockSpec(blk, part(part0)), pl.BlockSpec(blk, part(part0 + 1)),
                  pl.BlockSpec(blk, part(part0 + 2)),
                  pl.BlockSpec(blk, part(0)), pl.BlockSpec(blk, part(1)),
                  pl.BlockSpec((1, 2, HG_K), lambda b, h: (h, 0, 0)),
                  pl.BlockSpec((1, HEAD_W), lambda b, h: (0, 0))],
        out_specs=pl.BlockSpec(blk, lambda b, h: (b, 0, h)),
        out_shape=jax.ShapeDtypeStruct((B, T, H * HEAD_W), BF16),
        scratch_shapes=[pltpu.VMEM((2, T, HG_K), BF16),
                        pltpu.VMEM((2, T, HEAD_W), F32), pltpu.VMEM((2, T, HEAD_W), F32),
                        pltpu.VMEM((2, nc, HG_K), F32),
                        pltpu.VMEM((2, nc, HEAD_W, HG_K), F32),
                        pltpu.VMEM((2, HEAD_W, HG_K), F32),
                        pltpu.VMEM((2 * HG_K, 2 * HG_K), BF16),
                        pltpu.VMEM((_off_diag_rows(HG_CHUNK), _off_diag_rows(HG_CHUNK)), F32)],
        compiler_params=_cparams(("parallel", "parallel")),
        name="hgrn2",
    )(p, p, p, pf, pf, lb, norm_g.reshape(1, HEAD_W))


def _shift_rows(x, d):
    n = x.shape[0]
    row = lax.broadcasted_iota(jnp.int32, (n, 1), 0)
    rolled = pltpu.roll(x, (-d) % n, axis=0)
    keep = (row + d >= 0) & (row + d < n)
    return jnp.where(keep, rolled, 0.0)


def _pool_minus_identity(x, w):
    n = x.shape[0]
    ahead = w - w // 2
    behind = w // 2
    fwd = x
    span = 1
    while span < ahead:
        fwd = fwd + _shift_rows(fwd, span)
        span *= 2
    bwd = x
    span = 1
    while span < behind:
        bwd = bwd + _shift_rows(bwd, -span)
        span *= 2
    total = fwd + _shift_rows(bwd, -1)
    row = lax.broadcasted_iota(jnp.int32, (n, 1), 0)
    cnt = jnp.minimum(row + ahead, n) - jnp.maximum(row - behind, 0)
    return total / cnt.astype(F32) - x


def _pool_kernel(u_ref, z_ref, w_ref, ls_ref, o_ref, r_scr, *, n_ctx):
    j = pl.program_id(1)
    T = u_ref.shape[1]
    for jj, win in enumerate(POOL_WINDOWS):
        @pl.when(j == jj)
        def _(win=win):
            for lo, hi in ((0, n_ctx), (n_ctx, T)):
                r_scr[lo:hi, :] = _pool_minus_identity(u_ref[0, lo:hi, :].astype(F32), win).astype(BF16)

    sub = POOL_SUB if T % POOL_SUB == 0 else T
    project = lambda r0: jnp.dot(r_scr[r0:r0 + sub, :], w_ref[0], preferred_element_type=F32)
    y_next = project(0)
    for r0 in range(0, T, sub):
        y = y_next
        if r0 + sub < T:
            y_next = project(r0 + sub)
        z = z_ref[0, r0:r0 + sub, :].astype(F32)
        o_ref[0, r0:r0 + sub, :] = (y * ls_ref[...] * (z * _sigmoid(z))).astype(o_ref.dtype)


def _pool_mixer(p, w_pool, ls, n_ctx):
    B, T, two_e = p.shape
    G, gw, _ = w_pool.shape
    assert G == len(POOL_WINDOWS) and two_e == 2 * G * gw
    return pl.pallas_call(
        functools.partial(_pool_kernel, n_ctx=n_ctx),
        grid=(B, G),
        in_specs=[pl.BlockSpec((1, T, gw), lambda b, j: (b, 0, j)),
                  pl.BlockSpec((1, T, gw), lambda b, j: (b, 0, G + j)),
                  pl.BlockSpec((1, gw, gw), lambda b, j: (j, 0, 0)),
                  pl.BlockSpec((1, gw), lambda b, j: (0, j))],
        out_specs=pl.BlockSpec((1, T, gw), lambda b, j: (b, 0, j)),
        out_shape=jax.ShapeDtypeStruct((B, T, G * gw), BF16),
        scratch_shapes=[pltpu.VMEM((T, gw), BF16)],
        compiler_params=_cparams(("parallel", "parallel")),
        name="pool_mixer",
    )(p, p, w_pool, ls.reshape(1, G * gw))


def _out_proj_kernel(a_ref, b_ref, w_ref, *refs, n_x, tm, n_ctx, row_off):
    x_refs = refs[:n_x]
    mod_ref, g_ref, o_ref = refs[n_x:]
    ka = a_ref.shape[2]
    sub = OUT_SUB if tm % OUT_SUB == 0 else tm
    sources = _tile_sources(x_refs, tm, n_ctx)
    assert all(n_rows % sub == 0 for _, _, n_rows in sources)

    def residual(r0):
        for ref, first_tile_ref, n_rows in sources:
            if r0 < n_rows:
                x = ref[0, r0:r0 + sub, :]
                if first_tile_ref is not None:
                    x = jnp.where(pl.program_id(1) == 0, first_tile_ref[0, r0:r0 + sub, :], x)
                return x
            r0 -= n_rows

    def project(r0):
        y = jnp.dot(a_ref[0, r0:r0 + sub, :], w_ref[:ka, :], preferred_element_type=F32)
        return y + jnp.dot(b_ref[0, r0:r0 + sub, :], w_ref[ka:, :], preferred_element_type=F32)

    y_next = project(0)
    for r0 in range(0, tm, sub):
        y = y_next
        if r0 + sub < tm:
            y_next = project(r0 + sub)
        yn = y * lax.rsqrt(jnp.mean(y * y, axis=-1, keepdims=True) + EPS) * g_ref[...]
        gt = _row_mod(mod_ref, 2, 5, row_off + pl.program_id(1) * tm + r0, sub, n_ctx)
        o_ref[0, r0:r0 + sub, :] = residual(r0) + gt * yn


def _out_proj(a, b, a_blk, b_blk, w, x, modrows, g, n_ctx, tm, latents_only):
    off = n_ctx // tm if latents_only else 0
    x_ops, x_specs, (B, T, D) = _row_operands(x, tm, lambda bb, r: (bb, r + off), n_ctx)
    half_k = w.shape[0] // 2
    rows_out = T - n_ctx if latents_only else T
    return pl.pallas_call(
        functools.partial(_out_proj_kernel, n_x=len(x_ops), tm=tm, n_ctx=n_ctx, row_off=off * tm),
        grid=(B, rows_out // tm),
        in_specs=[pl.BlockSpec((1, tm, half_k), lambda bb, r: (bb, r + off, a_blk)),
                  pl.BlockSpec((1, tm, half_k), lambda bb, r: (bb, r + off, b_blk)),
                  pl.BlockSpec(w.shape, lambda bb, r: (0, 0), pipeline_mode=pl.Buffered(1))]
        + x_specs + [
                  pl.BlockSpec((1, SUBLANES, D), lambda bb, r: (bb, 0, 0)),
                  pl.BlockSpec((1, D), lambda bb, r: (0, 0))],
        out_specs=pl.BlockSpec((1, tm, D), lambda bb, r: (bb, r, 0)),
        out_shape=jax.ShapeDtypeStruct((B, rows_out, D), F32),
        compiler_params=_cparams(("parallel", "parallel")),
        name="out_proj",
    )(a, b, w, *x_ops, modrows, g.reshape(1, D))


def _row_tile(T, cap):
    best = NORM_SLAB
    for t in range(NORM_SLAB, min(T, cap) + 1, NORM_SLAB):
        if T % t == 0:
            best = t
    return best


def _col_tile(N, cap):
    best = 128
    for t in range(128, min(N, cap) + 1, 128):
        if N % t == 0:
            best = t
    return best


def kernel(x, c, ctx, c_ctx, w_mod, b_mod, g_pre, g_post, ev_w_in, ev_w_out, ev_lambda, ev_subln_g,
           ev_hg_lb_logits, ev_hg_norm_g, od_w_in, od_w_pool, od_scale, od_w_out):
    B, S, D = x.shape
    n_ctx = ctx.shape[1]
    depth = w_mod.shape[0]
    T = n_ctx + S
    W = DA_HEADS * HEAD_W

    xc = (ctx, x)
    tm_pair = max(t for t in range(n_ctx, 768 + 1, n_ctx) if T % t == 0)

    n_rows = -(-(B + 1) // SUBLANES) * SUBLANES
    cvec = jnp.zeros((n_rows, D), F32).at[:B].set(c).at[B].set(c_ctx)
    mod = _modulation(cvec, w_mod, b_mod)

    lb_cum = jnp.cumsum(jax.nn.softmax(ev_hg_lb_logits.astype(F32), axis=0), axis=0)
    lb_all = lb_cum - lb_cum[0]
    cos, sin = _rope_tables(n_ctx, S)

    tm_in = _row_tile(T, 1152)
    tm_out = _row_tile(T, 768)

    for l in range(depth):
        last = l == depth - 1
        tm_l, tm_o = (tm_in, tm_pair) if isinstance(xc, tuple) else (tm_in, n_ctx if last else tm_out)
        ml = mod[l]
        lat = ml[:B].reshape(B, 3, D)
        cx = jnp.broadcast_to(ml[B].reshape(1, 3, D), (B, 3, D))
        modrows = jnp.concatenate([lat, cx, jnp.zeros((B, SUBLANES - 6, D), F32)], axis=1)

        if l % 2 == 0:
            e = l // 2
            assert _col_tile(W, 1024) == W
            p, pf = _norm_proj(xc, modrows, g_pre[l], ev_w_in[e].astype(BF16), 2 * W, n_ctx, tm_l, W,
                               col_order=(0, 1, 2, 3, 4, 7, 8, 5, 6))

            lam_init = 0.8 - 0.6 * math.exp(-0.3 * l)
            lv = ev_lambda[e].astype(F32)
            lam = jnp.exp(jnp.sum(lv[0] * lv[1])) - jnp.exp(jnp.sum(lv[2] * lv[3])) + lam_init
            a = _diff_attention(p, lam.reshape(1), cos, sin, ev_subln_g[e], 1.0 - lam_init, n_ctx)
            lb = lb_all[e].reshape(2, HG_HEADS, HG_K).transpose(1, 0, 2)
            bh = _hgrn(p, pf, lb, ev_hg_norm_g[e], n_ctx, 4)
            w_out = ev_w_out[e].astype(BF16)
            ya, yb, ia, ib = a, bh, 0, 0
        else:
            o = l // 2
            p = _norm_proj(xc, modrows, g_pre[l], od_w_in[o].astype(BF16), 0, n_ctx, tm_l,
                           _col_tile(od_w_in.shape[2], 2048))
            y = _pool_mixer(p, od_w_pool[o].astype(BF16), od_scale[o], n_ctx)
            w_out = od_w_out[o].astype(BF16)
            ya, yb, ia, ib = y, y, 0, 1

        xc = _out_proj(ya, yb, ia, ib, w_out, xc, modrows, g_post[l], n_ctx, tm_o, latents_only=last)
    return xc
```

```python
import functools
import math

import jax
import jax.numpy as jnp
import numpy as np
from jax import lax
from jax.experimental import pallas as pl
from jax.experimental.pallas import tpu as pltpu

DA_HEADS = 8
DA_D = 64
HEAD_W = 2 * DA_D
HG_HEADS = 8
HG_K = 128
GRID_W = 64
ROPE_BASE = 10000.0
EPS = 1e-6
POOL_WINDOWS = (2, 4, 8, 16)
HG_CHUNK = 32
LOG2E = math.log2(math.e)
ATTN_Q_BLOCK = 2304
ATTN_Q_SUB = 128
ATTN_AHEAD = 8
HG_GROUP = 8
HG_PREP_UNROLL = 24
SUBLANES = 8
POOL_SUB = 768
OUT_SUB = 256
NORM_SLAB = 16
NORM_UNROLL = 4

F32 = jnp.float32
BF16 = jnp.bfloat16
VMEM_LIMIT = 56 * 1024 * 1024


def _cparams(sem):
    return pltpu.CompilerParams(dimension_semantics=sem, vmem_limit_bytes=VMEM_LIMIT)


def _sigmoid(x):
    return 1.0 / (1.0 + jnp.exp(-x))


def _mod_kernel(c_ref, w_ref, b_ref, o_ref):
    c = c_ref[...]
    s = c * _sigmoid(c)
    o_ref[0] = jnp.dot(s, w_ref[0], preferred_element_type=F32,
                       precision=lax.Precision.HIGHEST) + b_ref[0]


def _modulation(cvec, w_mod, b_mod):
    L, D, N = w_mod.shape
    R = cvec.shape[0]
    tn = 1024 if N % 1024 == 0 else N
    return pl.pallas_call(
        _mod_kernel,
        grid=(L, N // tn),
        in_specs=[pl.BlockSpec((R, D), lambda l, j: (0, 0)),
                  pl.BlockSpec((1, D, tn), lambda l, j: (l, 0, j)),
                  pl.BlockSpec((1, 1, tn), lambda l, j: (l, 0, j))],
        out_specs=pl.BlockSpec((1, R, tn), lambda l, j: (l, 0, j)),
        out_shape=jax.ShapeDtypeStruct((L, R, N), F32),
        compiler_params=_cparams(("parallel", "parallel")),
        name="modulation",
    )(cvec, w_mod, b_mod.reshape(L, 1, N))


def _row_mod(mod_ref, lat_row, ctx_row, row0, tm, n_ctx):
    rows = row0 + lax.broadcasted_iota(jnp.int32, (tm, 1), 0)
    return jnp.where(rows < n_ctx, mod_ref[0, ctx_row:ctx_row + 1, :], mod_ref[0, lat_row:lat_row + 1, :])


def _tile_sources(x_refs, tm, n_ctx):
    if len(x_refs) == 1:
        return [(x_refs[0], None, tm)]
    src_rows = x_refs[0].shape[1]
    n_ctx_src = n_ctx // src_rows
    return [(ref, x_refs[i] if i < n_ctx_src else None, src_rows) for i, ref in enumerate(x_refs[n_ctx_src:])]


def _norm_proj_kernel(*refs, n_x, tm, n_ctx, n_lo_tiles):
    x_refs = refs[:n_x]
    mod_ref, g_ref, w_ref, o_ref = refs[n_x:n_x + 4]
    rest = refs[n_x + 4:]
    h_scr, gs_scr = rest[-2:]
    r = pl.program_id(1)
    j = pl.program_id(2)

    @pl.when(j == 0)
    def _():
        gs_scr[0:1, :] = g_ref[...] * (1.0 + mod_ref[0, 1:2, :])
        gs_scr[1:2, :] = mod_ref[0, 0:1, :]
        gs_scr[2:3, :] = g_ref[...] * (1.0 + mod_ref[0, 4:5, :])
        gs_scr[3:4, :] = mod_ref[0, 3:4, :]

        row0 = 0
        for ref, first_tile_ref, n_rows in _tile_sources(x_refs, tm, n_ctx):
            def slab(k, carry, ref=ref, first_tile_ref=first_tile_ref, row0=row0):
                r0 = pl.multiple_of(k * NORM_SLAB, NORM_SLAB)
                x = ref[0, pl.ds(r0, NORM_SLAB), :]
                if first_tile_ref is not None:
                    x = jnp.where(r == 0, first_tile_ref[0, pl.ds(r0, NORM_SLAB), :], x)
                base = jnp.where(r * tm + row0 + r0 < n_ctx, 2, 0)
                y = x * lax.rsqrt(jnp.mean(x * x, axis=-1, keepdims=True) + EPS)
                h_scr[pl.ds(row0 + r0, NORM_SLAB), :] = (y * gs_scr[pl.ds(base, 1), :]
                                                         + gs_scr[pl.ds(base + 1, 1), :]).astype(BF16)
                return carry

            lax.fori_loop(0, n_rows // NORM_SLAB, slab, 0, unroll=NORM_UNROLL)
            row0 += n_rows

    if len(rest) == 2:
        o_ref[0] = jnp.dot(h_scr[...], w_ref[...], preferred_element_type=F32).astype(o_ref.dtype)
    else:
        @pl.when(j < n_lo_tiles)
        def _():
            o_ref[0] = jnp.dot(h_scr[...], w_ref[...], preferred_element_type=F32).astype(o_ref.dtype)

        @pl.when(j >= n_lo_tiles)
        def _():
            rest[0][0] = jnp.dot(h_scr[...], w_ref[...], preferred_element_type=F32)


def _row_operands(x, tm, tile_of, src_rows):
    if not isinstance(x, tuple):
        B, T, D = x.shape
        return [x], [pl.BlockSpec((1, tm, D), lambda *g: (*tile_of(*g), 0))], (B, T, D)
    ctx, lat = x
    B, n_ctx, D = ctx.shape
    assert tm % src_rows == 0 and n_ctx % src_rows == 0 and lat.shape[1] % src_rows == 0 and tm >= n_ctx
    q = tm // src_rows
    n_ctx_src = n_ctx // src_rows

    def ctx_block(i):
        return pl.BlockSpec((1, src_rows, D), lambda *g: (tile_of(*g)[0], i, 0))

    def lat_block(i):
        def index(*g):
            b, r = tile_of(*g)
            return b, jnp.maximum(q * r + i - n_ctx_src, 0), 0
        return pl.BlockSpec((1, src_rows, D), index)

    specs = [ctx_block(i) for i in range(n_ctx_src)] + [lat_block(i) for i in range(q)]
    return [ctx] * n_ctx_src + [lat] * q, specs, (B, n_ctx + lat.shape[1], D)


def _norm_proj(x, modrows, g, w, n_f32, n_ctx, tm, tn, col_order=None):
    x_ops, x_specs, (B, T, D) = _row_operands(x, tm, lambda b, r, j: (b, r), math.gcd(tm, n_ctx))
    n_lo = w.shape[1] - n_f32
    assert n_lo % tn == 0 and n_f32 % tn == 0
    lo_tiles, hi_tiles = n_lo // tn, n_f32 // tn
    if col_order is None:
        w_tile = lambda j: j
    else:
        assert sorted(col_order) == list(range(lo_tiles + hi_tiles))
        w_tile = lambda j: sum(jnp.where(j == k, src, 0) for k, src in enumerate(col_order))
    out_specs = [pl.BlockSpec((1, tm, tn), lambda b, r, j: (b, r, jnp.minimum(j, lo_tiles - 1)))]
    out_shape = [jax.ShapeDtypeStruct((B, T, n_lo), BF16)]
    if hi_tiles:
        out_specs.append(pl.BlockSpec((1, tm, tn), lambda b, r, j: (b, r, jnp.maximum(j - lo_tiles, 0))))
        out_shape.append(jax.ShapeDtypeStruct((B, T, n_f32), F32))
    outs = pl.pallas_call(
        functools.partial(_norm_proj_kernel, n_x=len(x_ops), tm=tm, n_ctx=n_ctx, n_lo_tiles=lo_tiles),
        grid=(B, T // tm, lo_tiles + hi_tiles),
        in_specs=x_specs + [
                  pl.BlockSpec((1, SUBLANES, D), lambda b, r, j: (b, 0, 0)),
                  pl.BlockSpec((1, D), lambda b, r, j: (0, 0)),
                  pl.BlockSpec((D, tn), lambda b, r, j: (0, w_tile(j)))],
        out_specs=out_specs,
        out_shape=out_shape,
        scratch_shapes=[pltpu.VMEM((tm, D), BF16), pltpu.VMEM((4, D), F32)],
        compiler_params=_cparams(("parallel", "parallel", "arbitrary")),
        name="norm_proj",
    )(*x_ops, modrows, g.reshape(1, D), w)
    return outs if hi_tiles else outs[0]


def _rope(x, cos, sin_signed):
    lane = lax.broadcasted_iota(jnp.int32, x.shape, 1)
    n = x.shape[1]
    partner = jnp.where(lane % 32 < 16, pltpu.roll(x, n - 16, axis=1), pltpu.roll(x, 16, axis=1))
    return x * cos + partner * sin_signed


def _attn_kernel(lam_ref, q_ref, k_ref, v_ref, gate_ref, cq_ref, sq_ref, ck_ref, sk_ref, g_ref,
                 o_ref, k_scr, vt_scr, *, tq, n_ctx, out_scale):
    i = pl.program_id(2)
    n_keys = k_scr.shape[0]

    @pl.when(i == 0)
    def _():
        k_scr[...] = _rope(k_ref[0].astype(F32), ck_ref[...], sk_ref[...]).astype(BF16)
        vt_scr[...] = v_ref[0].astype(F32).T.astype(BF16)

    qr = _rope(q_ref[0].astype(F32), cq_ref[...], sq_ref[...]) * (DA_D ** -0.5 * LOG2E)
    first_map = lax.broadcasted_iota(jnp.int32, qr.shape, 1) < DA_D
    q_maps = (jnp.where(first_map, qr, 0.0).astype(BF16), jnp.where(first_map, 0.0, qr).astype(BF16))
    lam = lam_ref[0]
    qs = ATTN_Q_SUB

    def attend(n_ctx_rows):
        def scores(r0):
            nk = n_ctx if r0 < n_ctx_rows else n_keys
            q2 = jnp.concatenate([q_maps[0][r0:r0 + qs], q_maps[1][r0:r0 + qs]], axis=0)
            return lax.dot_general(k_scr[:nk, :], q2, (((1,), (1,)), ((), ())),
                                   preferred_element_type=F32)

        starts = list(range(0, tq, qs))
        pending = [scores(r0) for r0 in starts[:ATTN_AHEAD]]
        for n, r0 in enumerate(starts):
            rows = slice(r0, r0 + qs)
            s = pending.pop(0)
            nk = s.shape[0]
            if n + ATTN_AHEAD < len(starts):
                pending.append(scores(starts[n + ATTN_AHEAD]))
            e = jnp.exp2(s - jnp.max(s, axis=0, keepdims=True))
            rinv = 1.0 / jnp.sum(e, axis=0, keepdims=True)
            ov = jnp.dot(vt_scr[:, :nk], e.astype(BF16), preferred_element_type=F32) * rinv
            o = (ov[:, :qs] - lam * ov[:, qs:]).T
            y = o * lax.rsqrt(jnp.mean(o * o, axis=-1, keepdims=True) + EPS) * g_ref[...] * out_scale
            gate = gate_ref[0, rows, :].astype(F32)
            o_ref[0, rows, :] = (y * (gate * _sigmoid(gate))).astype(o_ref.dtype)

    @pl.when(i == 0)
    def _():
        attend(n_ctx)

    @pl.when(i != 0)
    def _():
        attend(0)


def _diff_attention(p, lam, cos, sin, subln_g, out_scale, n_ctx):
    B, T, _ = p.shape
    tq = max(t for t in range(n_ctx, ATTN_Q_BLOCK + 1, n_ctx) if T % t == 0)
    assert n_ctx % ATTN_Q_SUB == 0
    H = DA_HEADS
    head = lambda part: (lambda b, h, i: (b, i, part * H + h))
    head_all = lambda part: (lambda b, h, i: (b, 0, part * H + h))
    return pl.pallas_call(
        functools.partial(_attn_kernel, tq=tq, n_ctx=n_ctx, out_scale=out_scale),
        grid=(B, H, T // tq),
        in_specs=[pl.BlockSpec(memory_space=pltpu.SMEM),
                  pl.BlockSpec((1, tq, HEAD_W), head(0)),
                  pl.BlockSpec((1, T, HEAD_W), head_all(1)),
                  pl.BlockSpec((1, T, HEAD_W), head_all(2)),
                  pl.BlockSpec((1, tq, HEAD_W), head(3)),
                  pl.BlockSpec((tq, HEAD_W), lambda b, h, i: (i, 0)),
                  pl.BlockSpec((tq, HEAD_W), lambda b, h, i: (i, 0)),
                  pl.BlockSpec((T, HEAD_W), lambda b, h, i: (0, 0)),
                  pl.BlockSpec((T, HEAD_W), lambda b, h, i: (0, 0)),
                  pl.BlockSpec((1, HEAD_W), lambda b, h, i: (0, 0))],
        out_specs=pl.BlockSpec((1, tq, HEAD_W), lambda b, h, i: (b, i, h)),
        out_shape=jax.ShapeDtypeStruct((B, T, H * HEAD_W), BF16),
        scratch_shapes=[pltpu.VMEM((T, HEAD_W), BF16), pltpu.VMEM((HEAD_W, T), BF16)],
        compiler_params=_cparams(("parallel", "parallel", "arbitrary")),
        name="diff_attention",
    )(lam, p, p, p, p, cos, sin, cos, sin, subln_g.reshape(1, HEAD_W))


def _rope_tables(n_ctx, seq):
    n_freq = DA_D // 4
    inv = np.power(np.float32(ROPE_BASE), -np.arange(n_freq, dtype=np.float32) / np.float32(n_freq))
    t = np.arange(seq)
    lane = np.arange(HEAD_W)
    pos = np.where((lane % DA_D < DA_D // 2)[None, :], (t // GRID_W)[:, None], (t % GRID_W)[:, None]).astype(np.float32)
    ang = (pos * inv[lane % n_freq][None, :]).astype(np.float64)
    sign = np.where(lane % 32 < 16, -1.0, 1.0)[None, :]
    cos = np.concatenate([np.ones((n_ctx, HEAD_W)), np.cos(ang)], axis=0).astype(np.float32)
    sin = np.concatenate([np.zeros((n_ctx, HEAD_W)), np.sin(ang) * sign], axis=0).astype(np.float32)
    return jnp.asarray(cos), jnp.asarray(sin)


def _chunk_cumsum(g, reverse):
    n = g.shape[0]
    row = lax.broadcasted_iota(jnp.int32, g.shape, 0)
    b = g
    sh = 1
    while sh < n:
        if reverse:
            b = b + jnp.where(row < n - sh, pltpu.roll(b, n - sh, axis=0), 0.0)
        else:
            b = b + jnp.where(row >= sh, pltpu.roll(b, sh, axis=0), 0.0)
        sh *= 2
    return b


def _diag_tiles(qq, b2, c2, reverse):
    half = SUBLANES
    sub = lax.broadcasted_iota(jnp.int32, (half, qq.shape[1]), 0)
    tiles = []
    for s in range(qq.shape[0]):
        sb = s // half
        rows = slice(sb * half, (sb + 1) * half)
        w = qq[rows] * jnp.exp2(b2[rows] - c2[s:s + 1])
        keep = (sub <= s - sb * half) if reverse else (sub >= s - sb * half)
        tiles.append((s, sb, jnp.where(keep, w, 0.0)))
    return tiles


def _off_diag_pieces(qq, b2, c2, reverse):
    C = qq.shape[0]
    pieces = []
    span = SUBLANES
    while span < C:
        for u in range(0, C, 2 * span):
            lo, hi = slice(u, u + span), slice(u + span, u + 2 * span)
            if reverse:
                q_rows, k_rows, ref = lo, hi, b2[u + span:u + span + 1]
            else:
                q_rows, k_rows, ref = hi, lo, b2[u + span - 1:u + span]
            pieces.append((q_rows, k_rows, qq[q_rows] * jnp.exp2(b2[q_rows] - ref), jnp.exp2(ref - c2[k_rows])))
        span *= 2
    return pieces


def _off_diag_rows(chunk):
    levels = int(math.log2(chunk // SUBLANES))
    return chunk * levels


def _lane_sums(tiles, ones_ref):
    n = len(tiles)
    assert n % 2 == 0
    lhs = jnp.concatenate([jnp.concatenate([tiles[2 * i], tiles[2 * i + 1]], axis=1) for i in range(n // 2)], axis=0)
    r = jnp.dot(lhs.astype(BF16), ones_ref[...], preferred_element_type=F32)
    w = tiles[0].shape[1]
    out = []
    for i in range(n // 2):
        rows = slice(i * SUBLANES, (i + 1) * SUBLANES)
        out += [r[rows, :w], r[rows, w:]]
    return out


def _hgrn_kernel(q_ref, v_ref, gate_ref, zf_ref, zb_ref, lb_ref, g_ref, o_ref,
                 qt_scr, oi_scr, ox_scr, dl_scr, u_scr, st_scr, ones_scr, blk_scr, *, n_ctx):
    T = q_ref.shape[1]
    C = HG_CHUNK
    K = HG_K
    nc = T // C
    nc_ctx = n_ctx // C
    same_half = (lax.broadcasted_iota(jnp.int32, (2 * K, 2 * K), 0) < K) == (lax.broadcasted_iota(jnp.int32, (2 * K, 2 * K), 1) < K)
    ones_scr[...] = jnp.where(same_half, 1.0, 0.0).astype(BF16)
    sizes = []
    span = SUBLANES
    while span < C:
        sizes += [span] * (C // (2 * span))
        span *= 2
    bounds = [sum((sizes * 2)[:i]) for i in range(1, 2 * len(sizes))]
    n_off = blk_scr.shape[0]
    row_i = lax.broadcasted_iota(jnp.int32, (n_off, n_off), 0)
    col_i = lax.broadcasted_iota(jnp.int32, (n_off, n_off), 1)
    piece_of_row = sum((row_i >= b).astype(jnp.int32) for b in bounds)
    piece_of_col = sum((col_i >= b).astype(jnp.int32) for b in bounds)
    blk_scr[...] = jnp.where(piece_of_row == piece_of_col, 1.0, 0.0)

    def stage1(c):
        r0 = pl.multiple_of(c * C, C)
        hq = q_ref[0, pl.ds(r0, C), :].astype(F32)
        qq = hq * _sigmoid(hq)
        vb = v_ref[0, pl.ds(r0, C), :]
        v = vb.astype(F32)
        tiles, keys, pieces = [], [], []
        for d, z_ref in enumerate((zf_ref, zb_ref)):
            reverse = d == 1
            z = z_ref[0, pl.ds(r0, C), :]
            lb = lb_ref[0, d:d + 1, :]
            e = jnp.exp(-jnp.abs(z))
            inv = 1.0 / (1.0 + e)
            sig = jnp.where(z >= 0, inv, e * inv)
            nsig = jnp.where(z >= 0, e * inv, inv)
            lk2 = jnp.log2(1.0 - lb) + jnp.log2(nsig)
            b2 = _chunk_cumsum(jnp.log2(lb + (1.0 - lb) * sig), reverse)
            b2_last = b2[0:1] if reverse else b2[C - 1:C]
            qt_scr[d, pl.ds(r0, C), :] = (qq * jnp.exp2(b2)).astype(BF16)
            keys.append(jnp.exp2(lk2 + (b2_last - b2)).astype(BF16))
            dl_scr[d, pl.ds(c, 1), :] = jnp.exp2(b2_last)
            tiles.append(_diag_tiles(qq, b2, b2 - lk2, reverse))
            pieces.append(_off_diag_pieces(qq, b2, b2 - lk2, reverse))
        sums = _lane_sums([w for per_dir in tiles for _, _, w in per_dir], ones_scr)
        q_off = jnp.concatenate([p[2] for per_dir in pieces for p in per_dir], axis=0).astype(BF16)
        k_off = jnp.concatenate([p[3] for per_dir in pieces for p in per_dir], axis=0).astype(BF16)
        cross = lax.dot_general(q_off, k_off, (((1,), (1,)), ((), ())), preferred_element_type=F32)
        upd = lax.dot_general(vb, jnp.concatenate(keys, axis=1), (((0,), (0,)), ((), ())),
                              preferred_element_type=F32)
        u_scr[0, c] = upd[:, :K]
        u_scr[1, c] = upd[:, K:]
        return r0, v, tiles, sums, pieces, cross

    def stage2(r0, v, tiles, sums, pieces, cross):
        v_off = jnp.concatenate([v[p[1]] for per_dir in pieces for p in per_dir], axis=0).astype(BF16)
        contrib = jnp.dot((cross * blk_scr[...]).astype(BF16), v_off, preferred_element_type=F32)
        k = 0
        off = 0
        for d in range(2):
            parts = [jnp.zeros((SUBLANES, v.shape[1]), F32) for _ in range(C // SUBLANES)]
            for s, blk, _ in tiles[d]:
                parts[blk] = parts[blk] + sums[k] * v[s:s + 1]
                k += 1
            for q_rows, _, qp, _ in pieces[d]:
                for i in range(qp.shape[0] // SUBLANES):
                    blk = q_rows.start // SUBLANES + i
                    parts[blk] = parts[blk] + contrib[off:off + SUBLANES]
                    off += SUBLANES
            oi_scr[d, pl.ds(r0, C), :] = jnp.concatenate(parts, axis=0)

    def prep(t, carry):
        staged = [stage1(t * HG_PREP_UNROLL + j) for j in range(HG_PREP_UNROLL)]
        for args in staged:
            stage2(*args)
        return carry

    assert nc % HG_PREP_UNROLL == 0
    lax.fori_loop(0, nc // HG_PREP_UNROLL, prep, 0)

    st_scr[...] = jnp.zeros_like(st_scr)

    def scan(m, carry):
        i0 = m * HG_GROUP
        cb0 = jnp.where(i0 < nc_ctx, nc_ctx - 1 - i0, nc + nc_ctx - 1 - i0)
        for j in range(HG_GROUP):
            for d, c in enumerate((i0 + j, cb0 - j)):
                r0 = pl.multiple_of(c * C, C)
                st = st_scr[d]
                ox_scr[d, pl.ds(r0, C), :] = jnp.dot(qt_scr[d, pl.ds(r0, C), :], st.T.astype(BF16),
                                                     preferred_element_type=F32)
                st_scr[d] = dl_scr[d, pl.ds(c, 1), :] * st + u_scr[d, c]
        return carry

    assert nc % HG_GROUP == 0 and nc_ctx % HG_GROUP == 0
    lax.fori_loop(0, nc // HG_GROUP, scan, 0)

    o = (oi_scr[0] + ox_scr[0]) + (oi_scr[1] + ox_scr[1])
    y = o * lax.rsqrt(jnp.mean(o * o, axis=-1, keepdims=True) + EPS) * g_ref[...]
    gate = gate_ref[0].astype(F32)
    o_ref[0] = (y * (gate * _sigmoid(gate))).astype(o_ref.dtype)


def _hgrn(p, pf, lb, norm_g, n_ctx, part0):
    B, T, _ = p.shape
    H = HG_HEADS
    nc = T // HG_CHUNK
    part = lambda k: (lambda b, h: (b, 0, k * H + h))
    blk = (1, T, HEAD_W)
    return pl.pallas_call(
        functools.partial(_hgrn_kernel, n_ctx=n_ctx),
        grid=(B, H),
        in_specs=[pl.BlockSpec(blk, part(part0)), pl.BlockSpec(blk, part(part0 + 1)),
                  pl.BlockSpec(blk, part(part0 + 2)),
                  pl.BlockSpec(blk, part(0)), pl.BlockSpec(blk, part(1)),
                  pl.BlockSpec((1, 2, HG_K), lambda b, h: (h, 0, 0)),
                  pl.BlockSpec((1, HEAD_W), lambda b, h: (0, 0))],
        out_specs=pl.BlockSpec(blk, lambda b, h: (b, 0, h)),
        out_shape=jax.ShapeDtypeStruct((B, T, H * HEAD_W), BF16),
        scratch_shapes=[pltpu.VMEM((2, T, HG_K), BF16),
                        pltpu.VMEM((2, T, HEAD_W), F32), pltpu.VMEM((2, T, HEAD_W), F32),
                        pltpu.VMEM((2, nc, HG_K), F32),
                        pltpu.VMEM((2, nc, HEAD_W, HG_K), F32),
                        pltpu.VMEM((2, HEAD_W, HG_K), F32),
                        pltpu.VMEM((2 * HG_K, 2 * HG_K), BF16),
                        pltpu.VMEM((_off_diag_rows(HG_CHUNK), _off_diag_rows(HG_CHUNK)), F32)],
        compiler_params=_cparams(("parallel", "parallel")),
        name="hgrn2",
    )(p, p, p, pf, pf, lb, norm_g.reshape(1, HEAD_W))


def _shift_rows(x, d):
    n = x.shape[0]
    row = lax.broadcasted_iota(jnp.int32, (n, 1), 0)
    rolled = pltpu.roll(x, (-d) % n, axis=0)
    keep = (row + d >= 0) & (row + d < n)
    return jnp.where(keep, rolled, 0.0)


def _pool_minus_identity(x, w):
    n = x.shape[0]
    ahead = w - w // 2
    behind = w // 2
    fwd = x
    span = 1
    while span < ahead:
        fwd = fwd + _shift_rows(fwd, span)
        span *= 2
    bwd = x
    span = 1
    while span < behind:
        bwd = bwd + _shift_rows(bwd, -span)
        span *= 2
    total = fwd + _shift_rows(bwd, -1)
    row = lax.broadcasted_iota(jnp.int32, (n, 1), 0)
    cnt = jnp.minimum(row + ahead, n) - jnp.maximum(row - behind, 0)
    return total / cnt.astype(F32) - x


def _pool_kernel(u_ref, z_ref, w_ref, ls_ref, o_ref, r_scr, *, n_ctx):
    j = pl.program_id(1)
    T = u_ref.shape[1]
    for jj, win in enumerate(POOL_WINDOWS):
        @pl.when(j == jj)
        def _(win=win):
            for lo, hi in ((0, n_ctx), (n_ctx, T)):
                r_scr[lo:hi, :] = _pool_minus_identity(u_ref[0, lo:hi, :].astype(F32), win).astype(BF16)

    sub = POOL_SUB if T % POOL_SUB == 0 else T
    project = lambda r0: jnp.dot(r_scr[r0:r0 + sub, :], w_ref[0], preferred_element_type=F32)
    y_next = project(0)
    for r0 in range(0, T, sub):
        y = y_next
        if r0 + sub < T:
            y_next = project(r0 + sub)
        z = z_ref[0, r0:r0 + sub, :].astype(F32)
        o_ref[0, r0:r0 + sub, :] = (y * ls_ref[...] * (z * _sigmoid(z))).astype(o_ref.dtype)


def _pool_mixer(p, w_pool, ls, n_ctx):
    B, T, two_e = p.shape
    G, gw, _ = w_pool.shape
    assert G == len(POOL_WINDOWS) and two_e == 2 * G * gw
    return pl.pallas_call(
        functools.partial(_pool_kernel, n_ctx=n_ctx),
        grid=(B, G),
        in_specs=[pl.BlockSpec((1, T, gw), lambda b, j: (b, 0, j)),
                  pl.BlockSpec((1, T, gw), lambda b, j: (b, 0, G + j)),
                  pl.BlockSpec((1, gw, gw), lambda b, j: (j, 0, 0)),
                  pl.BlockSpec((1, gw), lambda b, j: (0, j))],
        out_specs=pl.BlockSpec((1, T, gw), lambda b, j: (b, 0, j)),
        out_shape=jax.ShapeDtypeStruct((B, T, G * gw), BF16),
        scratch_shapes=[pltpu.VMEM((T, gw), BF16)],
        compiler_params=_cparams(("parallel", "parallel")),
        name="pool_mixer",
    )(p, p, w_pool, ls.reshape(1, G * gw))


def _out_proj_kernel(a_ref, b_ref, w_ref, *refs, n_x, tm, n_ctx, row_off):
    x_refs = refs[:n_x]
    mod_ref, g_ref, o_ref = refs[n_x:]
    ka = a_ref.shape[2]
    sub = OUT_SUB if tm % OUT_SUB == 0 else tm
    sources = _tile_sources(x_refs, tm, n_ctx)
    assert all(n_rows % sub == 0 for _, _, n_rows in sources)

    def residual(r0):
        for ref, first_tile_ref, n_rows in sources:
            if r0 < n_rows:
                x = ref[0, r0:r0 + sub, :]
                if first_tile_ref is not None:
                    x = jnp.where(pl.program_id(1) == 0, first_tile_ref[0, r0:r0 + sub, :], x)
                return x
            r0 -= n_rows

    def project(r0):
        y = jnp.dot(a_ref[0, r0:r0 + sub, :], w_ref[:ka, :], preferred_element_type=F32)
        return y + jnp.dot(b_ref[0, r0:r0 + sub, :], w_ref[ka:, :], preferred_element_type=F32)

    y_next = project(0)
    for r0 in range(0, tm, sub):
        y = y_next
        if r0 + sub < tm:
            y_next = project(r0 + sub)
        yn = y * lax.rsqrt(jnp.mean(y * y, axis=-1, keepdims=True) + EPS) * g_ref[...]
        gt = _row_mod(mod_ref, 2, 5, row_off + pl.program_id(1) * tm + r0, sub, n_ctx)
        o_ref[0, r0:r0 + sub, :] = residual(r0) + gt * yn


def _out_proj(a, b, a_blk, b_blk, w, x, modrows, g, n_ctx, tm, latents_only):
    off = n_ctx // tm if latents_only else 0
    x_ops, x_specs, (B, T, D) = _row_operands(x, tm, lambda bb, r: (bb, r + off), n_ctx)
    half_k = w.shape[0] // 2
    rows_out = T - n_ctx if latents_only else T
    return pl.pallas_call(
        functools.partial(_out_proj_kernel, n_x=len(x_ops), tm=tm, n_ctx=n_ctx, row_off=off * tm),
        grid=(B, rows_out // tm),
        in_specs=[pl.BlockSpec((1, tm, half_k), lambda bb, r: (bb, r + off, a_blk)),
                  pl.BlockSpec((1, tm, half_k), lambda bb, r: (bb, r + off, b_blk)),
                  pl.BlockSpec(w.shape, lambda bb, r: (0, 0), pipeline_mode=pl.Buffered(1))]
        + x_specs + [
                  pl.BlockSpec((1, SUBLANES, D), lambda bb, r: (bb, 0, 0)),
                  pl.BlockSpec((1, D), lambda bb, r: (0, 0))],
        out_specs=pl.BlockSpec((1, tm, D), lambda bb, r: (bb, r, 0)),
        out_shape=jax.ShapeDtypeStruct((B, rows_out, D), F32),
        compiler_params=_cparams(("parallel", "parallel")),
        name="out_proj",
    )(a, b, w, *x_ops, modrows, g.reshape(1, D))


def _row_tile(T, cap):
    best = NORM_SLAB
    for t in range(NORM_SLAB, min(T, cap) + 1, NORM_SLAB):
        if T % t == 0:
            best = t
    return best


def _col_tile(N, cap):
    best = 128
    for t in range(128, min(N, cap) + 1, 128):
        if N % t == 0:
            best = t
    return best


def kernel(x, c, ctx, c_ctx, w_mod, b_mod, g_pre, g_post, ev_w_in, ev_w_out, ev_lambda, ev_subln_g,
           ev_hg_lb_logits, ev_hg_norm_g, od_w_in, od_w_pool, od_scale, od_w_out):
    B, S, D = x.shape
    n_ctx = ctx.shape[1]
    depth = w_mod.shape[0]
    T = n_ctx + S
    W = DA_HEADS * HEAD_W

    xc = (ctx, x)
    tm_pair = max(t for t in range(n_ctx, 768 + 1, n_ctx) if T % t == 0)

    n_rows = -(-(B + 1) // SUBLANES) * SUBLANES
    cvec = jnp.zeros((n_rows, D), F32).at[:B].set(c).at[B].set(c_ctx)
    mod = _modulation(cvec, w_mod, b_mod)

    lb_cum = jnp.cumsum(jax.nn.softmax(ev_hg_lb_logits.astype(F32), axis=0), axis=0)
    lb_all = lb_cum - lb_cum[0]
    cos, sin = _rope_tables(n_ctx, S)

    tm_in = _row_tile(T, 1152)
    tm_out = _row_tile(T, 768)

    for l in range(depth):
        last = l == depth - 1
        tm_l, tm_o = (tm_in, tm_pair) if isinstance(xc, tuple) else (tm_in, n_ctx if last else tm_out)
        ml = mod[l]
        lat = ml[:B].reshape(B, 3, D)
        cx = jnp.broadcast_to(ml[B].reshape(1, 3, D), (B, 3, D))
        modrows = jnp.concatenate([lat, cx, jnp.zeros((B, SUBLANES - 6, D), F32)], axis=1)

        if l % 2 == 0:
            e = l // 2
            assert _col_tile(W, 1024) == W
            p, pf = _norm_proj(xc, modrows, g_pre[l], ev_w_in[e].astype(BF16), 2 * W, n_ctx, tm_l, W,
                               col_order=(0, 1, 2, 3, 4, 7, 8, 5, 6))

            lam_init = 0.8 - 0.6 * math.exp(-0.3 * l)
            lv = ev_lambda[e].astype(F32)
            lam = jnp.exp(jnp.sum(lv[0] * lv[1])) - jnp.exp(jnp.sum(lv[2] * lv[3])) + lam_init
            a = _diff_attention(p, lam.reshape(1), cos, sin, ev_subln_g[e], 1.0 - lam_init, n_ctx)
            lb = lb_all[e].reshape(2, HG_HEADS, HG_K).transpose(1, 0, 2)
            bh = _hgrn(p, pf, lb, ev_hg_norm_g[e], n_ctx, 4)
            w_out = ev_w_out[e].astype(BF16)
            ya, yb, ia, ib = a, bh, 0, 0
        else:
            o = l // 2
            p = _norm_proj(xc, modrows, g_pre[l], od_w_in[o].astype(BF16), 0, n_ctx, tm_l,
                           _col_tile(od_w_in.shape[2], 2048))
            y = _pool_mixer(p, od_w_pool[o].astype(BF16), od_scale[o], n_ctx)
            w_out = od_w_out[o].astype(BF16)
            ya, yb, ia, ib = y, y, 0, 1

        xc = _out_proj(ya, yb, ia, ib, w_out, xc, modrows, g_post[l], n_ctx, tm_o, latents_only=last)
    return xc
```
